```python
import math
import jax
import jax.numpy as jnp
from jax import lax
import numpy as np

D_MODEL = 2048
BATCH = 4
SEQ = 2048
DEPTH = 4
DEC_BATCH = 8
DEC_SEQ = 4096
PAST_LEN = 128

HEAD_DIM = 128
N_HEADS = D_MODEL // HEAD_DIM
N_KV_HEADS = N_HEADS // 4
GQA_GROUP = N_HEADS // N_KV_HEADS
DIFF_HEADS = D_MODEL // (2 * HEAD_DIM)
DIFF_KV_HEADS = DIFF_HEADS // 2
DIFF_GROUP = DIFF_HEADS // DIFF_KV_HEADS
D_FF = 5504
CONV_WIDTH = 3
BLOCK = 128
WINDOW = 128
BAND = BLOCK + 2 * WINDOW
DILATIONS = ((128, 1), (512, 4), (2048, 16))
GRID_W = 64
ROPE_THETA = 10000.0
N_MIXERS = 4
N_MOD = 6
EPS = 1e-6
NEG_INF = -1e30
QKV_WIDTH = (N_HEADS + 2 * N_KV_HEADS) * HEAD_DIM
DIFF_QKV_WIDTH = (DIFF_HEADS + 2 * DIFF_KV_HEADS) * 2 * HEAD_DIM

kernel_name = "hybrid_bidir_encoder_interleaved"


def _n_uses(m):
    return (DEPTH - m + N_MIXERS - 1) // N_MIXERS


def _lambda_init(layer):
    return 0.8 - 0.6 * math.exp(-0.3 * layer)


def rms_norm(x, g):
    xf = x.astype(jnp.float32)
    y = xf * lax.rsqrt(jnp.mean(xf * xf, axis=-1, keepdims=True) + EPS)
    return (y * g.astype(jnp.float32)).astype(x.dtype)


def alibi_slopes(n):
    return 2.0 ** (-8.0 * jnp.arange(1, n + 1, dtype=jnp.float32) / n)


def project_qkv(h, w, n_q, n_kv, d_qk, d_v):
    B, S, _ = h.shape
    qkv = h @ w
    q = qkv[..., :n_q * d_qk].reshape(B, S, n_q, d_qk)
    k = qkv[..., n_q * d_qk:(n_q + n_kv) * d_qk].reshape(B, S, n_kv, d_qk)
    v = qkv[..., (n_q + n_kv) * d_qk:].reshape(B, S, n_kv, d_v)
    return q, k, v


def to_blocks(x):
    B, S = x.shape[:2]
    return jnp.moveaxis(x.reshape(B, S // BLOCK, BLOCK, *x.shape[2:]), 1, 0)


def from_blocks(x):
    nb, B = x.shape[:2]
    return jnp.moveaxis(x, 0, 1).reshape(B, nb * BLOCK, *x.shape[3:])


def axial_rope_tables(S):
    rows = S // GRID_W
    row = jnp.repeat(jnp.arange(rows, dtype=jnp.float32), GRID_W)
    col = jnp.tile(jnp.arange(GRID_W, dtype=jnp.float32), rows)
    n_freq = HEAD_DIM // 4
    inv_freq = ROPE_THETA ** (-jnp.arange(n_freq, dtype=jnp.float32) / n_freq)
    ang = jnp.stack([row, col], axis=-1)[..., None] * inv_freq
    return jnp.cos(ang), jnp.sin(ang)


def apply_axial_rope(x, cos, sin):
    B, S, H, _ = x.shape
    xr = x.reshape(B, S, H, 2, 2, HEAD_DIM // 4)
    x1, x2 = xr[..., 0, :], xr[..., 1, :]
    c, s = cos[None, :, None], sin[None, :, None]
    out = jnp.stack([x1 * c - x2 * s, x1 * s + x2 * c], axis=-2)
    return out.reshape(B, S, H, HEAD_DIM).astype(x.dtype)


def mixer_a(h, wqkv, q_norm, k_norm, wo):
    B, S, _ = h.shape
    q, k, v = project_qkv(h, wqkv, N_HEADS, N_KV_HEADS, HEAD_DIM, HEAD_DIM)
    cos, sin = axial_rope_tables(S)
    q = apply_axial_rope(rms_norm(q, q_norm), cos, sin) * HEAD_DIM ** -0.5
    k = apply_axial_rope(rms_norm(k, k_norm), cos, sin)
    q = q.reshape(B, S, N_KV_HEADS, GQA_GROUP, HEAD_DIM)

    def block(qi):
        s = jnp.einsum('bqkgd,bskd->bkgqs', qi, k).astype(jnp.float32)
        p = jax.nn.softmax(s, axis=-1).astype(v.dtype)
        return jnp.einsum('bkgqs,bskd->bqkgd', p, v)

    o = from_blocks(lax.map(block, to_blocks(q)))
    return o.reshape(B, S, N_HEADS * HEAD_DIM) @ wo


def mixer_b(h, wqkv, sink, wo):
    B, S, _ = h.shape
    q, k, v = project_qkv(h, wqkv, N_HEADS, N_KV_HEADS, HEAD_DIM, HEAD_DIM)
    q = (q * HEAD_DIM ** -0.5).reshape(B, S, N_KV_HEADS, GQA_GROUP, HEAD_DIM)
    pad = ((0, 0), (WINDOW, WINDOW), (0, 0), (0, 0))
    kp, vp = jnp.pad(k, pad), jnp.pad(v, pad)
    slopes = alibi_slopes(N_HEADS).reshape(N_KV_HEADS, GQA_GROUP)
    sink_f = sink.astype(jnp.float32).reshape(N_KV_HEADS, GQA_GROUP)
    rel = jnp.arange(BAND)[None, :] - WINDOW - jnp.arange(BLOCK)[:, None]
    alibi = -slopes[:, :, None, None] * jnp.abs(rel).astype(jnp.float32)
    in_window = jnp.abs(rel) <= WINDOW
    starts = jnp.arange(S // BLOCK) * BLOCK

    def block(args):
        qi, start = args
        kb = lax.dynamic_slice_in_dim(kp, start, BAND, axis=1)
        vb = lax.dynamic_slice_in_dim(vp, start, BAND, axis=1)
        key_pos = start - WINDOW + jnp.arange(BAND)
        valid = in_window & ((key_pos >= 0) & (key_pos < S))[None, :]
        s = jnp.einsum('bqkgd,bskd->bkgqs', qi, kb).astype(jnp.float32) + alibi
        s = jnp.where(valid, s, NEG_INF)
        sink_col = jnp.broadcast_to(sink_f[:, :, None, None], s.shape[:-1] + (1,))
        p = jax.nn.softmax(jnp.concatenate([s, sink_col], axis=-1), axis=-1)[..., :-1]
        return jnp.einsum('bkgqs,bskd->bqkgd', p.astype(v.dtype), vb)

    o = from_blocks(lax.map(block, (to_blocks(q), starts)))
    return o.reshape(B, S, N_HEADS * HEAD_DIM) @ wo


def mixer_c(h, wqkv, lam_params, subln, wo, layer):
    B, S, _ = h.shape
    lam_init = _lambda_init(layer)
    q, k, v = project_qkv(h, wqkv, DIFF_HEADS, DIFF_KV_HEADS, 2 * HEAD_DIM, 2 * HEAD_DIM)
    q = (q * HEAD_DIM ** -0.5).reshape(B, S, DIFF_KV_HEADS, DIFF_GROUP, 2, HEAD_DIM)
    k = k.reshape(B, S, DIFF_KV_HEADS, 2, HEAD_DIM)
    lp = lam_params.astype(jnp.float32)
    lam = jnp.exp(jnp.sum(lp[0] * lp[1])) - jnp.exp(jnp.sum(lp[2] * lp[3])) + lam_init
    slopes = alibi_slopes(DIFF_HEADS).reshape(DIFF_KV_HEADS, DIFF_GROUP)
    key_pos = jnp.arange(S)
    starts = jnp.arange(S // BLOCK) * BLOCK

    def block(args):
        qi, start = args
        s = jnp.einsum('bqkgmd,bskmd->bkgmqs', qi, k).astype(jnp.float32)
        dist = jnp.abs(start + jnp.arange(BLOCK)[:, None] - key_pos[None, :]).astype(jnp.float32)
        s = s - slopes[:, :, None, None, None] * dist
        p = jax.nn.softmax(s, axis=-1)
        a = (p[:, :, :, 0] - lam * p[:, :, :, 1]).astype(v.dtype)
        return jnp.einsum('bkgqs,bskd->bqkgd', a, v)

    o = from_blocks(lax.map(block, (to_blocks(q), starts)))
    o = rms_norm(o, subln) * (1.0 - lam_init)
    return o.reshape(B, S, DIFF_HEADS * 2 * HEAD_DIM) @ wo


def mixer_d(h, wqkv, wo):
    B, S, _ = h.shape
    q, k, v = project_qkv(h, wqkv, N_HEADS, N_KV_HEADS, HEAD_DIM, HEAD_DIM)
    q = (q * HEAD_DIM ** -0.5).reshape(B, S, N_KV_HEADS, GQA_GROUP, HEAD_DIM)
    slopes = alibi_slopes(N_HEADS).reshape(N_KV_HEADS, GQA_GROUP)
    starts = jnp.arange(S // BLOCK) * BLOCK

    def block(args):
        qi, start = args
        t = start + jnp.arange(BLOCK)
        outs, lses = [], []
        for window, dil in DILATIONS:
            n_side = (window // 2) // dil
            off = dil * jnp.arange(-n_side, n_side + 1)
            idx = t[:, None] + off[None, :]
            valid = (idx >= 0) & (idx < S)
            idx = jnp.clip(idx, 0, S - 1)
            kg = jnp.take(k, idx, axis=1)
            vg = jnp.take(v, idx, axis=1)
            s = jnp.einsum('bqkgd,bqjkd->bkgqj', qi, kg).astype(jnp.float32)
            s = s - slopes[:, :, None, None] * jnp.abs(off).astype(jnp.float32)
            s = jnp.where(valid, s, NEG_INF)
            lse = jax.nn.logsumexp(s, axis=-1)
            p = jnp.exp(s - lse[..., None]).astype(v.dtype)
            outs.append(jnp.einsum('bkgqj,bqjkd->bqkgd', p, vg))
            lses.append(jnp.moveaxis(lse, -1, 1))
        wgt = jax.nn.softmax(jnp.stack(lses), axis=0).astype(v.dtype)
        return jnp.einsum('nbqkg,nbqkgd->bqkgd', wgt, jnp.stack(outs))

    o = from_blocks(lax.map(block, (to_blocks(q), starts)))
    return o.reshape(B, S, N_HEADS * HEAD_DIM) @ wo


def conv_glu(h, w_up, conv_w, conv_b, w_down):
    S = h.shape[1]
    u = h @ w_up
    a, b = u[..., :D_FF], u[..., D_FF:]
    half = CONV_WIDTH // 2
    ap = jnp.pad(a, ((0, 0), (half, half), (0, 0)))
    a = conv_b + sum(ap[:, j:j + S] * conv_w[j] for j in range(CONV_WIDTH))
    return (jax.nn.gelu(a, approximate=False) * b) @ w_down


def trunk(x, c, p):
    for i in range(DEPTH):
        mod = jax.nn.silu(c) @ p['w_ada'][i] + p['b_ada'][i]
        sh1, sc1, g1, sh2, sc2, g2 = jnp.split(mod[:, None, :], N_MOD, axis=-1)
        h = rms_norm(x, p['norm_attn'][i]) * (1.0 + sc1) + sh1
        m, j = i % N_MIXERS, i // N_MIXERS
        if m == 0:
            y = mixer_a(h, p['a_wqkv'][j], p['a_q_norm'][j], p['a_k_norm'][j], p['a_wo'][j])
        elif m == 1:
            y = mixer_b(h, p['b_wqkv'][j], p['b_sink'][j], p['b_wo'][j])
        elif m == 2:
            y = mixer_c(h, p['c_wqkv'][j], p['c_lambda'][j], p['c_subln'][j], p['c_wo'][j], i)
        else:
            y = mixer_d(h, p['d_wqkv'][j], p['d_wo'][j])
        x = x + g1 * y
        h = rms_norm(x, p['norm_ffn'][i]) * (1.0 + sc2) + sh2
        x = x + g2 * conv_glu(h, p['ffn_w_up'][i], p['ffn_conv_w'][i], p['ffn_conv_b'][i], p['ffn_w_down'][i])
    return rms_norm(x, p['norm_final'])


def _normal(key, shape, scale):
    return scale * jax.random.normal(key, shape, dtype=jnp.float32)


def setup_inputs(seed: int = 0) -> dict:
    key = jax.random.key(seed)
    ks = jax.random.split(key, 26)
    la, lb, lc, ld = (_n_uses(m) for m in range(N_MIXERS))
    D = D_MODEL
    inv = D ** -0.5
    attn_out = (N_HEADS * HEAD_DIM) ** -0.5
    return {
        'x_prompt': _normal(ks[0], (BATCH, SEQ, D), 1.0),
        'x_sample': _normal(ks[1], (DEC_BATCH, DEC_SEQ, D), 1.0),
        'c_prompt': _normal(ks[2], (BATCH, D), 1.0),
        'c_sample': _normal(ks[3], (DEC_BATCH, D), 1.0),
        'norm_attn': 1.0 + _normal(ks[4], (DEPTH, D), 0.05),
        'norm_ffn': 1.0 + _normal(ks[5], (DEPTH, D), 0.05),
        'w_ada': _normal(ks[6], (DEPTH, D, N_MOD * D), 0.5 * inv),
        'b_ada': _normal(ks[7], (DEPTH, N_MOD * D), 0.02),
        'a_wqkv': _normal(ks[8], (la, D, QKV_WIDTH), inv),
        'a_q_norm': 1.0 + _normal(ks[9], (la, HEAD_DIM), 0.05),
        'a_k_norm': 1.0 + _normal(ks[10], (la, HEAD_DIM), 0.05),
        'a_wo': _normal(ks[11], (la, N_HEADS * HEAD_DIM, D), attn_out),
        'b_wqkv': _normal(ks[12], (lb, D, QKV_WIDTH), inv),
        'b_sink': _normal(ks[13], (lb, N_HEADS), 0.5),
        'b_wo': _normal(ks[14], (lb, N_HEADS * HEAD_DIM, D), attn_out),
        'c_wqkv': _normal(ks[15], (lc, D, DIFF_QKV_WIDTH), inv),
        'c_lambda': _normal(ks[16], (lc, 4, HEAD_DIM), 0.1),
        'c_subln': 1.0 + _normal(ks[17], (lc, 2 * HEAD_DIM), 0.05),
        'c_wo': _normal(ks[18], (lc, DIFF_HEADS * 2 * HEAD_DIM, D), attn_out),
        'd_wqkv': _normal(ks[19], (ld, D, QKV_WIDTH), inv),
        'd_wo': _normal(ks[20], (ld, N_HEADS * HEAD_DIM, D), attn_out),
        'ffn_w_up': _normal(ks[21], (DEPTH, D, 2 * D_FF), inv),
        'ffn_conv_w': _normal(ks[22], (DEPTH, CONV_WIDTH, D_FF), CONV_WIDTH ** -0.5),
        'ffn_conv_b': _normal(ks[23], (DEPTH, D_FF), 0.02),
        'ffn_w_down': _normal(ks[24], (DEPTH, D_FF, D), D_FF ** -0.5),
        'norm_final': 1.0 + _normal(ks[25], (D,), 0.05),
    }


def reference(x_prompt, x_sample, c_prompt, c_sample, norm_attn, norm_ffn, w_ada, b_ada,
              a_wqkv, a_q_norm, a_k_norm, a_wo, b_wqkv, b_sink, b_wo,
              c_wqkv, c_lambda, c_subln, c_wo, d_wqkv, d_wo,
              ffn_w_up, ffn_conv_w, ffn_conv_b, ffn_w_down, norm_final):
    p = dict(norm_attn=norm_attn, norm_ffn=norm_ffn, w_ada=w_ada, b_ada=b_ada,
             a_wqkv=a_wqkv, a_q_norm=a_q_norm, a_k_norm=a_k_norm, a_wo=a_wo,
             b_wqkv=b_wqkv, b_sink=b_sink, b_wo=b_wo,
             c_wqkv=c_wqkv, c_lambda=c_lambda, c_subln=c_subln, c_wo=c_wo,
             d_wqkv=d_wqkv, d_wo=d_wo,
             ffn_w_up=ffn_w_up, ffn_conv_w=ffn_conv_w, ffn_conv_b=ffn_conv_b,
             ffn_w_down=ffn_w_down, norm_final=norm_final)
    y_prompt = trunk(x_prompt, c_prompt, p)
    y_sample = trunk(x_sample, c_sample, p)
    return (y_prompt, y_sample)
```

```python
import functools
import math

import jax
import jax.numpy as jnp
from jax import lax
from jax.experimental import pallas as pl
from jax.experimental.pallas import tpu as pltpu

F32 = jnp.float32
BF16 = jnp.bfloat16

D_MODEL = 2048
HEAD_DIM = 128
N_HEADS = 16
N_KV_HEADS = 4
GQA_GROUP = 4
DIFF_HEADS = 8
D_FF = 5504
FF_TILE = 512
D_FF_PAD = 5632
N_MOD = 6
DEPTH = 4
GRID_W = 64
ROPE_THETA = 10000.0
EPS = 1e-6
NEG = -1e30
WINDOW = 128
DIL_REACH = 1024
Q_SCALE = HEAD_DIM ** -0.5
HALO = 16
MIB = 1024 * 1024


def _cparams(sem, vmem_mib):
    return pltpu.CompilerParams(dimension_semantics=sem, vmem_limit_bytes=vmem_mib * MIB)


def _dot(a, b):
    return jnp.dot(a, b, preferred_element_type=F32)


def _dot_nt(a, b):
    return lax.dot_general(a, b, (((1,), (1,)), ((), ())), preferred_element_type=F32)


def _norm_mod(x, g, sc, sh):
    y = x * lax.rsqrt(jnp.mean(x * x, axis=-1, keepdims=True) + EPS) * g
    return y * (1.0 + sc) + sh


def _mod_kernel(c_ref, w_ref, b_ref, o_ref):
    c = c_ref[...]
    a = (c / (1.0 + jnp.exp(-c))).astype(BF16)
    o_ref[0] = _dot(a, w_ref[0].astype(BF16)) + b_ref[0]


def _modulation(c_all, w_ada, b_ada, tn=1024):
    nb, d = c_all.shape
    depth, _, n = w_ada.shape
    return pl.pallas_call(
        _mod_kernel,
        grid=(depth, n // tn),
        in_specs=[
            pl.BlockSpec((nb, d), lambda l, j: (0, 0)),
            pl.BlockSpec((1, d, tn), lambda l, j: (l, 0, j)),
            pl.BlockSpec((1, 1, tn), lambda l, j: (l, 0, j)),
        ],
        out_specs=pl.BlockSpec((1, nb, tn), lambda l, j: (l, 0, j)),
        out_shape=jax.ShapeDtypeStruct((depth, nb, n), F32),
        compiler_params=_cparams(("parallel", "parallel"), 40),
    )(c_all, w_ada, b_ada.reshape(depth, 1, n))


def _norm_matmul_kernel(x_ref, g_ref, sc_ref, sh_ref, w_ref, cs_ref, o_ref, h_ref):
    @pl.when(pl.program_id(1) == 0)
    def _():
        h_ref[...] = _norm_mod(x_ref[...], g_ref[...], sc_ref[0], sh_ref[0]).astype(BF16)

    o_ref[...] = (_dot(h_ref[...], w_ref[...]) * cs_ref[...]).astype(o_ref.dtype)


def _norm_matmul(x2, g, mod, row0, seq, k_sc, k_sh, w, colscale, tm=1024, tn=1024):
    t, d = x2.shape
    n = w.shape[1]
    tm = min(tm, seq)
    brow = lambda i, j: row0 + (i * tm) // seq
    return pl.pallas_call(
        _norm_matmul_kernel,
        grid=(t // tm, n // tn),
        in_specs=[
            pl.BlockSpec((tm, d), lambda i, j: (i, 0)),
            pl.BlockSpec((1, d), lambda i, j: (0, 0)),
            pl.BlockSpec((1, 1, d), lambda i, j: (brow(i, j), 0, k_sc)),
            pl.BlockSpec((1, 1, d), lambda i, j: (brow(i, j), 0, k_sh)),
            pl.BlockSpec((d, tn), lambda i, j: (0, j)),
            pl.BlockSpec((1, tn), lambda i, j: (0, j)),
        ],
        out_specs=pl.BlockSpec((tm, tn), lambda i, j: (i, j)),
        out_shape=jax.ShapeDtypeStruct((t, n), BF16),
        scratch_shapes=[pltpu.VMEM((tm, d), BF16)],
        compiler_params=_cparams(("parallel", "arbitrary"), 48),
    )(x2, g, mod, mod, w, colscale)


def _out_res_kernel(o_ref, w_ref, x_ref, gate_ref, out_ref):
    out_ref[...] = x_ref[...] + gate_ref[0] * _dot(o_ref[...], w_ref[...])


def _out_residual(o2, w, x2, mod, row0, seq, k_gate, tm=1024, tn=1024):
    t, d = x2.shape
    kdim = o2.shape[1]
    tm = min(tm, seq)
    return pl.pallas_call(
        _out_res_kernel,
        grid=(t // tm, d // tn),
        in_specs=[
            pl.BlockSpec((tm, kdim), lambda i, j: (i, 0)),
            pl.BlockSpec((kdim, tn), lambda i, j: (0, j)),
            pl.BlockSpec((tm, tn), lambda i, j: (i, j)),
            pl.BlockSpec((1, 1, tn), lambda i, j: (row0 + (i * tm) // seq, 0, k_gate * (d // tn) + j)),
        ],
        out_specs=pl.BlockSpec((tm, tn), lambda i, j: (i, j)),
        out_shape=jax.ShapeDtypeStruct((t, d), F32),
        compiler_params=_cparams(("parallel", "parallel"), 48),
    )(o2, w, x2, mod)


def _ffn_kernel(xp_ref, x_ref, xn_ref, g_ref, sc_ref, sh_ref, gate_ref, wa_ref, wb_ref,
                cw_ref, cb_ref, wd_ref, out_ref, h_ref, *, tm, seq):
    i = pl.program_id(0)
    f = pl.program_id(1)
    n_ext = tm + 2 * HALO

    @pl.when(f == 0)
    def _():
        g, sc, sh = g_ref[...], sc_ref[0], sh_ref[0]
        at_start = (i * tm) % seq == 0
        at_end = ((i + 1) * tm) % seq == 0
        hp = _norm_mod(xp_ref[...], g, sc, sh)
        hn = _norm_mod(xn_ref[...], g, sc, sh)
        h_ref[0:HALO, :] = jnp.where(at_start, 0.0, hp).astype(BF16)
        h_ref[HALO:HALO + tm, :] = _norm_mod(x_ref[...], g, sc, sh).astype(BF16)
        h_ref[HALO + tm:n_ext, :] = jnp.where(at_end, 0.0, hn).astype(BF16)

    a_ext = _dot(h_ref[...], wa_ref[...])
    b = _dot(h_ref[HALO:HALO + tm, :], wb_ref[...])
    a_prev = pltpu.roll(a_ext, 1, 0)[HALO:HALO + tm]
    a_next = pltpu.roll(a_ext, n_ext - 1, 0)[HALO:HALO + tm]
    a_cur = a_ext[HALO:HALO + tm]
    a = cb_ref[...] + (a_prev * cw_ref[0:1, :] + a_cur * cw_ref[1:2, :] + a_next * cw_ref[2:3, :])
    gelu = 0.5 * a * (1.0 + lax.erf(a * (2.0 ** -0.5)))
    y = _dot((gelu * b).astype(BF16), wd_ref[...])

    @pl.when(f == 0)
    def _():
        out_ref[...] = y

    @pl.when(f > 0)
    def _():
        out_ref[...] += y

    @pl.when(f == pl.num_programs(1) - 1)
    def _():
        out_ref[...] = x_ref[...] + gate_ref[0] * out_ref[...]


def _ffn(x2, g, mod, row0, seq, w_up, conv_w, conv_b, w_down, tm=512, tf=FF_TILE):
    t, d = x2.shape
    dff = w_down.shape[0]
    nf = dff // tf
    tm = min(tm, seq)
    hb = tm // HALO
    brow = lambda i, f: row0 + (i * tm) // seq
    return pl.pallas_call(
        functools.partial(_ffn_kernel, tm=tm, seq=seq),
        grid=(t // tm, nf),
        in_specs=[
            pl.BlockSpec((HALO, d), lambda i, f: (jnp.maximum(i * hb - 1, 0), 0)),
            pl.BlockSpec((tm, d), lambda i, f: (i, 0)),
            pl.BlockSpec((HALO, d), lambda i, f: (jnp.minimum((i + 1) * hb, t // HALO - 1), 0)),
            pl.BlockSpec((1, d), lambda i, f: (0, 0)),
            pl.BlockSpec((1, 1, d), lambda i, f: (brow(i, f), 0, 4)),
            pl.BlockSpec((1, 1, d), lambda i, f: (brow(i, f), 0, 3)),
            pl.BlockSpec((1, 1, d), lambda i, f: (brow(i, f), 0, 5)),
            pl.BlockSpec((d, tf), lambda i, f: (0, f)),
            pl.BlockSpec((d, tf), lambda i, f: (0, nf + f)),
            pl.BlockSpec((3, tf), lambda i, f: (0, f)),
            pl.BlockSpec((1, tf), lambda i, f: (0, f)),
            pl.BlockSpec((tf, d), lambda i, f: (f, 0)),
        ],
        out_specs=pl.BlockSpec((tm, d), lambda i, f: (i, 0)),
        out_shape=jax.ShapeDtypeStruct((t, d), F32),
        scratch_shapes=[pltpu.VMEM((tm + 2 * HALO, d), BF16)],
        compiler_params=_cparams(("parallel", "arbitrary"), 56),
    )(x2, x2, x2, g, mod, mod, mod, w_up, w_up, conv_w, conv_b, w_down)


def _stack_heads(q, width=HEAD_DIM):
    n = q.shape[1] // width
    return jnp.concatenate([q[:, i * width:(i + 1) * width] for i in range(n)], axis=0)


def _unstack_heads(o, n):
    tq = o.shape[0] // n
    return jnp.concatenate([o[i * tq:(i + 1) * tq] for i in range(n)], axis=1)


def _online_update(s, v, m, l, acc):
    m_new = jnp.maximum(m, jnp.max(s, axis=-1, keepdims=True))
    alpha = jnp.exp(m - m_new)
    p = jnp.exp(s - m_new)
    l = alpha * l + jnp.sum(p, axis=-1, keepdims=True)
    acc = alpha * acc + _dot(p.astype(BF16), v)
    return m_new, l, acc


def _softmax_state(rows, width):
    return (jnp.full((rows, 1), NEG, F32), jnp.zeros((rows, 1), F32), jnp.zeros((rows, width), F32))


def _rel_iota(tq, tk):
    return lax.broadcasted_iota(jnp.int32, (tq, tk), 1) - lax.broadcasted_iota(jnp.int32, (tq, tk), 0)


def _attn_specs(seq, tq, q_blk0, k_blk0, v_blk0, kv_width):
    nq = seq // tq
    return [
        pl.BlockSpec((tq, 4 * HEAD_DIM), lambda b, kh, qi: (b * nq + qi, q_blk0 + kh)),
        pl.BlockSpec((seq, kv_width), lambda b, kh, qi: (b, k_blk0 + kh)),
        pl.BlockSpec((seq, kv_width), lambda b, kh, qi: (b, v_blk0 + kh)),
    ], pl.BlockSpec((tq, 4 * HEAD_DIM), lambda b, kh, qi: (b * nq + qi, kh))


_SMEM = pl.BlockSpec(memory_space=pltpu.SMEM)
_ATTN_SEM = ("parallel", "parallel", "arbitrary")


def _prep_a_kernel(qkv_ref, cos_ref, sin_ref, w_ref, o_ref):
    tp = qkv_ref.shape[0]
    cos, sin = cos_ref[...], sin_ref[...]
    lane = lax.broadcasted_iota(jnp.int32, (tp, HEAD_DIM), 1)
    low_half = (lane % (HEAD_DIM // 2)) < (HEAD_DIM // 4)
    for h in range(N_HEADS + N_KV_HEADS):
        cols = slice(h * HEAD_DIM, (h + 1) * HEAD_DIM)
        x = qkv_ref[:, cols].astype(F32)
        is_q = h < N_HEADS
        y = x * lax.rsqrt(jnp.mean(x * x, axis=-1, keepdims=True) + EPS) * (w_ref[0:1, :] if is_q else w_ref[1:2, :])
        partner = jnp.where(low_half, pltpu.roll(y, 3 * HEAD_DIM // 4, 1), pltpu.roll(y, HEAD_DIM // 4, 1))
        r = y * cos + partner * sin
        if is_q:
            r = r * Q_SCALE
        o_ref[:, cols] = r.astype(BF16)


def _prep_a(qkv, cos_t, sin_t, qk_norm, seq, tp=512):
    t = qkv.shape[0]
    tp = min(tp, seq)
    width = (N_HEADS + N_KV_HEADS) * HEAD_DIM
    return pl.pallas_call(
        _prep_a_kernel,
        grid=(t // tp,),
        in_specs=[
            pl.BlockSpec((tp, width), lambda i: (i, 0)),
            pl.BlockSpec((tp, HEAD_DIM), lambda i: (i % (seq // tp), 0)),
            pl.BlockSpec((tp, HEAD_DIM), lambda i: (i % (seq // tp), 0)),
            pl.BlockSpec((2, HEAD_DIM), lambda i: (0, 0)),
        ],
        out_specs=pl.BlockSpec((tp, width), lambda i: (i, 0)),
        out_shape=jax.ShapeDtypeStruct((t, width), BF16),
        compiler_params=_cparams(("parallel",), 32),
    )(qkv, cos_t, sin_t, qk_norm)


def _attn_a_kernel(q_ref, k_ref, v_ref, o_ref, *, tk, seq):
    qs = _stack_heads(q_ref[...])

    def body(c, carry):
        off = pl.multiple_of(c * tk, tk)
        s = _dot_nt(qs, k_ref[pl.ds(off, tk), :])
        return _online_update(s, v_ref[pl.ds(off, tk), :], *carry)

    _, l, acc = lax.fori_loop(0, seq // tk, body, _softmax_state(qs.shape[0], HEAD_DIM))
    o_ref[...] = _unstack_heads(acc / l, GQA_GROUP).astype(o_ref.dtype)


def _attn_a(qk_rot, qkv, batch, seq, tq=128, tk=512):
    in_specs, out_spec = _attn_specs(seq, tq, 0, N_HEADS, N_HEADS + N_KV_HEADS, HEAD_DIM)
    return pl.pallas_call(
        functools.partial(_attn_a_kernel, tk=min(tk, seq), seq=seq),
        grid=(batch, N_KV_HEADS, seq // tq),
        in_specs=in_specs,
        out_specs=out_spec,
        out_shape=jax.ShapeDtypeStruct((batch * seq, D_MODEL), BF16),
        compiler_params=_cparams(_ATTN_SEM, 40),
    )(qk_rot, qk_rot, qkv)


def _attn_b_kernel(slopes_ref, sink_ref, q_ref, k_ref, v_ref, o_ref, *, tq, seq):
    kh, qi = pl.program_id(1), pl.program_id(2)
    band = tq + 2 * WINDOW
    start = qi * tq
    ws = pl.multiple_of(jnp.clip(start - WINDOW, 0, seq - band), WINDOW)
    qs = _stack_heads(q_ref[...])
    rel = jnp.abs(_rel_iota(tq, band) + (ws - start))
    dist = rel.astype(F32)
    in_window = rel <= WINDOW
    bias = jnp.concatenate(
        [jnp.where(in_window, -slopes_ref[kh * GQA_GROUP + g] * dist, NEG) for g in range(GQA_GROUP)], axis=0)
    sink = jnp.concatenate(
        [jnp.full((tq, 1), sink_ref[kh * GQA_GROUP + g], F32) for g in range(GQA_GROUP)], axis=0)
    s = _dot_nt(qs, k_ref[pl.ds(ws, band), :]) + bias
    m = jnp.maximum(jnp.max(s, axis=-1, keepdims=True), sink)
    p = jnp.exp(s - m)
    l = jnp.sum(p, axis=-1, keepdims=True) + jnp.exp(sink - m)
    o = _dot(p.astype(BF16), v_ref[pl.ds(ws, band), :]) / l
    o_ref[...] = _unstack_heads(o, GQA_GROUP).astype(o_ref.dtype)


def _attn_b(qkv, slopes, sink, batch, seq, tq=256):
    in_specs, out_spec = _attn_specs(seq, tq, 0, N_HEADS, N_HEADS + N_KV_HEADS, HEAD_DIM)
    return pl.pallas_call(
        functools.partial(_attn_b_kernel, tq=tq, seq=seq),
        grid=(batch, N_KV_HEADS, seq // tq),
        in_specs=[_SMEM, _SMEM] + in_specs,
        out_specs=out_spec,
        out_shape=jax.ShapeDtypeStruct((batch * seq, D_MODEL), BF16),
        compiler_params=_cparams(_ATTN_SEM, 40),
    )(slopes, sink, qkv, qkv, qkv)


def _attn_c_kernel(slopes_ref, q_ref, k_ref, v_ref, lam_ref, subln_ref, o_ref, *, tq, tk, seq, lam_init):
    kh, qi = pl.program_id(1), pl.program_id(2)
    q = q_ref[...]
    h = HEAD_DIM
    qs0 = jnp.concatenate([q[:, 0:h], q[:, 2 * h:3 * h]], axis=0)
    qs1 = jnp.concatenate([q[:, h:2 * h], q[:, 3 * h:4 * h]], axis=0)
    sl0, sl1 = slopes_ref[2 * kh], slopes_ref[2 * kh + 1]
    rel0 = _rel_iota(tq, tk)

    def body(c, carry):
        off = pl.multiple_of(c * tk, tk)
        k = k_ref[pl.ds(off, tk), :]
        v = v_ref[pl.ds(off, tk), :]
        dist = jnp.abs(rel0 + (off - qi * tq)).astype(F32)
        bias = jnp.concatenate([sl0 * dist, sl1 * dist], axis=0)
        st0 = _online_update(_dot_nt(qs0, k[:, :h]) - bias, v, *carry[:3])
        st1 = _online_update(_dot_nt(qs1, k[:, h:]) - bias, v, *carry[3:])
        return st0 + st1

    init = _softmax_state(2 * tq, 2 * h)
    _, l0, a0, _, l1, a1 = lax.fori_loop(0, seq // tk, body, init + init)
    lp = lam_ref[...]
    lam = (jnp.exp(jnp.sum(lp[0:1] * lp[1:2], axis=-1, keepdims=True))
           - jnp.exp(jnp.sum(lp[2:3] * lp[3:4], axis=-1, keepdims=True)) + lam_init)
    o = a0 / l0 - lam * (a1 / l1)
    y = o * lax.rsqrt(jnp.mean(o * o, axis=-1, keepdims=True) + EPS) * subln_ref[...] * (1.0 - lam_init)
    o_ref[...] = _unstack_heads(y, 2).astype(o_ref.dtype)


def _attn_c(qkv, slopes, lam_params, subln, batch, seq, lam_init, tq=128, tk=512):
    in_specs, out_spec = _attn_specs(seq, tq, 0, 8, 12, 2 * HEAD_DIM)
    return pl.pallas_call(
        functools.partial(_attn_c_kernel, tq=tq, tk=min(tk, seq), seq=seq, lam_init=lam_init),
        grid=(batch, N_KV_HEADS, seq // tq),
        in_specs=[_SMEM] + in_specs + [
            pl.BlockSpec((4, HEAD_DIM), lambda b, kh, qi: (0, 0)),
            pl.BlockSpec((1, 2 * HEAD_DIM), lambda b, kh, qi: (0, 0)),
        ],
        out_specs=out_spec,
        out_shape=jax.ShapeDtypeStruct((batch * seq, D_MODEL), BF16),
        compiler_params=_cparams(_ATTN_SEM, 40),
    )(slopes, qkv, qkv, qkv, lam_params, subln)


def _attn_d_kernel(slopes_ref, q_ref, k_ref, v_ref, o_ref, *, tq, tk, seq):
    kh, qi = pl.program_id(1), pl.program_id(2)
    start = qi * tq
    qs = _stack_heads(q_ref[...])
    rel0 = _rel_iota(tq, tk)
    ln2, ln3 = math.log(2.0), math.log(3.0)

    def body(c, carry):
        off = pl.multiple_of(c * tk, tk)
        a = jnp.abs(rel0 + (off - start))
        count = ((a <= 64).astype(jnp.int32)
                 + (((a & 3) == 0) & (a <= 256)).astype(jnp.int32)
                 + (((a & 15) == 0) & (a <= 1024)).astype(jnp.int32))
        logc = jnp.where(count == 0, NEG, jnp.where(count == 1, 0.0, jnp.where(count == 2, ln2, ln3)))
        dist = a.astype(F32)
        bias = jnp.concatenate(
            [logc - slopes_ref[kh * GQA_GROUP + g] * dist for g in range(GQA_GROUP)], axis=0)
        s = _dot_nt(qs, k_ref[pl.ds(off, tk), :]) + bias
        return _online_update(s, v_ref[pl.ds(off, tk), :], *carry)

    lo = jnp.maximum(start - DIL_REACH, 0) // tk
    hi = (jnp.minimum(start + tq + DIL_REACH, seq) + tk - 1) // tk
    _, l, acc = lax.fori_loop(lo, hi, body, _softmax_state(qs.shape[0], HEAD_DIM))
    o_ref[...] = _unstack_heads(acc / l, GQA_GROUP).astype(o_ref.dtype)


def _attn_d(qkv, slopes, batch, seq, tq=128, tk=512):
    in_specs, out_spec = _attn_specs(seq, tq, 0, N_HEADS, N_HEADS + N_KV_HEADS, HEAD_DIM)
    return pl.pallas_call(
        functools.partial(_attn_d_kernel, tq=tq, tk=min(tk, seq), seq=seq),
        grid=(batch, N_KV_HEADS, seq // tq),
        in_specs=[_SMEM] + in_specs,
        out_specs=out_spec,
        out_shape=jax.ShapeDtypeStruct((batch * seq, D_MODEL), BF16),
        compiler_params=_cparams(_ATTN_SEM, 40),
    )(slopes, qkv, qkv, qkv)


def _final_norm_kernel(x_ref, g_ref, o_ref):
    x = x_ref[...]
    o_ref[...] = x * lax.rsqrt(jnp.mean(x * x, axis=-1, keepdims=True) + EPS) * g_ref[...]


def _final_norm(x2, g, tm=512):
    t, d = x2.shape
    return pl.pallas_call(
        _final_norm_kernel,
        grid=(t // tm,),
        in_specs=[pl.BlockSpec((tm, d), lambda i: (i, 0)), pl.BlockSpec((1, d), lambda i: (0, 0))],
        out_specs=pl.BlockSpec((tm, d), lambda i: (i, 0)),
        out_shape=jax.ShapeDtypeStruct((t, d), F32),
        compiler_params=_cparams(("parallel",), 32),
    )(x2, g)


def _alibi_slopes(n):
    return 2.0 ** (-8.0 * jnp.arange(1, n + 1, dtype=F32) / n)


def _lambda_init(layer):
    return 0.8 - 0.6 * math.exp(-0.3 * layer)


def _rope_tables(seq):
    rows = seq // GRID_W
    row = jnp.repeat(jnp.arange(rows, dtype=F32), GRID_W)
    col = jnp.tile(jnp.arange(GRID_W, dtype=F32), rows)
    n_freq = HEAD_DIM // 4
    inv_freq = ROPE_THETA ** (-jnp.arange(n_freq, dtype=F32) / n_freq)
    ang = jnp.stack([row, col], axis=-1)[..., None] * inv_freq
    cos, sin = jnp.cos(ang), jnp.sin(ang)
    cos_t = jnp.concatenate([cos[:, 0], cos[:, 0], cos[:, 1], cos[:, 1]], axis=-1)
    sin_t = jnp.concatenate([-sin[:, 0], sin[:, 0], -sin[:, 1], sin[:, 1]], axis=-1)
    return cos_t, sin_t


def _q_colscale(n_q_cols, n_cols):
    return jnp.concatenate([jnp.full((1, n_q_cols), Q_SCALE, F32), jnp.ones((1, n_cols - n_q_cols), F32)], axis=1)


def _prepare_weights(p):
    pad = D_FF_PAD - D_FF
    w_up = p['ffn_w_up']
    w_up = jnp.concatenate([jnp.pad(w_up[..., :D_FF], ((0, 0), (0, 0), (0, pad))),
                            jnp.pad(w_up[..., D_FF:], ((0, 0), (0, 0), (0, pad)))], axis=-1).astype(BF16)
    return dict(
        a_wqkv=p['a_wqkv'].astype(BF16), a_wo=p['a_wo'].astype(BF16),
        b_wqkv=p['b_wqkv'].astype(BF16), b_wo=p['b_wo'].astype(BF16),
        c_wqkv=p['c_wqkv'].astype(BF16), c_wo=p['c_wo'].astype(BF16),
        d_wqkv=p['d_wqkv'].astype(BF16), d_wo=p['d_wo'].astype(BF16),
        ffn_w_up=w_up,
        ffn_w_down=jnp.pad(p['ffn_w_down'], ((0, 0), (0, pad), (0, 0))).astype(BF16),
        ffn_conv_w=jnp.pad(p['ffn_conv_w'], ((0, 0), (0, 0), (0, pad))),
        ffn_conv_b=jnp.pad(p['ffn_conv_b'], ((0, 0), (0, pad)))[:, None, :],
    )


def _trunk(x, mod_all, row0, p, w):
    batch, seq, d = x.shape
    x2 = x.reshape(batch * seq, d)
    slopes16, slopes8 = _alibi_slopes(N_HEADS), _alibi_slopes(DIFF_HEADS)
    n_qkv = (N_HEADS + 2 * N_KV_HEADS) * HEAD_DIM
    cs_plain = jnp.ones((1, n_qkv), F32)
    cs_q = _q_colscale(N_HEADS * HEAD_DIM, n_qkv)
    cs_q_diff = _q_colscale(DIFF_HEADS * 2 * HEAD_DIM, 2 * D_MODEL)
    for i in range(DEPTH):
        mod = mod_all[i][:, None, :]
        g_attn = p['norm_attn'][i][None, :]
        m, j = i % 4, i // 4
        if m == 0:
            qkv = _norm_matmul(x2, g_attn, mod, row0, seq, 1, 0, w['a_wqkv'][j], cs_plain)
            cos_t, sin_t = _rope_tables(seq)
            qk_norm = jnp.stack([p['a_q_norm'][j], p['a_k_norm'][j]])
            o = _attn_a(_prep_a(qkv, cos_t, sin_t, qk_norm, seq), qkv, batch, seq)
            wo = w['a_wo'][j]
        elif m == 1:
            qkv = _norm_matmul(x2, g_attn, mod, row0, seq, 1, 0, w['b_wqkv'][j], cs_q)
            o = _attn_b(qkv, slopes16, p['b_sink'][j].astype(F32), batch, seq)
            wo = w['b_wo'][j]
        elif m == 2:
            qkv = _norm_matmul(x2, g_attn, mod, row0, seq, 1, 0, w['c_wqkv'][j], cs_q_diff)
            o = _attn_c(qkv, slopes8, p['c_lambda'][j].astype(F32), p['c_subln'][j][None, :],
                        batch, seq, _lambda_init(i))
            wo = w['c_wo'][j]
        else:
            qkv = _norm_matmul(x2, g_attn, mod, row0, seq, 1, 0, w['d_wqkv'][j], cs_q)
            o = _attn_d(qkv, slopes16, batch, seq)
            wo = w['d_wo'][j]
        x2 = _out_residual(o, wo, x2, mod, row0, seq, 2)
        x2 = _ffn(x2, p['norm_ffn'][i][None, :], mod, row0, seq, w['ffn_w_up'][i], w['ffn_conv_w'][i],
                  w['ffn_conv_b'][i], w['ffn_w_down'][i])
    return _final_norm(x2, p['norm_final'][None, :]).reshape(batch, seq, d)


def kernel(x_prompt, x_sample, c_prompt, c_sample, norm_attn, norm_ffn, w_ada, b_ada, a_wqkv, a_q_norm, a_k_norm, a_wo, b_wqkv, b_sink, b_wo, c_wqkv, c_lambda, c_subln, c_wo, d_wqkv, d_wo, ffn_w_up, ffn_conv_w, ffn_conv_b, ffn_w_down, norm_final):
    p = dict(norm_attn=norm_attn, norm_ffn=norm_ffn, a_q_norm=a_q_norm, a_k_norm=a_k_norm, b_sink=b_sink,
             c_lambda=c_lambda, c_subln=c_subln, norm_final=norm_final,
             a_wqkv=a_wqkv, a_wo=a_wo, b_wqkv=b_wqkv, b_wo=b_wo, c_wqkv=c_wqkv, c_wo=c_wo,
             d_wqkv=d_wqkv, d_wo=d_wo, ffn_w_up=ffn_w_up, ffn_conv_w=ffn_conv_w, ffn_conv_b=ffn_conv_b,
             ffn_w_down=ffn_w_down)
    w = _prepare_weights(p)
    n_prompt, n_sample = c_prompt.shape[0], c_sample.shape[0]
    rows = -(-(n_prompt + n_sample) // 8) * 8
    c_all = jnp.concatenate([c_prompt, c_sample, jnp.zeros((rows - n_prompt - n_sample, D_MODEL), F32)], axis=0)
    mod_all = _modulation(c_all, w_ada, b_ada)
    y_prompt = _trunk(x_prompt, mod_all, 0, p, w)
    y_sample = _trunk(x_sample, mod_all, n_prompt, p, w)
    return (y_prompt, y_sample)
```

```python
import functools
import math

import jax
import jax.numpy as jnp
from jax import lax
from jax.experimental import pallas as pl
from jax.experimental.pallas import tpu as pltpu

F32 = jnp.float32
BF16 = jnp.bfloat16

D_MODEL = 2048
HEAD_DIM = 128
N_HEADS = 16
N_KV_HEADS = 4
GQA_GROUP = 4
DIFF_HEADS = 8
D_FF = 5504
FF_TILE = 512
D_FF_PAD = 5632
N_MOD = 6
DEPTH = 4
GRID_W = 64
ROPE_THETA = 10000.0
EPS = 1e-6
NEG = -1e30
WINDOW = 128
DIL_REACH = 1024
Q_SCALE = HEAD_DIM ** -0.5
LOG2E = math.log2(math.e)
Q_SCALE_LOG2 = Q_SCALE * LOG2E
HALO = 16
MIB = 1024 * 1024


def _cparams(sem, vmem_mib):
    return pltpu.CompilerParams(dimension_semantics=sem, vmem_limit_bytes=vmem_mib * MIB)


def _dot(a, b):
    return jnp.dot(a, b, preferred_element_type=F32)


def _dot_nt(a, b):
    return lax.dot_general(a, b, (((1,), (1,)), ((), ())), preferred_element_type=F32)


def _norm_mod(x, g, sc, sh):
    y = x * lax.rsqrt(jnp.mean(x * x, axis=-1, keepdims=True) + EPS) * g
    return y * (1.0 + sc) + sh


def _mod_kernel(c_ref, w_ref, b_ref, o_ref):
    c = c_ref[...]
    a = (c / (1.0 + jnp.exp(-c))).astype(BF16)
    o_ref[0] = _dot(a, w_ref[0].astype(BF16)) + b_ref[0]


def _modulation(c_all, w_ada, b_ada, tn=1024):
    nb, d = c_all.shape
    depth, _, n = w_ada.shape
    return pl.pallas_call(
        _mod_kernel,
        grid=(depth, n // tn),
        in_specs=[
            pl.BlockSpec((nb, d), lambda l, j: (0, 0)),
            pl.BlockSpec((1, d, tn), lambda l, j: (l, 0, j)),
            pl.BlockSpec((1, 1, tn), lambda l, j: (l, 0, j)),
        ],
        out_specs=pl.BlockSpec((1, nb, tn), lambda l, j: (l, 0, j)),
        out_shape=jax.ShapeDtypeStruct((depth, nb, n), F32),
        compiler_params=_cparams(("parallel", "parallel"), 40),
        name="adaln_modulation",
    )(c_all, w_ada, b_ada.reshape(depth, 1, n))


def _norm_matmul_kernel(x_ref, g_ref, sc_ref, sh_ref, w_ref, cs_ref, o_ref, h_ref):
    @pl.when(pl.program_id(1) == 0)
    def _():
        h_ref[...] = _norm_mod(x_ref[...], g_ref[...], sc_ref[0], sh_ref[0]).astype(BF16)

    o_ref[...] = (_dot(h_ref[...], w_ref[...]) * cs_ref[...]).astype(o_ref.dtype)


def _norm_matmul(x2, g, mod, row0, seq, k_sc, k_sh, w, colscale, tm=1024, tn=1024):
    t, d = x2.shape
    n = w.shape[1]
    tm = min(tm, seq)
    brow = lambda i, j: row0 + (i * tm) // seq
    return pl.pallas_call(
        _norm_matmul_kernel,
        grid=(t // tm, n // tn),
        in_specs=[
            pl.BlockSpec((tm, d), lambda i, j: (i, 0)),
            pl.BlockSpec((1, d), lambda i, j: (0, 0)),
            pl.BlockSpec((1, 1, d), lambda i, j: (brow(i, j), 0, k_sc)),
            pl.BlockSpec((1, 1, d), lambda i, j: (brow(i, j), 0, k_sh)),
            pl.BlockSpec((d, tn), lambda i, j: (0, j)),
            pl.BlockSpec((1, tn), lambda i, j: (0, j)),
        ],
        out_specs=pl.BlockSpec((tm, tn), lambda i, j: (i, j)),
        out_shape=jax.ShapeDtypeStruct((t, n), BF16),
        scratch_shapes=[pltpu.VMEM((tm, d), BF16)],
        compiler_params=_cparams(("parallel", "arbitrary"), 48),
        name="norm_qkv_proj",
    )(x2, g, mod, mod, w, colscale)


def _out_res_kernel(o_ref, w_ref, x_ref, gate_ref, out_ref):
    out_ref[...] = x_ref[...] + gate_ref[0] * _dot(o_ref[...], w_ref[...])


def _out_residual(o2, w, x2, mod, row0, seq, k_gate, tm=1024, tn=1024):
    t, d = x2.shape
    kdim = o2.shape[1]
    tm = min(tm, seq)
    return pl.pallas_call(
        _out_res_kernel,
        grid=(t // tm, d // tn),
        in_specs=[
            pl.BlockSpec((tm, kdim), lambda i, j: (i, 0)),
            pl.BlockSpec((kdim, tn), lambda i, j: (0, j)),
            pl.BlockSpec((tm, tn), lambda i, j: (i, j)),
            pl.BlockSpec((1, 1, tn), lambda i, j: (row0 + (i * tm) // seq, 0, k_gate * (d // tn) + j)),
        ],
        out_specs=pl.BlockSpec((tm, tn), lambda i, j: (i, j)),
        out_shape=jax.ShapeDtypeStruct((t, d), F32),
        compiler_params=_cparams(("parallel", "parallel"), 48),
        name="out_proj_residual",
    )(o2, w, x2, mod)


def _ffn_kernel(xp_ref, x_ref, xn_ref, g_ref, sc_ref, sh_ref, gate_ref, wa_ref, wb_ref,
                cw_ref, cb_ref, wd_ref, out_ref, h_ref, *, tm, seq):
    i = pl.program_id(0)
    f = pl.program_id(1)
    n_ext = tm + 2 * HALO

    @pl.when(f == 0)
    def _():
        g, sc, sh = g_ref[...], sc_ref[0], sh_ref[0]
        at_start = (i * tm) % seq == 0
        at_end = ((i + 1) * tm) % seq == 0
        hp = _norm_mod(xp_ref[...], g, sc, sh)
        hn = _norm_mod(xn_ref[...], g, sc, sh)
        h_ref[0:HALO, :] = jnp.where(at_start, 0.0, hp).astype(BF16)
        h_ref[HALO:HALO + tm, :] = _norm_mod(x_ref[...], g, sc, sh).astype(BF16)
        h_ref[HALO + tm:n_ext, :] = jnp.where(at_end, 0.0, hn).astype(BF16)

    a_ext = _dot(h_ref[...], wa_ref[...])
    b = _dot(h_ref[HALO:HALO + tm, :], wb_ref[...])
    a_prev = pltpu.roll(a_ext, 1, 0)[HALO:HALO + tm]
    a_next = pltpu.roll(a_ext, n_ext - 1, 0)[HALO:HALO + tm]
    a_cur = a_ext[HALO:HALO + tm]
    a = cb_ref[...] + (a_prev * cw_ref[0:1, :] + a_cur * cw_ref[1:2, :] + a_next * cw_ref[2:3, :])
    gelu = 0.5 * a * (1.0 + lax.erf(a * (2.0 ** -0.5)))
    y = _dot((gelu * b).astype(BF16), wd_ref[...])

    @pl.when(f == 0)
    def _():
        out_ref[...] = y

    @pl.when(f > 0)
    def _():
        out_ref[...] += y

    @pl.when(f == pl.num_programs(1) - 1)
    def _():
        out_ref[...] = x_ref[...] + gate_ref[0] * out_ref[...]


def _ffn(x2, g, mod, row0, seq, w_up, conv_w, conv_b, w_down, tm=512, tf=FF_TILE):
    t, d = x2.shape
    dff = w_down.shape[0]
    nf = dff // tf
    tm = min(tm, seq)
    hb = tm // HALO
    brow = lambda i, f: row0 + (i * tm) // seq
    return pl.pallas_call(
        functools.partial(_ffn_kernel, tm=tm, seq=seq),
        grid=(t // tm, nf),
        in_specs=[
            pl.BlockSpec((HALO, d), lambda i, f: (jnp.maximum(i * hb - 1, 0), 0)),
            pl.BlockSpec((tm, d), lambda i, f: (i, 0)),
            pl.BlockSpec((HALO, d), lambda i, f: (jnp.minimum((i + 1) * hb, t // HALO - 1), 0)),
            pl.BlockSpec((1, d), lambda i, f: (0, 0)),
            pl.BlockSpec((1, 1, d), lambda i, f: (brow(i, f), 0, 4)),
            pl.BlockSpec((1, 1, d), lambda i, f: (brow(i, f), 0, 3)),
            pl.BlockSpec((1, 1, d), lambda i, f: (brow(i, f), 0, 5)),
            pl.BlockSpec((d, tf), lambda i, f: (0, f)),
            pl.BlockSpec((d, tf), lambda i, f: (0, nf + f)),
            pl.BlockSpec((3, tf), lambda i, f: (0, f)),
            pl.BlockSpec((1, tf), lambda i, f: (0, f)),
            pl.BlockSpec((tf, d), lambda i, f: (f, 0)),
        ],
        out_specs=pl.BlockSpec((tm, d), lambda i, f: (i, 0)),
        out_shape=jax.ShapeDtypeStruct((t, d), F32),
        scratch_shapes=[pltpu.VMEM((tm + 2 * HALO, d), BF16)],
        compiler_params=_cparams(("parallel", "arbitrary"), 56),
        name="conv_glu_mlp",
    )(x2, x2, x2, g, mod, mod, mod, w_up, w_up, conv_w, conv_b, w_down)


def _stack_heads(q, width=HEAD_DIM):
    n = q.shape[1] // width
    return jnp.concatenate([q[:, i * width:(i + 1) * width] for i in range(n)], axis=0)


def _unstack_heads(o, n):
    tq = o.shape[0] // n
    return jnp.concatenate([o[i * tq:(i + 1) * tq] for i in range(n)], axis=1)


def _rel_iota(tq, tk):
    return lax.broadcasted_iota(jnp.int32, (tq, tk), 1) - lax.broadcasted_iota(jnp.int32, (tq, tk), 0)


def _attn_specs(seq, tq, q_blk0, k_blk0, v_blk0, kv_width, lag):
    nq = seq // tq
    q_row = lambda b, qi: b * nq + jnp.minimum(qi, nq - 1)
    o_row = lambda b, qi: b * nq + jnp.maximum(qi - lag, 0)
    return [
        pl.BlockSpec((tq, 4 * HEAD_DIM), lambda b, kh, qi: (q_row(b, qi), q_blk0 + kh)),
        pl.BlockSpec((seq, kv_width), lambda b, kh, qi: (b, k_blk0 + kh)),
        pl.BlockSpec((seq, kv_width), lambda b, kh, qi: (b, v_blk0 + kh)),
    ], pl.BlockSpec((tq, 4 * HEAD_DIM), lambda b, kh, qi: (o_row(b, qi), kh))


_SMEM = pl.BlockSpec(memory_space=pltpu.SMEM)
_ATTN_SEM = ("parallel", "parallel", "arbitrary")


def _pipeline_step(qi, score_fn, v_ref, s_refs, macc_ref, finish, *, v_width, tk):
    nck, m_rows, _ = s_refs[0].shape

    @pl.when(qi == 0)
    def _():
        s_refs[1][...] = jnp.zeros(s_refs[1].shape, F32)
        macc_ref[1] = jnp.zeros(macc_ref.shape[1:], F32)

    def run(cur, prev):
        s_cur, s_prev = s_refs[cur], s_refs[prev]
        m_prev = jnp.max(macc_ref[prev], axis=-1, keepdims=True)

        def body(c, carry):
            macc, lacc, acc = carry
            p = jnp.exp2(s_prev[c] - m_prev)
            s = score_fn(c)
            s_cur[c] = s
            for j in range(tk // 128):
                macc = jnp.maximum(macc, s[:, j * 128:(j + 1) * 128])
                lacc = lacc + p[:, j * 128:(j + 1) * 128]
            off = pl.multiple_of(c * tk, tk)
            acc = acc + _dot(p.astype(BF16), v_ref[pl.ds(off, tk), :])
            return macc, lacc, acc

        init = (jnp.full((m_rows, 128), NEG, F32), jnp.zeros((m_rows, 128), F32),
                jnp.zeros((m_rows, v_width), F32))
        macc, lacc, acc = lax.fori_loop(0, nck, body, init)
        macc_ref[cur] = macc
        finish(acc, jnp.sum(lacc, axis=-1, keepdims=True))

    @pl.when(qi % 2 == 0)
    def _():
        run(0, 1)

    @pl.when(qi % 2 == 1)
    def _():
        run(1, 0)


def _pipelined_attention(name, kernel_fn, operands, in_specs, out_spec, batch, seq, tq, tk, m_rows, extra_scratch=()):
    nck = seq // tk
    return pl.pallas_call(
        kernel_fn,
        grid=(batch, N_KV_HEADS, seq // tq + 1),
        in_specs=in_specs,
        out_specs=out_spec,
        out_shape=jax.ShapeDtypeStruct((batch * seq, D_MODEL), BF16),
        scratch_shapes=[pltpu.VMEM((nck, m_rows, tk), F32), pltpu.VMEM((nck, m_rows, tk), F32),
                        pltpu.VMEM((2, m_rows, 128), F32), *extra_scratch],
        compiler_params=_cparams(_ATTN_SEM, 52),
        name=name,
    )(*operands)


def _prep_a_kernel(qkv_ref, cos_ref, sin_ref, w_ref, o_ref):
    tp = qkv_ref.shape[0]
    cos, sin = cos_ref[...], sin_ref[...]
    lane = lax.broadcasted_iota(jnp.int32, (tp, HEAD_DIM), 1)
    low_half = (lane % (HEAD_DIM // 2)) < (HEAD_DIM // 4)
    for h in range(N_HEADS + N_KV_HEADS):
        cols = slice(h * HEAD_DIM, (h + 1) * HEAD_DIM)
        x = qkv_ref[:, cols].astype(F32)
        is_q = h < N_HEADS
        y = x * lax.rsqrt(jnp.mean(x * x, axis=-1, keepdims=True) + EPS) * (w_ref[0:1, :] if is_q else w_ref[1:2, :])
        partner = jnp.where(low_half, pltpu.roll(y, 3 * HEAD_DIM // 4, 1), pltpu.roll(y, HEAD_DIM // 4, 1))
        r = y * cos + partner * sin
        if is_q:
            r = r * Q_SCALE_LOG2
        o_ref[:, cols] = r.astype(BF16)


def _prep_a(qkv, cos_t, sin_t, qk_norm, seq, tp=512):
    t = qkv.shape[0]
    tp = min(tp, seq)
    width = (N_HEADS + N_KV_HEADS) * HEAD_DIM
    return pl.pallas_call(
        _prep_a_kernel,
        grid=(t // tp,),
        in_specs=[
            pl.BlockSpec((tp, width), lambda i: (i, 0)),
            pl.BlockSpec((tp, HEAD_DIM), lambda i: (i % (seq // tp), 0)),
            pl.BlockSpec((tp, HEAD_DIM), lambda i: (i % (seq // tp), 0)),
            pl.BlockSpec((2, HEAD_DIM), lambda i: (0, 0)),
        ],
        out_specs=pl.BlockSpec((tp, width), lambda i: (i, 0)),
        out_shape=jax.ShapeDtypeStruct((t, width), BF16),
        compiler_params=_cparams(("parallel",), 32),
        name="rope_qk_norm",
    )(qkv, cos_t, sin_t, qk_norm)


def _attn_a_kernel(q_ref, k_ref, v_ref, o_ref, s0_ref, s1_ref, macc_ref, *, tk):
    qs = _stack_heads(q_ref[...])

    def score(c):
        return _dot_nt(qs, k_ref[pl.ds(pl.multiple_of(c * tk, tk), tk), :])

    def finish(acc, l):
        o_ref[...] = _unstack_heads(acc / l, GQA_GROUP).astype(o_ref.dtype)

    _pipeline_step(pl.program_id(2), score, v_ref, (s0_ref, s1_ref), macc_ref, finish, v_width=HEAD_DIM, tk=tk)


def _attn_a(qk_rot, qkv, batch, seq, tq=128, tk=1024):
    tk = min(tk, seq)
    in_specs, out_spec = _attn_specs(seq, tq, 0, N_HEADS, N_HEADS + N_KV_HEADS, HEAD_DIM, 1)
    return _pipelined_attention("attn_axial", functools.partial(_attn_a_kernel, tk=tk), (qk_rot, qk_rot, qkv),
                                in_specs, out_spec, batch, seq, tq, tk, GQA_GROUP * tq)


def _attn_b_kernel(slopes_ref, sink_ref, q_ref, k_ref, v_ref, o_ref, *, tq, seq):
    kh, qi = pl.program_id(1), pl.program_id(2)
    band = tq + 2 * WINDOW
    start = qi * tq
    ws = pl.multiple_of(jnp.clip(start - WINDOW, 0, seq - band), WINDOW)
    qs = _stack_heads(q_ref[...])
    rel = jnp.abs(_rel_iota(tq, band) + (ws - start))
    dist = rel.astype(F32)
    in_window = rel <= WINDOW
    bias = jnp.concatenate(
        [jnp.where(in_window, -slopes_ref[kh * GQA_GROUP + g] * dist, NEG) for g in range(GQA_GROUP)], axis=0)
    sink = jnp.concatenate(
        [jnp.full((tq, 1), sink_ref[kh * GQA_GROUP + g], F32) for g in range(GQA_GROUP)], axis=0)
    s = _dot_nt(qs, k_ref[pl.ds(ws, band), :]) + bias
    m = jnp.maximum(jnp.max(s, axis=-1, keepdims=True), sink)
    p = jnp.exp(s - m)
    l = jnp.sum(p, axis=-1, keepdims=True) + jnp.exp(sink - m)
    o = _dot(p.astype(BF16), v_ref[pl.ds(ws, band), :]) / l
    o_ref[...] = _unstack_heads(o, GQA_GROUP).astype(o_ref.dtype)


def _attn_b(qkv, slopes, sink, batch, seq, tq=256):
    in_specs, out_spec = _attn_specs(seq, tq, 0, N_HEADS, N_HEADS + N_KV_HEADS, HEAD_DIM, 0)
    return pl.pallas_call(
        functools.partial(_attn_b_kernel, tq=tq, seq=seq),
        grid=(batch, N_KV_HEADS, seq // tq),
        in_specs=[_SMEM, _SMEM] + in_specs,
        out_specs=out_spec,
        out_shape=jax.ShapeDtypeStruct((batch * seq, D_MODEL), BF16),
        compiler_params=_cparams(_ATTN_SEM, 40),
        name="attn_window",
    )(slopes, sink, qkv, qkv, qkv)


def _attn_c_kernel(slopes_ref, q_ref, k_ref, v_ref, lam_ref, subln_ref, o_ref, s0_ref, s1_ref, macc_ref,
                   *, tq, tk, seq, lam_init):
    kh, qi = pl.program_id(1), pl.program_id(2)
    start = jnp.minimum(qi, seq // tq - 1) * tq
    q = q_ref[...]
    h = HEAD_DIM
    qs0 = jnp.concatenate([q[:, 0:h], q[:, 2 * h:3 * h]], axis=0)
    qs1 = jnp.concatenate([q[:, h:2 * h], q[:, 3 * h:4 * h]], axis=0)
    sl0, sl1 = slopes_ref[2 * kh], slopes_ref[2 * kh + 1]
    rel0 = _rel_iota(tq, tk)

    def score(c):
        off = pl.multiple_of(c * tk, tk)
        k = k_ref[pl.ds(off, tk), :]
        dist = jnp.abs(rel0 + (off - start)).astype(F32)
        bias = jnp.concatenate([sl0 * dist, sl1 * dist], axis=0)
        return jnp.concatenate([_dot_nt(qs0, k[:, :h]) - bias, _dot_nt(qs1, k[:, h:]) - bias], axis=0)

    lp = lam_ref[...]
    lam = (jnp.exp(jnp.sum(lp[0:1] * lp[1:2], axis=-1, keepdims=True))
           - jnp.exp(jnp.sum(lp[2:3] * lp[3:4], axis=-1, keepdims=True)) + lam_init)

    def finish(acc, l):
        a = acc / l
        o = a[:2 * tq] - lam * a[2 * tq:]
        y = o * lax.rsqrt(jnp.mean(o * o, axis=-1, keepdims=True) + EPS) * subln_ref[...] * (1.0 - lam_init)
        o_ref[...] = _unstack_heads(y, 2).astype(o_ref.dtype)

    _pipeline_step(qi, score, v_ref, (s0_ref, s1_ref), macc_ref, finish, v_width=2 * h, tk=tk)


def _attn_c(qkv, slopes_log2, lam_params, subln, batch, seq, lam_init, tq=128, tk=1024):
    tk = min(tk, seq)
    in_specs, out_spec = _attn_specs(seq, tq, 0, 8, 12, 2 * HEAD_DIM, 1)
    in_specs = [_SMEM] + in_specs + [
        pl.BlockSpec((4, HEAD_DIM), lambda b, kh, qi: (0, 0)),
        pl.BlockSpec((1, 2 * HEAD_DIM), lambda b, kh, qi: (0, 0)),
    ]
    return _pipelined_attention(
        "attn_diff", functools.partial(_attn_c_kernel, tq=tq, tk=tk, seq=seq, lam_init=lam_init),
        (slopes_log2, qkv, qkv, qkv, lam_params, subln), in_specs, out_spec, batch, seq, tq, tk, 4 * tq)


DIL_BIAS_BLOCKS = 2 * (DIL_REACH // 128 + 2) + 1
DIL_BIAS_CENTRE = DIL_BIAS_BLOCKS // 2


def _attn_d_kernel(slopes_ref, q_ref, k_ref, v_ref, o_ref, s0_ref, s1_ref, macc_ref, bias_ref, *, tq, tk, seq):
    kh, qi = pl.program_id(1), pl.program_id(2)
    start = jnp.minimum(qi, seq // tq - 1) * tq
    qs = _stack_heads(q_ref[...])

    @pl.when(qi == 0)
    def _():
        lane_minus_row = _rel_iota(tq, 128)
        ln2, ln3 = math.log(2.0), math.log(3.0)
        for u in range(DIL_BIAS_BLOCKS):
            a = jnp.abs(lane_minus_row + (u - DIL_BIAS_CENTRE) * 128)
            count = ((a <= 64).astype(jnp.int32)
                     + (((a & 3) == 0) & (a <= 256)).astype(jnp.int32)
                     + (((a & 15) == 0) & (a <= 1024)).astype(jnp.int32))
            logc = jnp.where(count == 0, NEG, jnp.where(count == 1, 0.0, jnp.where(count == 2, ln2, ln3)))
            dist = a.astype(F32)
            for g in range(GQA_GROUP):
                bias_ref[g, u] = (logc - slopes_ref[kh * GQA_GROUP + g] * dist) * LOG2E

    def score(c):
        off = pl.multiple_of(c * tk, tk)
        u0 = (off - start) // 128 + DIL_BIAS_CENTRE
        blocks = [jnp.clip(u0 + j, 0, DIL_BIAS_BLOCKS - 1) for j in range(tk // 128)]
        bias = jnp.concatenate(
            [jnp.concatenate([bias_ref[g, u] for u in blocks], axis=1) for g in range(GQA_GROUP)], axis=0)
        return _dot_nt(qs, k_ref[pl.ds(off, tk), :]) + bias

    def finish(acc, l):
        o_ref[...] = _unstack_heads(acc / l, GQA_GROUP).astype(o_ref.dtype)

    _pipeline_step(qi, score, v_ref, (s0_ref, s1_ref), macc_ref, finish, v_width=HEAD_DIM, tk=tk)


def _attn_d(qkv, slopes, batch, seq, tq=128, tk=1024):
    assert tq == 128, "the bias table is built for 128-row query blocks"
    tk = min(tk, seq)
    in_specs, out_spec = _attn_specs(seq, tq, 0, N_HEADS, N_HEADS + N_KV_HEADS, HEAD_DIM, 1)
    return _pipelined_attention(
        "attn_dilated", functools.partial(_attn_d_kernel, tq=tq, tk=tk, seq=seq), (slopes, qkv, qkv, qkv),
        [_SMEM] + in_specs, out_spec, batch, seq, tq, tk, GQA_GROUP * tq,
        extra_scratch=[pltpu.VMEM((GQA_GROUP, DIL_BIAS_BLOCKS, tq, 128), F32)])


def _final_norm_kernel(x_ref, g_ref, o_ref):
    x = x_ref[...]
    o_ref[...] = x * lax.rsqrt(jnp.mean(x * x, axis=-1, keepdims=True) + EPS) * g_ref[...]


def _final_norm(x2, g, tm=512):
    t, d = x2.shape
    return pl.pallas_call(
        _final_norm_kernel,
        grid=(t // tm,),
        in_specs=[pl.BlockSpec((tm, d), lambda i: (i, 0)), pl.BlockSpec((1, d), lambda i: (0, 0))],
        out_specs=pl.BlockSpec((tm, d), lambda i: (i, 0)),
        out_shape=jax.ShapeDtypeStruct((t, d), F32),
        compiler_params=_cparams(("parallel",), 32),
        name="final_norm",
    )(x2, g)


def _alibi_slopes(n):
    return 2.0 ** (-8.0 * jnp.arange(1, n + 1, dtype=F32) / n)


def _lambda_init(layer):
    return 0.8 - 0.6 * math.exp(-0.3 * layer)


def _rope_tables(seq):
    rows = seq // GRID_W
    row = jnp.repeat(jnp.arange(rows, dtype=F32), GRID_W)
    col = jnp.tile(jnp.arange(GRID_W, dtype=F32), rows)
    n_freq = HEAD_DIM // 4
    inv_freq = ROPE_THETA ** (-jnp.arange(n_freq, dtype=F32) / n_freq)
    ang = jnp.stack([row, col], axis=-1)[..., None] * inv_freq
    cos, sin = jnp.cos(ang), jnp.sin(ang)
    cos_t = jnp.concatenate([cos[:, 0], cos[:, 0], cos[:, 1], cos[:, 1]], axis=-1)
    sin_t = jnp.concatenate([-sin[:, 0], sin[:, 0], -sin[:, 1], sin[:, 1]], axis=-1)
    return cos_t, sin_t


def _q_colscale(scale, n_q_cols, n_cols):
    return jnp.concatenate([jnp.full((1, n_q_cols), scale, F32), jnp.ones((1, n_cols - n_q_cols), F32)], axis=1)


def _prepare_weights(p):
    pad = D_FF_PAD - D_FF
    w_up = p['ffn_w_up']
    w_up = jnp.concatenate([jnp.pad(w_up[..., :D_FF], ((0, 0), (0, 0), (0, pad))),
                            jnp.pad(w_up[..., D_FF:], ((0, 0), (0, 0), (0, pad)))], axis=-1).astype(BF16)
    return dict(
        a_wqkv=p['a_wqkv'].astype(BF16), a_wo=p['a_wo'].astype(BF16),
        b_wqkv=p['b_wqkv'].astype(BF16), b_wo=p['b_wo'].astype(BF16),
        c_wqkv=p['c_wqkv'].astype(BF16), c_wo=p['c_wo'].astype(BF16),
        d_wqkv=p['d_wqkv'].astype(BF16), d_wo=p['d_wo'].astype(BF16),
        ffn_w_up=w_up,
        ffn_w_down=jnp.pad(p['ffn_w_down'], ((0, 0), (0, pad), (0, 0))).astype(BF16),
        ffn_conv_w=jnp.pad(p['ffn_conv_w'], ((0, 0), (0, 0), (0, pad))),
        ffn_conv_b=jnp.pad(p['ffn_conv_b'], ((0, 0), (0, pad)))[:, None, :],
    )


def _trunk(x, mod_all, row0, p, w):
    batch, seq, d = x.shape
    x2 = x.reshape(batch * seq, d)
    slopes16, slopes8 = _alibi_slopes(N_HEADS), _alibi_slopes(DIFF_HEADS)
    n_qkv = (N_HEADS + 2 * N_KV_HEADS) * HEAD_DIM
    cs_plain = jnp.ones((1, n_qkv), F32)
    cs_q = _q_colscale(Q_SCALE, N_HEADS * HEAD_DIM, n_qkv)
    cs_q_log2 = _q_colscale(Q_SCALE_LOG2, N_HEADS * HEAD_DIM, n_qkv)
    cs_q_diff = _q_colscale(Q_SCALE_LOG2, DIFF_HEADS * 2 * HEAD_DIM, 2 * D_MODEL)
    for i in range(DEPTH):
        mod = mod_all[i][:, None, :]
        g_attn = p['norm_attn'][i][None, :]
        m, j = i % 4, i // 4
        if m == 0:
            qkv = _norm_matmul(x2, g_attn, mod, row0, seq, 1, 0, w['a_wqkv'][j], cs_plain)
            cos_t, sin_t = _rope_tables(seq)
            qk_norm = jnp.stack([p['a_q_norm'][j], p['a_k_norm'][j]])
            o = _attn_a(_prep_a(qkv, cos_t, sin_t, qk_norm, seq), qkv, batch, seq)
            wo = w['a_wo'][j]
        elif m == 1:
            qkv = _norm_matmul(x2, g_attn, mod, row0, seq, 1, 0, w['b_wqkv'][j], cs_q)
            o = _attn_b(qkv, slopes16, p['b_sink'][j].astype(F32), batch, seq)
            wo = w['b_wo'][j]
        elif m == 2:
            qkv = _norm_matmul(x2, g_attn, mod, row0, seq, 1, 0, w['c_wqkv'][j], cs_q_diff)
            o = _attn_c(qkv, slopes8 * LOG2E, p['c_lambda'][j].astype(F32), p['c_subln'][j][None, :],
                        batch, seq, _lambda_init(i))
            wo = w['c_wo'][j]
        else:
            qkv = _norm_matmul(x2, g_attn, mod, row0, seq, 1, 0, w['d_wqkv'][j], cs_q_log2)
            o = _attn_d(qkv, slopes16, batch, seq)
            wo = w['d_wo'][j]
        x2 = _out_residual(o, wo, x2, mod, row0, seq, 2)
        x2 = _ffn(x2, p['norm_ffn'][i][None, :], mod, row0, seq, w['ffn_w_up'][i], w['ffn_conv_w'][i],
                  w['ffn_conv_b'][i], w['ffn_w_down'][i])
    return _final_norm(x2, p['norm_final'][None, :]).reshape(batch, seq, d)


def kernel(x_prompt, x_sample, c_prompt, c_sample, norm_attn, norm_ffn, w_ada, b_ada, a_wqkv, a_q_norm, a_k_norm, a_wo, b_wqkv, b_sink, b_wo, c_wqkv, c_lambda, c_subln, c_wo, d_wqkv, d_wo, ffn_w_up, ffn_conv_w, ffn_conv_b, ffn_w_down, norm_final):
    p = dict(norm_attn=norm_attn, norm_ffn=norm_ffn, a_q_norm=a_q_norm, a_k_norm=a_k_norm, b_sink=b_sink,
             c_lambda=c_lambda, c_subln=c_subln, norm_final=norm_final,
             a_wqkv=a_wqkv, a_wo=a_wo, b_wqkv=b_wqkv, b_wo=b_wo, c_wqkv=c_wqkv, c_wo=c_wo,
             d_wqkv=d_wqkv, d_wo=d_wo, ffn_w_up=ffn_w_up, ffn_conv_w=ffn_conv_w, ffn_conv_b=ffn_conv_b,
             ffn_w_down=ffn_w_down)
    w = _prepare_weights(p)
    n_prompt, n_sample = c_prompt.shape[0], c_sample.shape[0]
    rows = -(-(n_prompt + n_sample) // 8) * 8
    c_all = jnp.concatenate([c_prompt, c_sample, jnp.zeros((rows - n_prompt - n_sample, D_MODEL), F32)], axis=0)
    mod_all = _modulation(c_all, w_ada, b_ada)
    y_prompt = _trunk(x_prompt, mod_all, 0, p, w)
    y_sample = _trunk(x_sample, mod_all, n_prompt, p, w)
    return (y_prompt, y_sample)
```

```python
import functools
import math

import jax
import jax.numpy as jnp
from jax import lax
from jax.experimental import pallas as pl
from jax.experimental.pallas import tpu as pltpu

F32 = jnp.float32
BF16 = jnp.bfloat16

D_MODEL = 2048
HEAD_DIM = 128
N_HEADS = 16
N_KV_HEADS = 4
GQA_GROUP = 4
DIFF_HEADS = 8
D_FF = 5504
FF_TILE = 512
D_FF_PAD = 5632
N_MOD = 6
DEPTH = 4
GRID_W = 64
ROPE_THETA = 10000.0
EPS = 1e-6
NEG = -1e30
WINDOW = 128
DILATIONS = ((128, 1), (512, 4), (2048, 16))
Q_SCALE = HEAD_DIM ** -0.5
LOG2E = math.log2(math.e)
Q_SCALE_LOG2 = Q_SCALE * LOG2E
HALO = 16
MIB = 1024 * 1024


def _cparams(sem, vmem_mib):
    return pltpu.CompilerParams(dimension_semantics=sem, vmem_limit_bytes=vmem_mib * MIB)


def _dot(a, b):
    return jnp.dot(a, b, preferred_element_type=F32)


def _dot_nt(a, b):
    return lax.dot_general(a, b, (((1,), (1,)), ((), ())), preferred_element_type=F32)


def _norm_mod(x, g, sc, sh):
    y = x * lax.rsqrt(jnp.mean(x * x, axis=-1, keepdims=True) + EPS) * g
    return y * (1.0 + sc) + sh


def _mod_kernel(c_ref, w_ref, b_ref, o_ref):
    c = c_ref[...]
    a = (c / (1.0 + jnp.exp(-c))).astype(BF16)
    o_ref[0] = _dot(a, w_ref[0].astype(BF16)) + b_ref[0]


def _modulation(c_all, w_ada, b_ada, tn=1024):
    nb, d = c_all.shape
    depth, _, n = w_ada.shape
    return pl.pallas_call(
        _mod_kernel,
        grid=(depth, n // tn),
        in_specs=[
            pl.BlockSpec((nb, d), lambda l, j: (0, 0)),
            pl.BlockSpec((1, d, tn), lambda l, j: (l, 0, j)),
            pl.BlockSpec((1, 1, tn), lambda l, j: (l, 0, j)),
        ],
        out_specs=pl.BlockSpec((1, nb, tn), lambda l, j: (l, 0, j)),
        out_shape=jax.ShapeDtypeStruct((depth, nb, n), F32),
        compiler_params=_cparams(("parallel", "parallel"), 40),
        name="adaln_modulation",
    )(c_all, w_ada, b_ada.reshape(depth, 1, n))


def _norm_matmul_kernel(x_ref, g_ref, sc_ref, sh_ref, w_ref, cs_ref, o_ref, h_ref):
    @pl.when(pl.program_id(1) == 0)
    def _():
        h_ref[...] = _norm_mod(x_ref[...], g_ref[...], sc_ref[0], sh_ref[0]).astype(BF16)

    o_ref[...] = (_dot(h_ref[...], w_ref[...]) * cs_ref[...]).astype(o_ref.dtype)


def _norm_matmul(x2, g, mod, row0, seq, k_sc, k_sh, w, colscale, tm=1024, tn=1024):
    t, d = x2.shape
    n = w.shape[1]
    tm = min(tm, seq)
    brow = lambda i, j: row0 + (i * tm) // seq
    return pl.pallas_call(
        _norm_matmul_kernel,
        grid=(t // tm, n // tn),
        in_specs=[
            pl.BlockSpec((tm, d), lambda i, j: (i, 0)),
            pl.BlockSpec((1, d), lambda i, j: (0, 0)),
            pl.BlockSpec((1, 1, d), lambda i, j: (brow(i, j), 0, k_sc)),
            pl.BlockSpec((1, 1, d), lambda i, j: (brow(i, j), 0, k_sh)),
            pl.BlockSpec((d, tn), lambda i, j: (0, j)),
            pl.BlockSpec((1, tn), lambda i, j: (0, j)),
        ],
        out_specs=pl.BlockSpec((tm, tn), lambda i, j: (i, j)),
        out_shape=jax.ShapeDtypeStruct((t, n), BF16),
        scratch_shapes=[pltpu.VMEM((tm, d), BF16)],
        compiler_params=_cparams(("parallel", "arbitrary"), 48),
        name="norm_qkv_proj",
    )(x2, g, mod, mod, w, colscale)


def _out_res_kernel(o_ref, w_ref, x_ref, gate_ref, out_ref):
    out_ref[...] = x_ref[...] + gate_ref[0] * _dot(o_ref[...], w_ref[...])


def _out_residual(o2, w, x2, mod, row0, seq, k_gate, tm=1024, tn=1024):
    t, d = x2.shape
    kdim = o2.shape[1]
    tm = min(tm, seq)
    return pl.pallas_call(
        _out_res_kernel,
        grid=(t // tm, d // tn),
        in_specs=[
            pl.BlockSpec((tm, kdim), lambda i, j: (i, 0)),
            pl.BlockSpec((kdim, tn), lambda i, j: (0, j)),
            pl.BlockSpec((tm, tn), lambda i, j: (i, j)),
            pl.BlockSpec((1, 1, tn), lambda i, j: (row0 + (i * tm) // seq, 0, k_gate * (d // tn) + j)),
        ],
        out_specs=pl.BlockSpec((tm, tn), lambda i, j: (i, j)),
        out_shape=jax.ShapeDtypeStruct((t, d), F32),
        compiler_params=_cparams(("parallel", "parallel"), 48),
        name="out_proj_residual",
    )(o2, w, x2, mod)


def _ffn_kernel(xp_ref, x_ref, xn_ref, g_ref, sc_ref, sh_ref, gate_ref, wa_ref, wb_ref,
                cw_ref, cb_ref, wd_ref, out_ref, h_ref, *, tm, seq):
    i = pl.program_id(0)
    f = pl.program_id(1)
    n_ext = tm + 2 * HALO

    @pl.when(f == 0)
    def _():
        g, sc, sh = g_ref[...], sc_ref[0], sh_ref[0]
        at_start = (i * tm) % seq == 0
        at_end = ((i + 1) * tm) % seq == 0
        hp = _norm_mod(xp_ref[...], g, sc, sh)
        hn = _norm_mod(xn_ref[...], g, sc, sh)
        h_ref[0:HALO, :] = jnp.where(at_start, 0.0, hp).astype(BF16)
        h_ref[HALO:HALO + tm, :] = _norm_mod(x_ref[...], g, sc, sh).astype(BF16)
        h_ref[HALO + tm:n_ext, :] = jnp.where(at_end, 0.0, hn).astype(BF16)

    a_ext = _dot(h_ref[...], wa_ref[...])
    b = _dot(h_ref[HALO:HALO + tm, :], wb_ref[...])
    a_prev = pltpu.roll(a_ext, 1, 0)[HALO:HALO + tm]
    a_next = pltpu.roll(a_ext, n_ext - 1, 0)[HALO:HALO + tm]
    a_cur = a_ext[HALO:HALO + tm]
    a = cb_ref[...] + (a_prev * cw_ref[0:1, :] + a_cur * cw_ref[1:2, :] + a_next * cw_ref[2:3, :])
    gelu = 0.5 * a * (1.0 + lax.erf(a * (2.0 ** -0.5)))
    y = _dot((gelu * b).astype(BF16), wd_ref[...])

    @pl.when(f == 0)
    def _():
        out_ref[...] = y

    @pl.when(f > 0)
    def _():
        out_ref[...] += y

    @pl.when(f == pl.num_programs(1) - 1)
    def _():
        out_ref[...] = x_ref[...] + gate_ref[0] * out_ref[...]


def _ffn(x2, g, mod, row0, seq, w_up, conv_w, conv_b, w_down, tm=512, tf=FF_TILE):
    t, d = x2.shape
    dff = w_down.shape[0]
    nf = dff // tf
    tm = min(tm, seq)
    hb = tm // HALO
    brow = lambda i, f: row0 + (i * tm) // seq
    return pl.pallas_call(
        functools.partial(_ffn_kernel, tm=tm, seq=seq),
        grid=(t // tm, nf),
        in_specs=[
            pl.BlockSpec((HALO, d), lambda i, f: (jnp.maximum(i * hb - 1, 0), 0)),
            pl.BlockSpec((tm, d), lambda i, f: (i, 0)),
            pl.BlockSpec((HALO, d), lambda i, f: (jnp.minimum((i + 1) * hb, t // HALO - 1), 0)),
            pl.BlockSpec((1, d), lambda i, f: (0, 0)),
            pl.BlockSpec((1, 1, d), lambda i, f: (brow(i, f), 0, 4)),
            pl.BlockSpec((1, 1, d), lambda i, f: (brow(i, f), 0, 3)),
            pl.BlockSpec((1, 1, d), lambda i, f: (brow(i, f), 0, 5)),
            pl.BlockSpec((d, tf), lambda i, f: (0, f)),
            pl.BlockSpec((d, tf), lambda i, f: (0, nf + f)),
            pl.BlockSpec((3, tf), lambda i, f: (0, f)),
            pl.BlockSpec((1, tf), lambda i, f: (0, f)),
            pl.BlockSpec((tf, d), lambda i, f: (f, 0)),
        ],
        out_specs=pl.BlockSpec((tm, d), lambda i, f: (i, 0)),
        out_shape=jax.ShapeDtypeStruct((t, d), F32),
        scratch_shapes=[pltpu.VMEM((tm + 2 * HALO, d), BF16)],
        compiler_params=_cparams(("parallel", "arbitrary"), 56),
        name="conv_glu_mlp",
    )(x2, x2, x2, g, mod, mod, mod, w_up, w_up, conv_w, conv_b, w_down)


def _stack_heads(q, width=HEAD_DIM):
    n = q.shape[1] // width
    return jnp.concatenate([q[:, i * width:(i + 1) * width] for i in range(n)], axis=0)


def _unstack_heads(o, n):
    tq = o.shape[0] // n
    return jnp.concatenate([o[i * tq:(i + 1) * tq] for i in range(n)], axis=1)


def _rel_iota(tq, tk):
    return lax.broadcasted_iota(jnp.int32, (tq, tk), 1) - lax.broadcasted_iota(jnp.int32, (tq, tk), 0)


def _attn_specs(seq, tq, q_blk0, k_blk0, v_blk0, kv_width, lag):
    nq = seq // tq
    q_row = lambda b, qi: b * nq + jnp.minimum(qi, nq - 1)
    o_row = lambda b, qi: b * nq + jnp.maximum(qi - lag, 0)
    return [
        pl.BlockSpec((tq, 4 * HEAD_DIM), lambda b, kh, qi: (q_row(b, qi), q_blk0 + kh)),
        pl.BlockSpec((seq, kv_width), lambda b, kh, qi: (b, k_blk0 + kh)),
        pl.BlockSpec((seq, kv_width), lambda b, kh, qi: (b, v_blk0 + kh)),
    ], pl.BlockSpec((tq, 4 * HEAD_DIM), lambda b, kh, qi: (o_row(b, qi), kh))


_SMEM = pl.BlockSpec(memory_space=pltpu.SMEM)
_ATTN_SEM = ("parallel", "parallel", "arbitrary")


def _pipeline_step(qi, score_fn, v_ref, s_refs, macc_ref, finish, *, v_width, tk):
    nck, m_rows, _ = s_refs[0].shape

    @pl.when(qi == 0)
    def _():
        s_refs[1][...] = jnp.zeros(s_refs[1].shape, F32)
        macc_ref[1] = jnp.zeros(macc_ref.shape[1:], F32)

    def run(cur, prev):
        s_cur, s_prev = s_refs[cur], s_refs[prev]
        m_prev = jnp.max(macc_ref[prev], axis=-1, keepdims=True)

        def body(c, carry):
            macc, lacc, acc = carry
            p = jnp.exp2(s_prev[c] - m_prev)
            s = score_fn(c)
            s_cur[c] = s
            for j in range(tk // 128):
                macc = jnp.maximum(macc, s[:, j * 128:(j + 1) * 128])
                lacc = lacc + p[:, j * 128:(j + 1) * 128]
            off = pl.multiple_of(c * tk, tk)
            acc = acc + _dot(p.astype(BF16), v_ref[pl.ds(off, tk), :])
            return macc, lacc, acc

        init = (jnp.full((m_rows, 128), NEG, F32), jnp.zeros((m_rows, 128), F32),
                jnp.zeros((m_rows, v_width), F32))
        macc, lacc, acc = lax.fori_loop(0, nck, body, init, unroll=True)
        macc_ref[cur] = macc
        finish(acc, jnp.sum(lacc, axis=-1, keepdims=True))

    @pl.when(qi % 2 == 0)
    def _():
        run(0, 1)

    @pl.when(qi % 2 == 1)
    def _():
        run(1, 0)


def _pipelined_attention(name, kernel_fn, operands, in_specs, out_spec, batch, seq, tq, tk, m_rows, extra_scratch=()):
    nck = seq // tk
    return pl.pallas_call(
        kernel_fn,
        grid=(batch, N_KV_HEADS, seq // tq + 1),
        in_specs=in_specs,
        out_specs=out_spec,
        out_shape=jax.ShapeDtypeStruct((batch * seq, D_MODEL), BF16),
        scratch_shapes=[pltpu.VMEM((nck, m_rows, tk), F32), pltpu.VMEM((nck, m_rows, tk), F32),
                        pltpu.VMEM((2, m_rows, 128), F32), *extra_scratch],
        compiler_params=_cparams(_ATTN_SEM, 52),
        name=name,
    )(*operands)


def _prep_a_kernel(qkv_ref, cos_ref, sin_ref, w_ref, o_ref):
    tp = qkv_ref.shape[0]
    cos, sin = cos_ref[...], sin_ref[...]
    lane = lax.broadcasted_iota(jnp.int32, (tp, HEAD_DIM), 1)
    low_half = (lane % (HEAD_DIM // 2)) < (HEAD_DIM // 4)
    for h in range(N_HEADS + N_KV_HEADS):
        cols = slice(h * HEAD_DIM, (h + 1) * HEAD_DIM)
        x = qkv_ref[:, cols].astype(F32)
        is_q = h < N_HEADS
        y = x * lax.rsqrt(jnp.mean(x * x, axis=-1, keepdims=True) + EPS) * (w_ref[0:1, :] if is_q else w_ref[1:2, :])
        partner = jnp.where(low_half, pltpu.roll(y, 3 * HEAD_DIM // 4, 1), pltpu.roll(y, HEAD_DIM // 4, 1))
        r = y * cos + partner * sin
        if is_q:
            r = r * Q_SCALE_LOG2
        o_ref[:, cols] = r.astype(BF16)


def _prep_a(qkv, cos_t, sin_t, qk_norm, seq, tp=512):
    t = qkv.shape[0]
    tp = min(tp, seq)
    width = (N_HEADS + N_KV_HEADS) * HEAD_DIM
    return pl.pallas_call(
        _prep_a_kernel,
        grid=(t // tp,),
        in_specs=[
            pl.BlockSpec((tp, width), lambda i: (i, 0)),
            pl.BlockSpec((tp, HEAD_DIM), lambda i: (i % (seq // tp), 0)),
            pl.BlockSpec((tp, HEAD_DIM), lambda i: (i % (seq // tp), 0)),
            pl.BlockSpec((2, HEAD_DIM), lambda i: (0, 0)),
        ],
        out_specs=pl.BlockSpec((tp, width), lambda i: (i, 0)),
        out_shape=jax.ShapeDtypeStruct((t, width), BF16),
        compiler_params=_cparams(("parallel",), 32),
        name="rope_qk_norm",
    )(qkv, cos_t, sin_t, qk_norm)


def _attn_a_kernel(q_ref, k_ref, v_ref, o_ref, s0_ref, s1_ref, macc_ref, *, tk):
    qs = _stack_heads(q_ref[...])

    def score(c):
        return _dot_nt(qs, k_ref[pl.ds(pl.multiple_of(c * tk, tk), tk), :])

    def finish(acc, l):
        o_ref[...] = _unstack_heads(acc / l, GQA_GROUP).astype(o_ref.dtype)

    _pipeline_step(pl.program_id(2), score, v_ref, (s0_ref, s1_ref), macc_ref, finish, v_width=HEAD_DIM, tk=tk)


def _attn_a(qk_rot, qkv, batch, seq, tq=128, tk=1024):
    tk = min(tk, seq)
    in_specs, out_spec = _attn_specs(seq, tq, 0, N_HEADS, N_HEADS + N_KV_HEADS, HEAD_DIM, 1)
    return _pipelined_attention("attn_axial", functools.partial(_attn_a_kernel, tk=tk), (qk_rot, qk_rot, qkv),
                                in_specs, out_spec, batch, seq, tq, tk, GQA_GROUP * tq)


def _window_kernel(slopes_ref, sink_ref, q_ref, k_ref, v_ref, o_ref, lse_ref, s0_ref, s1_ref, macc_ref,
                   *, tq, band, window, nq, seq_c, dil, use_sink, n_steps):
    n = pl.program_id(0)

    def head_and_block(step):
        return (step // nq) % N_KV_HEADS, step % nq

    def band_start(qi):
        if band == seq_c:
            return 0
        return pl.multiple_of(jnp.clip(qi * tq - window, 0, seq_c - band), window)

    @pl.when(n == 0)
    def _():
        s1_ref[...] = jnp.zeros(s1_ref.shape, F32)
        macc_ref[1] = jnp.zeros(macc_ref.shape[1:], F32)

    def run(cur, prev):
        s_cur, s_prev = (s0_ref, s1_ref) if cur == 0 else (s1_ref, s0_ref)
        kh, qi = head_and_block(jnp.minimum(n, n_steps - 1))
        ws = band_start(qi)
        qs = _stack_heads(q_ref[...])
        rel = jnp.abs(_rel_iota(tq, band) + (ws - qi * tq))
        in_window = rel <= window
        dist = rel.astype(F32)
        bias = jnp.concatenate(
            [jnp.where(in_window, -(slopes_ref[kh * GQA_GROUP + g] * (dil * LOG2E)) * dist, NEG)
             for g in range(GQA_GROUP)], axis=0)
        s = _dot_nt(qs, k_ref[pl.ds(ws, band), :]) + bias
        s_cur[...] = s
        macc = s[:, 0:128]
        for j in range(1, band // 128):
            macc = jnp.maximum(macc, s[:, j * 128:(j + 1) * 128])
        macc_ref[cur] = macc
        khp, qip = head_and_block(jnp.maximum(n - 1, 0))
        wsp = band_start(qip)
        m = jnp.max(macc_ref[prev], axis=-1, keepdims=True)
        if use_sink:
            sink = jnp.concatenate(
                [jnp.full((tq, 1), sink_ref[khp * GQA_GROUP + g] * LOG2E, F32) for g in range(GQA_GROUP)], axis=0)
            m = jnp.maximum(m, sink)
        p = jnp.exp2(s_prev[...] - m)
        lacc = p[:, 0:128]
        for j in range(1, band // 128):
            lacc = lacc + p[:, j * 128:(j + 1) * 128]
        l = jnp.sum(lacc, axis=-1, keepdims=True)
        if use_sink:
            l = l + jnp.exp2(sink - m)
        o = _dot(p.astype(BF16), v_ref[pl.ds(wsp, band), :]) / l
        o_ref[...] = _unstack_heads(o, GQA_GROUP).astype(o_ref.dtype)
        lse = m + jnp.log2(l)
        lse_ref[...] = jnp.concatenate(
            [jnp.broadcast_to(lse[g * tq:(g + 1) * tq], (tq, 128 // GQA_GROUP)) for g in range(GQA_GROUP)], axis=1)

    @pl.when(n % 2 == 0)
    def _():
        run(0, 1)

    @pl.when(n % 2 == 1)
    def _():
        run(1, 0)


def _attn_window(qkv, slopes, sink, batch, seq, *, dil, window, use_sink):
    t, width = qkv.shape
    seq_c = seq // dil
    tq = min(256, seq_c)
    band = min(tq + 2 * window, seq_c)
    nq = seq_c // tq
    assert band % 128 == 0 and seq_c % tq == 0, "bands are reduced in 128-lane blocks"
    n_steps = batch * dil * N_KV_HEADS * nq
    qw = GQA_GROUP * HEAD_DIM

    def split(step):
        qi = step % nq
        rest = step // nq
        kh = rest % N_KV_HEADS
        rest = rest // N_KV_HEADS
        return rest // dil, rest % dil, kh, qi

    def q_map(n):
        b, r, kh, qi = split(jnp.minimum(n, n_steps - 1))
        return b * nq + qi, r * (width // qw) + kh

    def k_map(n):
        b, r, kh, _ = split(jnp.minimum(n, n_steps - 1))
        return b, r * (width // HEAD_DIM) + N_HEADS + kh

    def v_map(n):
        b, r, kh, _ = split(jnp.maximum(n - 1, 0))
        return b, r * (width // HEAD_DIM) + N_HEADS + N_KV_HEADS + kh

    def o_map(n):
        b, r, kh, qi = split(jnp.maximum(n - 1, 0))
        return b * nq + qi, r * N_KV_HEADS + kh

    view = qkv.reshape(t // dil, dil * width)
    o, lse = pl.pallas_call(
        functools.partial(_window_kernel, tq=tq, band=band, window=window, nq=nq, seq_c=seq_c, dil=dil,
                          use_sink=use_sink, n_steps=n_steps),
        grid=(n_steps + 1,),
        in_specs=[_SMEM, _SMEM,
                  pl.BlockSpec((tq, qw), q_map),
                  pl.BlockSpec((seq_c, HEAD_DIM), k_map),
                  pl.BlockSpec((seq_c, HEAD_DIM), v_map)],
        out_specs=[pl.BlockSpec((tq, qw), o_map), pl.BlockSpec((tq, 128), o_map)],
        out_shape=[jax.ShapeDtypeStruct((t // dil, dil * D_MODEL), BF16),
                   jax.ShapeDtypeStruct((t // dil, dil * N_KV_HEADS * 128), F32)],
        scratch_shapes=[pltpu.VMEM((GQA_GROUP * tq, band), F32), pltpu.VMEM((GQA_GROUP * tq, band), F32),
                        pltpu.VMEM((2, GQA_GROUP * tq, 128), F32)],
        compiler_params=_cparams(("arbitrary",), 40),
        name=f"attn_window_d{dil}",
    )(slopes, sink, view, view, view)
    return o.reshape(t, D_MODEL), lse.reshape(t, N_KV_HEADS * 128)


def _attn_c_kernel(slopes_ref, q_ref, k_ref, v_ref, lam_ref, subln_ref, o_ref, s0_ref, s1_ref, macc_ref, bias_ref,
                   *, tq, tk, seq, lam_init):
    kh, qi = pl.program_id(1), pl.program_id(2)
    start = jnp.minimum(qi, seq // tq - 1) * tq
    q = q_ref[...]
    h = HEAD_DIM
    qs0 = jnp.concatenate([q[:, 0:h], q[:, 2 * h:3 * h]], axis=0)
    qs1 = jnp.concatenate([q[:, h:2 * h], q[:, 3 * h:4 * h]], axis=0)
    n_blk = seq // 128

    @pl.when(qi == 0)
    def _():
        lane_minus_row = _rel_iota(tq, 128)
        for u in range(2 * n_blk - 1):
            dist = jnp.abs(lane_minus_row + (u - (n_blk - 1)) * 128).astype(F32)
            for g in range(2):
                bias_ref[g, u] = slopes_ref[2 * kh + g] * dist

    def score(c):
        off = pl.multiple_of(c * tk, tk)
        k = k_ref[pl.ds(off, tk), :]
        u0 = (off - start) // 128 + (n_blk - 1)
        bias = jnp.concatenate(
            [jnp.concatenate([bias_ref[g, u0 + j] for j in range(tk // 128)], axis=1) for g in range(2)], axis=0)
        return jnp.concatenate([_dot_nt(qs0, k[:, :h]) - bias, _dot_nt(qs1, k[:, h:]) - bias], axis=0)

    lp = lam_ref[...]
    lam = (jnp.exp(jnp.sum(lp[0:1] * lp[1:2], axis=-1, keepdims=True))
           - jnp.exp(jnp.sum(lp[2:3] * lp[3:4], axis=-1, keepdims=True)) + lam_init)

    def finish(acc, l):
        a = acc / l
        o = a[:2 * tq] - lam * a[2 * tq:]
        y = o * lax.rsqrt(jnp.mean(o * o, axis=-1, keepdims=True) + EPS) * subln_ref[...] * (1.0 - lam_init)
        o_ref[...] = _unstack_heads(y, 2).astype(o_ref.dtype)

    _pipeline_step(qi, score, v_ref, (s0_ref, s1_ref), macc_ref, finish, v_width=2 * h, tk=tk)


def _attn_c(qkv, slopes_log2, lam_params, subln, batch, seq, lam_init, tq=128, tk=1024):
    assert tq == 128, "the bias table is built for 128-row query blocks"
    tk = min(tk, seq)
    in_specs, out_spec = _attn_specs(seq, tq, 0, 8, 12, 2 * HEAD_DIM, 1)
    in_specs = [_SMEM] + in_specs + [
        pl.BlockSpec((4, HEAD_DIM), lambda b, kh, qi: (0, 0)),
        pl.BlockSpec((1, 2 * HEAD_DIM), lambda b, kh, qi: (0, 0)),
    ]
    return _pipelined_attention(
        "attn_diff", functools.partial(_attn_c_kernel, tq=tq, tk=tk, seq=seq, lam_init=lam_init),
        (slopes_log2, qkv, qkv, qkv, lam_params, subln), in_specs, out_spec, batch, seq, tq, tk, 4 * tq,
        extra_scratch=[pltpu.VMEM((2, 2 * (seq // 128) - 1, tq, 128), F32)])


def _combine_kernel(o1_ref, o2_ref, o3_ref, l1_ref, l2_ref, l3_ref, out_ref):
    lanes = 128 // GQA_GROUP
    for h in range(N_HEADS):
        e = [l_ref[:, h * lanes:h * lanes + 1] for l_ref in (l1_ref, l2_ref, l3_ref)]
        m = jnp.maximum(jnp.maximum(e[0], e[1]), e[2])
        w = [jnp.exp2(x - m) for x in e]
        cols = slice(h * HEAD_DIM, (h + 1) * HEAD_DIM)
        mix = (w[0] * o1_ref[:, cols].astype(F32) + w[1] * o2_ref[:, cols].astype(F32)
               + w[2] * o3_ref[:, cols].astype(F32))
        out_ref[:, cols] = (mix / (w[0] + w[1] + w[2])).astype(out_ref.dtype)


def _attn_d(qkv, slopes, batch, seq, tc=256):
    t = qkv.shape[0]
    dummy_sink = jnp.zeros((N_HEADS,), F32)
    branches = [_attn_window(qkv, slopes, dummy_sink, batch, seq, dil=dil, window=(win // 2) // dil, use_sink=False)
                for win, dil in DILATIONS]
    o_spec = pl.BlockSpec((tc, D_MODEL), lambda i: (i, 0))
    l_spec = pl.BlockSpec((tc, N_KV_HEADS * 128), lambda i: (i, 0))
    return pl.pallas_call(
        _combine_kernel,
        grid=(t // tc,),
        in_specs=[o_spec] * 3 + [l_spec] * 3,
        out_specs=o_spec,
        out_shape=jax.ShapeDtypeStruct((t, D_MODEL), BF16),
        compiler_params=_cparams(("parallel",), 32),
        name="dilated_combine",
    )(*[b[0] for b in branches], *[b[1] for b in branches])


def _final_norm_kernel(x_ref, g_ref, o_ref):
    x = x_ref[...]
    o_ref[...] = x * lax.rsqrt(jnp.mean(x * x, axis=-1, keepdims=True) + EPS) * g_ref[...]


def _final_norm(x2, g, tm=512):
    t, d = x2.shape
    return pl.pallas_call(
        _final_norm_kernel,
        grid=(t // tm,),
        in_specs=[pl.BlockSpec((tm, d), lambda i: (i, 0)), pl.BlockSpec((1, d), lambda i: (0, 0))],
        out_specs=pl.BlockSpec((tm, d), lambda i: (i, 0)),
        out_shape=jax.ShapeDtypeStruct((t, d), F32),
        compiler_params=_cparams(("parallel",), 32),
        name="final_norm",
    )(x2, g)


def _alibi_slopes(n):
    return 2.0 ** (-8.0 * jnp.arange(1, n + 1, dtype=F32) / n)


def _lambda_init(layer):
    return 0.8 - 0.6 * math.exp(-0.3 * layer)


def _rope_tables(seq):
    rows = seq // GRID_W
    row = jnp.repeat(jnp.arange(rows, dtype=F32), GRID_W)
    col = jnp.tile(jnp.arange(GRID_W, dtype=F32), rows)
    n_freq = HEAD_DIM // 4
    inv_freq = ROPE_THETA ** (-jnp.arange(n_freq, dtype=F32) / n_freq)
    ang = jnp.stack([row, col], axis=-1)[..., None] * inv_freq
    cos, sin = jnp.cos(ang), jnp.sin(ang)
    cos_t = jnp.concatenate([cos[:, 0], cos[:, 0], cos[:, 1], cos[:, 1]], axis=-1)
    sin_t = jnp.concatenate([-sin[:, 0], sin[:, 0], -sin[:, 1], sin[:, 1]], axis=-1)
    return cos_t, sin_t


def _q_colscale(scale, n_q_cols, n_cols):
    return jnp.concatenate([jnp.full((1, n_q_cols), scale, F32), jnp.ones((1, n_cols - n_q_cols), F32)], axis=1)


def _prepare_weights(p):
    pad = D_FF_PAD - D_FF
    w_up = p['ffn_w_up']
    w_up = jnp.concatenate([jnp.pad(w_up[..., :D_FF], ((0, 0), (0, 0), (0, pad))),
                            jnp.pad(w_up[..., D_FF:], ((0, 0), (0, 0), (0, pad)))], axis=-1).astype(BF16)
    return dict(
        a_wqkv=p['a_wqkv'].astype(BF16), a_wo=p['a_wo'].astype(BF16),
        b_wqkv=p['b_wqkv'].astype(BF16), b_wo=p['b_wo'].astype(BF16),
        c_wqkv=p['c_wqkv'].astype(BF16), c_wo=p['c_wo'].astype(BF16),
        d_wqkv=p['d_wqkv'].astype(BF16), d_wo=p['d_wo'].astype(BF16),
        ffn_w_up=w_up,
        ffn_w_down=jnp.pad(p['ffn_w_down'], ((0, 0), (0, pad), (0, 0))).astype(BF16),
        ffn_conv_w=jnp.pad(p['ffn_conv_w'], ((0, 0), (0, 0), (0, pad))),
        ffn_conv_b=jnp.pad(p['ffn_conv_b'], ((0, 0), (0, pad)))[:, None, :],
    )


def _trunk(x, mod_all, row0, p, w):
    batch, seq, d = x.shape
    x2 = x.reshape(batch * seq, d)
    slopes16, slopes8 = _alibi_slopes(N_HEADS), _alibi_slopes(DIFF_HEADS)
    n_qkv = (N_HEADS + 2 * N_KV_HEADS) * HEAD_DIM
    cs_plain = jnp.ones((1, n_qkv), F32)
    cs_q_log2 = _q_colscale(Q_SCALE_LOG2, N_HEADS * HEAD_DIM, n_qkv)
    cs_q_diff = _q_colscale(Q_SCALE_LOG2, DIFF_HEADS * 2 * HEAD_DIM, 2 * D_MODEL)
    for i in range(DEPTH):
        mod = mod_all[i][:, None, :]
        g_attn = p['norm_attn'][i][None, :]
        m, j = i % 4, i // 4
        if m == 0:
            qkv = _norm_matmul(x2, g_attn, mod, row0, seq, 1, 0, w['a_wqkv'][j], cs_plain)
            cos_t, sin_t = _rope_tables(seq)
            qk_norm = jnp.stack([p['a_q_norm'][j], p['a_k_norm'][j]])
            o = _attn_a(_prep_a(qkv, cos_t, sin_t, qk_norm, seq), qkv, batch, seq)
            wo = w['a_wo'][j]
        elif m == 1:
            qkv = _norm_matmul(x2, g_attn, mod, row0, seq, 1, 0, w['b_wqkv'][j], cs_q_log2)
            o, _ = _attn_window(qkv, slopes16, p['b_sink'][j].astype(F32), batch, seq, dil=1, window=WINDOW,
                                use_sink=True)
            wo = w['b_wo'][j]
        elif m == 2:
            qkv = _norm_matmul(x2, g_attn, mod, row0, seq, 1, 0, w['c_wqkv'][j], cs_q_diff)
            o = _attn_c(qkv, slopes8 * LOG2E, p['c_lambda'][j].astype(F32), p['c_subln'][j][None, :],
                        batch, seq, _lambda_init(i))
            wo = w['c_wo'][j]
        else:
            qkv = _norm_matmul(x2, g_attn, mod, row0, seq, 1, 0, w['d_wqkv'][j], cs_q_log2)
            o = _attn_d(qkv, slopes16, batch, seq)
            wo = w['d_wo'][j]
        x2 = _out_residual(o, wo, x2, mod, row0, seq, 2)
        x2 = _ffn(x2, p['norm_ffn'][i][None, :], mod, row0, seq, w['ffn_w_up'][i], w['ffn_conv_w'][i],
                  w['ffn_conv_b'][i], w['ffn_w_down'][i])
    return _final_norm(x2, p['norm_final'][None, :]).reshape(batch, seq, d)


def kernel(x_prompt, x_sample, c_prompt, c_sample, norm_attn, norm_ffn, w_ada, b_ada, a_wqkv, a_q_norm, a_k_norm, a_wo, b_wqkv, b_sink, b_wo, c_wqkv, c_lambda, c_subln, c_wo, d_wqkv, d_wo, ffn_w_up, ffn_conv_w, ffn_conv_b, ffn_w_down, norm_final):
    p = dict(norm_attn=norm_attn, norm_ffn=norm_ffn, a_q_norm=a_q_norm, a_k_norm=a_k_norm, b_sink=b_sink,
             c_lambda=c_lambda, c_subln=c_subln, norm_final=norm_final,
             a_wqkv=a_wqkv, a_wo=a_wo, b_wqkv=b_wqkv, b_wo=b_wo, c_wqkv=c_wqkv, c_wo=c_wo,
             d_wqkv=d_wqkv, d_wo=d_wo, ffn_w_up=ffn_w_up, ffn_conv_w=ffn_conv_w, ffn_conv_b=ffn_conv_b,
             ffn_w_down=ffn_w_down)
    w = _prepare_weights(p)
    n_prompt, n_sample = c_prompt.shape[0], c_sample.shape[0]
    rows = -(-(n_prompt + n_sample) // 8) * 8
    c_all = jnp.concatenate([c_prompt, c_sample, jnp.zeros((rows - n_prompt - n_sample, D_MODEL), F32)], axis=0)
    mod_all = _modulation(c_all, w_ada, b_ada)
    y_prompt = _trunk(x_prompt, mod_all, 0, p, w)
    y_sample = _trunk(x_sample, mod_all, n_prompt, p, w)
    return (y_prompt, y_sample)
```

```python
import functools
import math

import jax
import jax.numpy as jnp
from jax import lax
from jax.experimental import pallas as pl
from jax.experimental.pallas import tpu as pltpu

F32 = jnp.float32
BF16 = jnp.bfloat16

D_MODEL = 2048
HEAD_DIM = 128
N_HEADS = 16
N_KV_HEADS = 4
GQA_GROUP = 4
DIFF_HEADS = 8
D_FF = 5504
FF_TILE = 512
D_FF_PAD = 5632
N_MOD = 6
DEPTH = 4
GRID_W = 64
ROPE_THETA = 10000.0
EPS = 1e-6
NEG = -1e30
WINDOW = 128
DILATIONS = ((128, 1), (512, 4), (2048, 16))
Q_SCALE = HEAD_DIM ** -0.5
LOG2E = math.log2(math.e)
Q_SCALE_LOG2 = Q_SCALE * LOG2E
HALO = 16
MIB = 1024 * 1024


def _cparams(sem, vmem_mib):
    return pltpu.CompilerParams(dimension_semantics=sem, vmem_limit_bytes=vmem_mib * MIB)


def _dot(a, b):
    return jnp.dot(a, b, preferred_element_type=F32)


def _dot_nt(a, b):
    return lax.dot_general(a, b, (((1,), (1,)), ((), ())), preferred_element_type=F32)


def _norm_mod(x, g, sc, sh):
    y = x * lax.rsqrt(jnp.mean(x * x, axis=-1, keepdims=True) + EPS) * g
    return y * (1.0 + sc) + sh


def _mod_kernel(c_ref, w_ref, b_ref, o_ref):
    c = c_ref[...]
    a = (c / (1.0 + jnp.exp(-c))).astype(BF16)
    o_ref[0] = _dot(a, w_ref[0].astype(BF16)) + b_ref[0]


def _modulation(c_all, w_ada, b_ada, tn=1024):
    nb, d = c_all.shape
    depth, _, n = w_ada.shape
    return pl.pallas_call(
        _mod_kernel,
        grid=(depth, n // tn),
        in_specs=[
            pl.BlockSpec((nb, d), lambda l, j: (0, 0)),
            pl.BlockSpec((1, d, tn), lambda l, j: (l, 0, j)),
            pl.BlockSpec((1, 1, tn), lambda l, j: (l, 0, j)),
        ],
        out_specs=pl.BlockSpec((1, nb, tn), lambda l, j: (l, 0, j)),
        out_shape=jax.ShapeDtypeStruct((depth, nb, n), F32),
        compiler_params=_cparams(("parallel", "parallel"), 40),
        name="adaln_modulation",
    )(c_all, w_ada, b_ada.reshape(depth, 1, n))


def _norm_matmul_kernel(x_ref, g_ref, sc_ref, sh_ref, w_ref, cs_ref, *refs, dils):
    o_ref, class_refs, h_ref = refs[0], refs[1:1 + len(dils)], refs[1 + len(dils)]

    @pl.when(pl.program_id(1) == 0)
    def _():
        h_ref[...] = _norm_mod(x_ref[...], g_ref[...], sc_ref[0], sh_ref[0]).astype(BF16)

    y = _dot(h_ref[...], w_ref[...]) * cs_ref[...]
    o_ref[...] = y.astype(o_ref.dtype)
    if dils:
        y_ref = refs[-1]
        nblk, rows, _ = y_ref.shape
        for jb in range(nblk):
            y_ref[jb] = y[:, jb * 128:(jb + 1) * 128]
        for dil, c_ref in zip(dils, class_refs):
            for r in range(dil):
                c_ref[r] = jnp.concatenate(
                    [y_ref[jb, pl.ds(r, rows // dil, stride=dil), :] for jb in range(nblk)], axis=1).astype(c_ref.dtype)


def _norm_matmul(x2, g, mod, row0, seq, k_sc, k_sh, w, colscale, tm=1024, tn=1024, dils=()):
    t, d = x2.shape
    n = w.shape[1]
    tm = min(tm, seq)
    if dils:
        tn //= 2
    brow = lambda i, j: row0 + (i * tm) // seq
    out_specs = [pl.BlockSpec((tm, tn), lambda i, j: (i, j))]
    out_shape = [jax.ShapeDtypeStruct((t, n), BF16)]
    scratch = [pltpu.VMEM((tm, d), BF16)]
    for dil in dils:
        out_specs.append(pl.BlockSpec((dil, tm // dil, tn), lambda i, j: (0, i, j)))
        out_shape.append(jax.ShapeDtypeStruct((dil, t // dil, n), BF16))
    if dils:
        scratch.append(pltpu.VMEM((tn // 128, tm, 128), F32))
    outs = pl.pallas_call(
        functools.partial(_norm_matmul_kernel, dils=tuple(dils)),
        grid=(t // tm, n // tn),
        in_specs=[
            pl.BlockSpec((tm, d), lambda i, j: (i, 0)),
            pl.BlockSpec((1, d), lambda i, j: (0, 0)),
            pl.BlockSpec((1, 1, d), lambda i, j: (brow(i, j), 0, k_sc)),
            pl.BlockSpec((1, 1, d), lambda i, j: (brow(i, j), 0, k_sh)),
            pl.BlockSpec((d, tn), lambda i, j: (0, j)),
            pl.BlockSpec((1, tn), lambda i, j: (0, j)),
        ],
        out_specs=out_specs,
        out_shape=out_shape,
        scratch_shapes=scratch,
        compiler_params=_cparams(("parallel", "arbitrary"), 52),
        name="norm_qkv_proj",
    )(x2, g, mod, mod, w, colscale)
    return outs if dils else outs[0]


def _out_res_kernel(o_ref, w_ref, x_ref, gate_ref, out_ref):
    out_ref[...] = x_ref[...] + gate_ref[0] * _dot(o_ref[...], w_ref[...])


def _out_residual(o2, w, x2, mod, row0, seq, k_gate, tm=1024, tn=1024):
    t, d = x2.shape
    kdim = o2.shape[1]
    tm = min(tm, seq)
    return pl.pallas_call(
        _out_res_kernel,
        grid=(t // tm, d // tn),
        in_specs=[
            pl.BlockSpec((tm, kdim), lambda i, j: (i, 0)),
            pl.BlockSpec((kdim, tn), lambda i, j: (0, j)),
            pl.BlockSpec((tm, tn), lambda i, j: (i, j)),
            pl.BlockSpec((1, 1, tn), lambda i, j: (row0 + (i * tm) // seq, 0, k_gate * (d // tn) + j)),
        ],
        out_specs=pl.BlockSpec((tm, tn), lambda i, j: (i, j)),
        out_shape=jax.ShapeDtypeStruct((t, d), F32),
        compiler_params=_cparams(("parallel", "parallel"), 48),
        name="out_proj_residual",
    )(o2, w, x2, mod)


def _ffn_kernel(xp_ref, x_ref, xn_ref, g_ref, sc_ref, sh_ref, gate_ref, wa_ref, wb_ref,
                cw_ref, cb_ref, wd_ref, out_ref, h_ref, *, tm, seq):
    i = pl.program_id(0)
    f = pl.program_id(1)
    n_ext = tm + 2 * HALO

    @pl.when(f == 0)
    def _():
        g, sc, sh = g_ref[...], sc_ref[0], sh_ref[0]
        at_start = (i * tm) % seq == 0
        at_end = ((i + 1) * tm) % seq == 0
        hp = _norm_mod(xp_ref[...], g, sc, sh)
        hn = _norm_mod(xn_ref[...], g, sc, sh)
        h_ref[0:HALO, :] = jnp.where(at_start, 0.0, hp).astype(BF16)
        h_ref[HALO:HALO + tm, :] = _norm_mod(x_ref[...], g, sc, sh).astype(BF16)
        h_ref[HALO + tm:n_ext, :] = jnp.where(at_end, 0.0, hn).astype(BF16)
        out_ref[...] = jnp.zeros(out_ref.shape, F32)

    a_ext = _dot(h_ref[...], wa_ref[...])
    b = _dot(h_ref[HALO:HALO + tm, :], wb_ref[...])
    a_prev = pltpu.roll(a_ext, 1, 0)[HALO:HALO + tm]
    a_next = pltpu.roll(a_ext, n_ext - 1, 0)[HALO:HALO + tm]
    a_cur = a_ext[HALO:HALO + tm]
    a = cb_ref[...] + (a_prev * cw_ref[0:1, :] + a_cur * cw_ref[1:2, :] + a_next * cw_ref[2:3, :])
    gelu = 0.5 * a * (1.0 + lax.erf(a * (2.0 ** -0.5)))
    out_ref[...] += _dot((gelu * b).astype(BF16), wd_ref[...])

    @pl.when(f == pl.num_programs(1) - 1)
    def _():
        out_ref[...] = x_ref[...] + gate_ref[0] * out_ref[...]


def _ffn(x2, g, mod, row0, seq, w_up, conv_w, conv_b, w_down, tm=512, tf=FF_TILE):
    t, d = x2.shape
    dff = w_down.shape[0]
    nf = dff // tf
    tm = min(tm, seq)
    hb = tm // HALO
    brow = lambda i, f: row0 + (i * tm) // seq
    return pl.pallas_call(
        functools.partial(_ffn_kernel, tm=tm, seq=seq),
        grid=(t // tm, nf),
        in_specs=[
            pl.BlockSpec((HALO, d), lambda i, f: (jnp.maximum(i * hb - 1, 0), 0)),
            pl.BlockSpec((tm, d), lambda i, f: (i, 0)),
            pl.BlockSpec((HALO, d), lambda i, f: (jnp.minimum((i + 1) * hb, t // HALO - 1), 0)),
            pl.BlockSpec((1, d), lambda i, f: (0, 0)),
            pl.BlockSpec((1, 1, d), lambda i, f: (brow(i, f), 0, 4)),
            pl.BlockSpec((1, 1, d), lambda i, f: (brow(i, f), 0, 3)),
            pl.BlockSpec((1, 1, d), lambda i, f: (brow(i, f), 0, 5)),
            pl.BlockSpec((d, tf), lambda i, f: (0, f)),
            pl.BlockSpec((d, tf), lambda i, f: (0, nf + f)),
            pl.BlockSpec((3, tf), lambda i, f: (0, f)),
            pl.BlockSpec((1, tf), lambda i, f: (0, f)),
            pl.BlockSpec((tf, d), lambda i, f: (f, 0)),
        ],
        out_specs=pl.BlockSpec((tm, d), lambda i, f: (i, 0)),
        out_shape=jax.ShapeDtypeStruct((t, d), F32),
        scratch_shapes=[pltpu.VMEM((tm + 2 * HALO, d), BF16)],
        compiler_params=_cparams(("parallel", "arbitrary"), 56),
        name="conv_glu_mlp",
    )(x2, x2, x2, g, mod, mod, mod, w_up, w_up, conv_w, conv_b, w_down)


def _stack_heads(q, width=HEAD_DIM):
    n = q.shape[1] // width
    return jnp.concatenate([q[:, i * width:(i + 1) * width] for i in range(n)], axis=0)


def _unstack_heads(o, n):
    tq = o.shape[0] // n
    return jnp.concatenate([o[i * tq:(i + 1) * tq] for i in range(n)], axis=1)


def _rel_iota(tq, tk):
    return lax.broadcasted_iota(jnp.int32, (tq, tk), 1) - lax.broadcasted_iota(jnp.int32, (tq, tk), 0)


def _attn_specs(seq, tq, q_blk0, k_blk0, v_blk0, kv_width, lag):
    nq = seq // tq
    q_row = lambda b, qi: b * nq + jnp.minimum(qi, nq - 1)
    o_row = lambda b, qi: b * nq + jnp.maximum(qi - lag, 0)
    return [
        pl.BlockSpec((tq, 4 * HEAD_DIM), lambda b, kh, qi: (q_row(b, qi), q_blk0 + kh)),
        pl.BlockSpec((seq, kv_width), lambda b, kh, qi: (b, k_blk0 + kh)),
        pl.BlockSpec((seq, kv_width), lambda b, kh, qi: (b, v_blk0 + kh)),
    ], pl.BlockSpec((tq, 4 * HEAD_DIM), lambda b, kh, qi: (o_row(b, qi), kh))


_SMEM = pl.BlockSpec(memory_space=pltpu.SMEM)
_ATTN_SEM = ("parallel", "parallel", "arbitrary")


def _pipeline_step(qi, score_fn, v_ref, s_refs, macc_ref, finish, *, v_width, tk):
    nck, m_rows, _ = s_refs[0].shape

    @pl.when(qi == 0)
    def _():
        s_refs[1][...] = jnp.zeros(s_refs[1].shape, F32)
        macc_ref[1] = jnp.zeros(macc_ref.shape[1:], F32)

    def run(cur, prev):
        s_cur, s_prev = s_refs[cur], s_refs[prev]
        m_prev = jnp.max(macc_ref[prev], axis=-1, keepdims=True)

        def body(c, carry):
            macc, lacc, acc = carry
            p = jnp.exp2(s_prev[c] - m_prev)
            s = score_fn(c)
            s_cur[c] = s
            for j in range(tk // 128):
                macc = jnp.maximum(macc, s[:, j * 128:(j + 1) * 128])
                lacc = lacc + p[:, j * 128:(j + 1) * 128]
            off = pl.multiple_of(c * tk, tk)
            acc = acc + _dot(p.astype(BF16), v_ref[pl.ds(off, tk), :])
            return macc, lacc, acc

        init = (jnp.full((m_rows, 128), NEG, F32), jnp.zeros((m_rows, 128), F32),
                jnp.zeros((m_rows, v_width), F32))
        macc, lacc, acc = lax.fori_loop(0, nck, body, init, unroll=True)
        macc_ref[cur] = macc
        finish(acc, jnp.sum(lacc, axis=-1, keepdims=True))

    @pl.when(qi % 2 == 0)
    def _():
        run(0, 1)

    @pl.when(qi % 2 == 1)
    def _():
        run(1, 0)


def _pipelined_attention(name, kernel_fn, operands, in_specs, out_spec, batch, seq, tq, tk, m_rows, extra_scratch=()):
    nck = seq // tk
    return pl.pallas_call(
        kernel_fn,
        grid=(batch, N_KV_HEADS, seq // tq + 1),
        in_specs=in_specs,
        out_specs=out_spec,
        out_shape=jax.ShapeDtypeStruct((batch * seq, D_MODEL), BF16),
        scratch_shapes=[pltpu.VMEM((nck, m_rows, tk), F32), pltpu.VMEM((nck, m_rows, tk), F32),
                        pltpu.VMEM((2, m_rows, 128), F32), *extra_scratch],
        compiler_params=_cparams(_ATTN_SEM, 52),
        name=name,
    )(*operands)


def _prep_a_kernel(qkv_ref, cos_ref, sin_ref, w_ref, o_ref):
    tp = qkv_ref.shape[0]
    cos, sin = cos_ref[...], sin_ref[...]
    lane = lax.broadcasted_iota(jnp.int32, (tp, HEAD_DIM), 1)
    low_half = (lane % (HEAD_DIM // 2)) < (HEAD_DIM // 4)
    for h in range(N_HEADS + N_KV_HEADS):
        cols = slice(h * HEAD_DIM, (h + 1) * HEAD_DIM)
        x = qkv_ref[:, cols].astype(F32)
        is_q = h < N_HEADS
        y = x * lax.rsqrt(jnp.mean(x * x, axis=-1, keepdims=True) + EPS) * (w_ref[0:1, :] if is_q else w_ref[1:2, :])
        partner = jnp.where(low_half, pltpu.roll(y, 3 * HEAD_DIM // 4, 1), pltpu.roll(y, HEAD_DIM // 4, 1))
        r = y * cos + partner * sin
        if is_q:
            r = r * Q_SCALE_LOG2
        o_ref[:, cols] = r.astype(BF16)


def _prep_a(qkv, cos_t, sin_t, qk_norm, seq, tp=512):
    t = qkv.shape[0]
    tp = min(tp, seq)
    width = (N_HEADS + N_KV_HEADS) * HEAD_DIM
    return pl.pallas_call(
        _prep_a_kernel,
        grid=(t // tp,),
        in_specs=[
            pl.BlockSpec((tp, width), lambda i: (i, 0)),
            pl.BlockSpec((tp, HEAD_DIM), lambda i: (i % (seq // tp), 0)),
            pl.BlockSpec((tp, HEAD_DIM), lambda i: (i % (seq // tp), 0)),
            pl.BlockSpec((2, HEAD_DIM), lambda i: (0, 0)),
        ],
        out_specs=pl.BlockSpec((tp, width), lambda i: (i, 0)),
        out_shape=jax.ShapeDtypeStruct((t, width), BF16),
        compiler_params=_cparams(("parallel",), 32),
        name="rope_qk_norm",
    )(qkv, cos_t, sin_t, qk_norm)


def _attn_a_kernel(q_ref, k_ref, v_ref, o_ref, s0_ref, s1_ref, macc_ref, *, tk):
    qs = _stack_heads(q_ref[...])

    def score(c):
        return _dot_nt(qs, k_ref[pl.ds(pl.multiple_of(c * tk, tk), tk), :])

    def finish(acc, l):
        o_ref[...] = _unstack_heads(acc / l, GQA_GROUP).astype(o_ref.dtype)

    _pipeline_step(pl.program_id(2), score, v_ref, (s0_ref, s1_ref), macc_ref, finish, v_width=HEAD_DIM, tk=tk)


def _attn_a(qk_rot, qkv, batch, seq, tq=128, tk=1024):
    tk = min(tk, seq)
    in_specs, out_spec = _attn_specs(seq, tq, 0, N_HEADS, N_HEADS + N_KV_HEADS, HEAD_DIM, 1)
    return _pipelined_attention("attn_axial", functools.partial(_attn_a_kernel, tk=tk), (qk_rot, qk_rot, qkv),
                                in_specs, out_spec, batch, seq, tq, tk, GQA_GROUP * tq)


def _window_kernel(slopes_ref, sink_ref, q_ref, k_ref, v_ref, o_ref, lse_ref, s0_ref, s1_ref, macc_ref,
                   *, tq, band, window, nq, seq_c, dil, use_sink, n_steps):
    n = pl.program_id(0)

    def head_and_block(step):
        return (step // nq) % N_KV_HEADS, step % nq

    def band_start(qi):
        if band == seq_c:
            return 0
        return pl.multiple_of(jnp.clip(qi * tq - window, 0, seq_c - band), window)

    @pl.when(n == 0)
    def _():
        s1_ref[...] = jnp.zeros(s1_ref.shape, F32)
        macc_ref[1] = jnp.zeros(macc_ref.shape[1:], F32)

    def run(cur, prev):
        s_cur, s_prev = (s0_ref, s1_ref) if cur == 0 else (s1_ref, s0_ref)
        kh, qi = head_and_block(jnp.minimum(n, n_steps - 1))
        ws = band_start(qi)
        qs = _stack_heads(q_ref[...])
        rel = jnp.abs(_rel_iota(tq, band) + (ws - qi * tq))
        in_window = rel <= window
        dist = rel.astype(F32)
        bias = jnp.concatenate(
            [jnp.where(in_window, -(slopes_ref[kh * GQA_GROUP + g] * (dil * LOG2E)) * dist, NEG)
             for g in range(GQA_GROUP)], axis=0)
        s = _dot_nt(qs, k_ref[pl.ds(ws, band), :]) + bias
        s_cur[...] = s
        macc = s[:, 0:128]
        for j in range(1, band // 128):
            macc = jnp.maximum(macc, s[:, j * 128:(j + 1) * 128])
        macc_ref[cur] = macc
        khp, qip = head_and_block(jnp.maximum(n - 1, 0))
        wsp = band_start(qip)
        m = jnp.max(macc_ref[prev], axis=-1, keepdims=True)
        if use_sink:
            sink = jnp.concatenate(
                [jnp.full((tq, 1), sink_ref[khp * GQA_GROUP + g] * LOG2E, F32) for g in range(GQA_GROUP)], axis=0)
            m = jnp.maximum(m, sink)
        p = jnp.exp2(s_prev[...] - m)
        lacc = p[:, 0:128]
        for j in range(1, band // 128):
            lacc = lacc + p[:, j * 128:(j + 1) * 128]
        l = jnp.sum(lacc, axis=-1, keepdims=True)
        if use_sink:
            l = l + jnp.exp2(sink - m)
        o = _dot(p.astype(BF16), v_ref[pl.ds(wsp, band), :]) / l
        o_ref[...] = _unstack_heads(o, GQA_GROUP).astype(o_ref.dtype)
        lse = m + jnp.log2(l)
        lse_ref[...] = jnp.concatenate(
            [jnp.broadcast_to(lse[g * tq:(g + 1) * tq], (tq, 128 // GQA_GROUP)) for g in range(GQA_GROUP)], axis=1)

    @pl.when(n % 2 == 0)
    def _():
        run(0, 1)

    @pl.when(n % 2 == 1)
    def _():
        run(1, 0)


def _attn_window(qkv_c, slopes, sink, batch, seq, *, window, use_sink):
    dil, t_c, width = qkv_c.shape
    seq_c = seq // dil
    tq = min(256, seq_c)
    band = min(tq + 2 * window, seq_c)
    nq = seq_c // tq
    assert band % 128 == 0 and seq_c % tq == 0, "bands are reduced in 128-lane blocks"
    n_steps = batch * dil * N_KV_HEADS * nq
    qw = GQA_GROUP * HEAD_DIM

    def split(step):
        qi = step % nq
        rest = step // nq
        kh = rest % N_KV_HEADS
        rest = rest // N_KV_HEADS
        return rest // dil, rest % dil, kh, qi

    def q_map(n):
        b, r, kh, qi = split(jnp.minimum(n, n_steps - 1))
        return r, b * nq + qi, kh

    def k_map(n):
        b, r, kh, _ = split(jnp.minimum(n, n_steps - 1))
        return r, b, N_HEADS + kh

    def v_map(n):
        b, r, kh, _ = split(jnp.maximum(n - 1, 0))
        return r, b, N_HEADS + N_KV_HEADS + kh

    def o_map(n):
        b, r, kh, qi = split(jnp.maximum(n - 1, 0))
        return r, b * nq + qi, kh

    return pl.pallas_call(
        functools.partial(_window_kernel, tq=tq, band=band, window=window, nq=nq, seq_c=seq_c, dil=dil,
                          use_sink=use_sink, n_steps=n_steps),
        grid=(n_steps + 1,),
        in_specs=[_SMEM, _SMEM,
                  pl.BlockSpec((None, tq, qw), q_map),
                  pl.BlockSpec((None, seq_c, HEAD_DIM), k_map),
                  pl.BlockSpec((None, seq_c, HEAD_DIM), v_map)],
        out_specs=[pl.BlockSpec((None, tq, qw), o_map), pl.BlockSpec((None, tq, 128), o_map)],
        out_shape=[jax.ShapeDtypeStruct((dil, t_c, D_MODEL), BF16),
                   jax.ShapeDtypeStruct((dil, t_c, N_KV_HEADS * 128), F32)],
        scratch_shapes=[pltpu.VMEM((GQA_GROUP * tq, band), F32), pltpu.VMEM((GQA_GROUP * tq, band), F32),
                        pltpu.VMEM((2, GQA_GROUP * tq, 128), F32)],
        compiler_params=_cparams(("arbitrary",), 40),
        name=f"attn_window_d{dil}",
    )(slopes, sink, qkv_c, qkv_c, qkv_c)


def _attn_c_kernel(slopes_ref, q_ref, k_ref, v_ref, lam_ref, subln_ref, o_ref, s0_ref, s1_ref, macc_ref, bias_ref,
                   *, tq, tk, seq, lam_init):
    kh, qi = pl.program_id(1), pl.program_id(2)
    start = jnp.minimum(qi, seq // tq - 1) * tq
    q = q_ref[...]
    h = HEAD_DIM
    qs0 = jnp.concatenate([q[:, 0:h], q[:, 2 * h:3 * h]], axis=0)
    qs1 = jnp.concatenate([q[:, h:2 * h], q[:, 3 * h:4 * h]], axis=0)
    n_blk = seq // 128

    @pl.when(qi == 0)
    def _():
        lane_minus_row = _rel_iota(tq, 128)
        for u in range(2 * n_blk - 1):
            dist = jnp.abs(lane_minus_row + (u - (n_blk - 1)) * 128).astype(F32)
            for g in range(2):
                bias_ref[g, u] = slopes_ref[2 * kh + g] * dist

    def score(c):
        off = pl.multiple_of(c * tk, tk)
        k = k_ref[pl.ds(off, tk), :]
        u0 = (off - start) // 128 + (n_blk - 1)
        bias = jnp.concatenate(
            [jnp.concatenate([bias_ref[g, u0 + j] for j in range(tk // 128)], axis=1) for g in range(2)], axis=0)
        return jnp.concatenate([_dot_nt(qs0, k[:, :h]) - bias, _dot_nt(qs1, k[:, h:]) - bias], axis=0)

    lp = lam_ref[...]
    lam = (jnp.exp(jnp.sum(lp[0:1] * lp[1:2], axis=-1, keepdims=True))
           - jnp.exp(jnp.sum(lp[2:3] * lp[3:4], axis=-1, keepdims=True)) + lam_init)

    def finish(acc, l):
        a = acc / l
        o = a[:2 * tq] - lam * a[2 * tq:]
        y = o * lax.rsqrt(jnp.mean(o * o, axis=-1, keepdims=True) + EPS) * subln_ref[...] * (1.0 - lam_init)
        o_ref[...] = _unstack_heads(y, 2).astype(o_ref.dtype)

    _pipeline_step(qi, score, v_ref, (s0_ref, s1_ref), macc_ref, finish, v_width=2 * h, tk=tk)


def _attn_c(qkv, slopes_log2, lam_params, subln, batch, seq, lam_init, tq=128, tk=1024):
    assert tq == 128, "the bias table is built for 128-row query blocks"
    tk = min(tk, seq)
    in_specs, out_spec = _attn_specs(seq, tq, 0, 8, 12, 2 * HEAD_DIM, 1)
    in_specs = [_SMEM] + in_specs + [
        pl.BlockSpec((4, HEAD_DIM), lambda b, kh, qi: (0, 0)),
        pl.BlockSpec((1, 2 * HEAD_DIM), lambda b, kh, qi: (0, 0)),
    ]
    return _pipelined_attention(
        "attn_diff", functools.partial(_attn_c_kernel, tq=tq, tk=tk, seq=seq, lam_init=lam_init),
        (slopes_log2, qkv, qkv, qkv, lam_params, subln), in_specs, out_spec, batch, seq, tq, tk, 4 * tq,
        extra_scratch=[pltpu.VMEM((2, 2 * (seq // 128) - 1, tq, 128), F32)])


def _combine_kernel(*refs):
    nb = len(DILATIONS)
    o_refs, l_refs, out_ref, scratch = refs[:nb], refs[nb:2 * nb], refs[2 * nb], refs[2 * nb + 1:]
    tc = out_ref.shape[0]
    o_nat, l_nat = [], []
    for b, (_, dil) in enumerate(DILATIONS):
        if dil == 1:
            o_nat.append(o_refs[b][0].astype(F32))
            l_nat.append(l_refs[b][0])
            continue
        so, sl = scratch[2 * b], scratch[2 * b + 1]
        rows = tc // dil
        for r in range(dil):
            o_r, l_r = o_refs[b][r].astype(F32), l_refs[b][r]
            for jb in range(so.shape[0]):
                so[jb, pl.ds(r, rows, stride=dil), :] = o_r[:, jb * 128:(jb + 1) * 128]
            for jb in range(sl.shape[0]):
                sl[jb, pl.ds(r, rows, stride=dil), :] = l_r[:, jb * 128:(jb + 1) * 128]
        o_nat.append(jnp.concatenate([so[jb] for jb in range(so.shape[0])], axis=1))
        l_nat.append(jnp.concatenate([sl[jb] for jb in range(sl.shape[0])], axis=1))
    lanes = 128 // GQA_GROUP
    for h in range(N_HEADS):
        e = [l[:, h * lanes:h * lanes + 1] for l in l_nat]
        m = functools.reduce(jnp.maximum, e)
        w = [jnp.exp2(x - m) for x in e]
        cols = slice(h * HEAD_DIM, (h + 1) * HEAD_DIM)
        mix = sum(wi * o[:, cols] for wi, o in zip(w, o_nat))
        out_ref[:, cols] = (mix / sum(w)).astype(out_ref.dtype)


def _attn_d(qkv_classes, slopes, batch, seq, tc=256):
    t = qkv_classes[0].shape[1]
    dummy_sink = jnp.zeros((N_HEADS,), F32)
    branches = [_attn_window(qkv_c, slopes, dummy_sink, batch, seq, window=(win // 2) // dil, use_sink=False)
                for qkv_c, (win, dil) in zip(qkv_classes, DILATIONS)]
    lw = N_KV_HEADS * 128
    o_specs = [pl.BlockSpec((dil, tc // dil, D_MODEL), lambda i: (0, i, 0)) for _, dil in DILATIONS]
    l_specs = [pl.BlockSpec((dil, tc // dil, lw), lambda i: (0, i, 0)) for _, dil in DILATIONS]
    scratch = []
    for _ in DILATIONS:
        scratch += [pltpu.VMEM((D_MODEL // 128, tc, 128), F32), pltpu.VMEM((lw // 128, tc, 128), F32)]
    return pl.pallas_call(
        _combine_kernel,
        grid=(t // tc,),
        in_specs=o_specs + l_specs,
        out_specs=pl.BlockSpec((tc, D_MODEL), lambda i: (i, 0)),
        out_shape=jax.ShapeDtypeStruct((t, D_MODEL), BF16),
        scratch_shapes=scratch,
        compiler_params=_cparams(("parallel",), 40),
        name="dilated_combine",
    )(*[b[0] for b in branches], *[b[1] for b in branches])


def _final_norm_kernel(x_ref, g_ref, o_ref):
    x = x_ref[...]
    o_ref[...] = x * lax.rsqrt(jnp.mean(x * x, axis=-1, keepdims=True) + EPS) * g_ref[...]


def _final_norm(x2, g, tm=512):
    t, d = x2.shape
    return pl.pallas_call(
        _final_norm_kernel,
        grid=(t // tm,),
        in_specs=[pl.BlockSpec((tm, d), lambda i: (i, 0)), pl.BlockSpec((1, d), lambda i: (0, 0))],
        out_specs=pl.BlockSpec((tm, d), lambda i: (i, 0)),
        out_shape=jax.ShapeDtypeStruct((t, d), F32),
        compiler_params=_cparams(("parallel",), 32),
        name="final_norm",
    )(x2, g)


def _alibi_slopes(n):
    return 2.0 ** (-8.0 * jnp.arange(1, n + 1, dtype=F32) / n)


def _lambda_init(layer):
    return 0.8 - 0.6 * math.exp(-0.3 * layer)


def _rope_tables(seq):
    rows = seq // GRID_W
    row = jnp.repeat(jnp.arange(rows, dtype=F32), GRID_W)
    col = jnp.tile(jnp.arange(GRID_W, dtype=F32), rows)
    n_freq = HEAD_DIM // 4
    inv_freq = ROPE_THETA ** (-jnp.arange(n_freq, dtype=F32) / n_freq)
    ang = jnp.stack([row, col], axis=-1)[..., None] * inv_freq
    cos, sin = jnp.cos(ang), jnp.sin(ang)
    cos_t = jnp.concatenate([cos[:, 0], cos[:, 0], cos[:, 1], cos[:, 1]], axis=-1)
    sin_t = jnp.concatenate([-sin[:, 0], sin[:, 0], -sin[:, 1], sin[:, 1]], axis=-1)
    return cos_t, sin_t


def _q_colscale(scale, n_q_cols, n_cols):
    return jnp.concatenate([jnp.full((1, n_q_cols), scale, F32), jnp.ones((1, n_cols - n_q_cols), F32)], axis=1)


def _prepare_weights(p):
    pad = D_FF_PAD - D_FF
    w_up = p['ffn_w_up']
    w_up = jnp.concatenate([jnp.pad(w_up[..., :D_FF], ((0, 0), (0, 0), (0, pad))),
                            jnp.pad(w_up[..., D_FF:], ((0, 0), (0, 0), (0, pad)))], axis=-1).astype(BF16)
    return dict(
        a_wqkv=p['a_wqkv'].astype(BF16), a_wo=p['a_wo'].astype(BF16),
        b_wqkv=p['b_wqkv'].astype(BF16), b_wo=p['b_wo'].astype(BF16),
        c_wqkv=p['c_wqkv'].astype(BF16), c_wo=p['c_wo'].astype(BF16),
        d_wqkv=p['d_wqkv'].astype(BF16), d_wo=p['d_wo'].astype(BF16),
        ffn_w_up=w_up,
        ffn_w_down=jnp.pad(p['ffn_w_down'], ((0, 0), (0, pad), (0, 0))).astype(BF16),
        ffn_conv_w=jnp.pad(p['ffn_conv_w'], ((0, 0), (0, 0), (0, pad))),
        ffn_conv_b=jnp.pad(p['ffn_conv_b'], ((0, 0), (0, pad)))[:, None, :],
    )


def _trunk(x, mod_all, row0, p, w):
    batch, seq, d = x.shape
    x2 = x.reshape(batch * seq, d)
    slopes16, slopes8 = _alibi_slopes(N_HEADS), _alibi_slopes(DIFF_HEADS)
    n_qkv = (N_HEADS + 2 * N_KV_HEADS) * HEAD_DIM
    cs_plain = jnp.ones((1, n_qkv), F32)
    cs_q_log2 = _q_colscale(Q_SCALE_LOG2, N_HEADS * HEAD_DIM, n_qkv)
    cs_q_diff = _q_colscale(Q_SCALE_LOG2, DIFF_HEADS * 2 * HEAD_DIM, 2 * D_MODEL)
    for i in range(DEPTH):
        mod = mod_all[i][:, None, :]
        g_attn = p['norm_attn'][i][None, :]
        m, j = i % 4, i // 4
        if m == 0:
            qkv = _norm_matmul(x2, g_attn, mod, row0, seq, 1, 0, w['a_wqkv'][j], cs_plain)
            cos_t, sin_t = _rope_tables(seq)
            qk_norm = jnp.stack([p['a_q_norm'][j], p['a_k_norm'][j]])
            o = _attn_a(_prep_a(qkv, cos_t, sin_t, qk_norm, seq), qkv, batch, seq)
            wo = w['a_wo'][j]
        elif m == 1:
            qkv = _norm_matmul(x2, g_attn, mod, row0, seq, 1, 0, w['b_wqkv'][j], cs_q_log2)
            o, _ = _attn_window(qkv[None], slopes16, p['b_sink'][j].astype(F32), batch, seq, window=WINDOW,
                                use_sink=True)
            o = o[0]
            wo = w['b_wo'][j]
        elif m == 2:
            qkv = _norm_matmul(x2, g_attn, mod, row0, seq, 1, 0, w['c_wqkv'][j], cs_q_diff)
            o = _attn_c(qkv, slopes8 * LOG2E, p['c_lambda'][j].astype(F32), p['c_subln'][j][None, :],
                        batch, seq, _lambda_init(i))
            wo = w['c_wo'][j]
        else:
            dils = tuple(dil for _, dil in DILATIONS if dil > 1)
            qkv, *qkv_classes = _norm_matmul(x2, g_attn, mod, row0, seq, 1, 0, w['d_wqkv'][j], cs_q_log2, dils=dils)
            o = _attn_d([qkv[None], *qkv_classes], slopes16, batch, seq)
            wo = w['d_wo'][j]
        x2 = _out_residual(o, wo, x2, mod, row0, seq, 2)
        x2 = _ffn(x2, p['norm_ffn'][i][None, :], mod, row0, seq, w['ffn_w_up'][i], w['ffn_conv_w'][i],
                  w['ffn_conv_b'][i], w['ffn_w_down'][i])
    return _final_norm(x2, p['norm_final'][None, :]).reshape(batch, seq, d)


def kernel(x_prompt, x_sample, c_prompt, c_sample, norm_attn, norm_ffn, w_ada, b_ada, a_wqkv, a_q_norm, a_k_norm, a_wo, b_wqkv, b_sink, b_wo, c_wqkv, c_lambda, c_subln, c_wo, d_wqkv, d_wo, ffn_w_up, ffn_conv_w, ffn_conv_b, ffn_w_down, norm_final):
    p = dict(norm_attn=norm_attn, norm_ffn=norm_ffn, a_q_norm=a_q_norm, a_k_norm=a_k_norm, b_sink=b_sink,
             c_lambda=c_lambda, c_subln=c_subln, norm_final=norm_final,
             a_wqkv=a_wqkv, a_wo=a_wo, b_wqkv=b_wqkv, b_wo=b_wo, c_wqkv=c_wqkv, c_wo=c_wo,
             d_wqkv=d_wqkv, d_wo=d_wo, ffn_w_up=ffn_w_up, ffn_conv_w=ffn_conv_w, ffn_conv_b=ffn_conv_b,
             ffn_w_down=ffn_w_down)
    w = _prepare_weights(p)
    n_prompt, n_sample = c_prompt.shape[0], c_sample.shape[0]
    rows = -(-(n_prompt + n_sample) // 8) * 8
    c_all = jnp.concatenate([c_prompt, c_sample, jnp.zeros((rows - n_prompt - n_sample, D_MODEL), F32)], axis=0)
    mod_all = _modulation(c_all, w_ada, b_ada)
    y_prompt = _trunk(x_prompt, mod_all, 0, p, w)
    y_sample = _trunk(x_sample, mod_all, n_prompt, p, w)
    return (y_prompt, y_sample)
```

```python
import functools
import math

import jax
import jax.numpy as jnp
from jax import lax
from jax.experimental import pallas as pl
from jax.experimental.pallas import tpu as pltpu

F32 = jnp.float32
BF16 = jnp.bfloat16

D_MODEL = 2048
HEAD_DIM = 128
N_HEADS = 16
N_KV_HEADS = 4
GQA_GROUP = 4
DIFF_HEADS = 8
D_FF = 5504
FF_TILE = 512
D_FF_PAD = 5632
N_MOD = 6
DEPTH = 4
GRID_W = 64
ROPE_THETA = 10000.0
EPS = 1e-6
NEG = -1e30
WINDOW = 128
DILATIONS = ((128, 1), (512, 4), (2048, 16))
Q_SCALE = HEAD_DIM ** -0.5
LOG2E = math.log2(math.e)
Q_SCALE_LOG2 = Q_SCALE * LOG2E
HALO = 16
MIB = 1024 * 1024


def _cparams(sem, vmem_mib):
    return pltpu.CompilerParams(dimension_semantics=sem, vmem_limit_bytes=vmem_mib * MIB)


def _dot(a, b):
    return jnp.dot(a, b, preferred_element_type=F32)


def _dot_nt(a, b):
    return lax.dot_general(a, b, (((1,), (1,)), ((), ())), preferred_element_type=F32)


def _norm_mod(x, g, sc, sh):
    y = x * lax.rsqrt(jnp.mean(x * x, axis=-1, keepdims=True) + EPS) * g
    return y * (1.0 + sc) + sh


def _mod_kernel(c_ref, w_ref, b_ref, o_ref):
    c = c_ref[...]
    a = (c / (1.0 + jnp.exp(-c))).astype(BF16)
    o_ref[0] = _dot(a, w_ref[0].astype(BF16)) + b_ref[0]


def _modulation(c_all, w_ada, b_ada, tn=1024):
    nb, d = c_all.shape
    depth, _, n = w_ada.shape
    return pl.pallas_call(
        _mod_kernel,
        grid=(depth, n // tn),
        in_specs=[
            pl.BlockSpec((nb, d), lambda l, j: (0, 0)),
            pl.BlockSpec((1, d, tn), lambda l, j: (l, 0, j)),
            pl.BlockSpec((1, 1, tn), lambda l, j: (l, 0, j)),
        ],
        out_specs=pl.BlockSpec((1, nb, tn), lambda l, j: (l, 0, j)),
        out_shape=jax.ShapeDtypeStruct((depth, nb, n), F32),
        compiler_params=_cparams(("parallel", "parallel"), 40),
        name="adaln_modulation",
    )(c_all, w_ada, b_ada.reshape(depth, 1, n))


def _norm_matmul_kernel(x_ref, g_ref, sc_ref, sh_ref, w_ref, cs_ref, *refs, dils):
    o_ref, class_refs, h_ref = refs[0], refs[1:1 + len(dils)], refs[1 + len(dils)]

    def project(h):
        y = _dot(h, w_ref[...]) * cs_ref[...]
        o_ref[...] = y.astype(o_ref.dtype)
        if dils:
            y_ref = refs[-1]
            nblk, rows, _ = y_ref.shape
            for jb in range(nblk):
                y_ref[jb] = y[:, jb * 128:(jb + 1) * 128]
            for dil, c_ref in zip(dils, class_refs):
                for r in range(dil):
                    c_ref[r] = jnp.concatenate(
                        [y_ref[jb, pl.ds(r, rows // dil, stride=dil), :] for jb in range(nblk)],
                        axis=1).astype(c_ref.dtype)

    @pl.when(pl.program_id(1) == 0)
    def _():
        h = _norm_mod(x_ref[...], g_ref[...], sc_ref[0], sh_ref[0]).astype(BF16)
        h_ref[...] = h
        project(h)

    @pl.when(pl.program_id(1) > 0)
    def _():
        project(h_ref[...])


def _norm_matmul(x2, g, mod, row0, seq, k_sc, k_sh, w, colscale, tm=1024, tn=1024, dils=()):
    t, d = x2.shape
    n = w.shape[1]
    tm = min(tm, seq)
    if dils:
        tn //= 2
    brow = lambda i, j: row0 + (i * tm) // seq
    out_specs = [pl.BlockSpec((tm, tn), lambda i, j: (i, j))]
    out_shape = [jax.ShapeDtypeStruct((t, n), BF16)]
    scratch = [pltpu.VMEM((tm, d), BF16)]
    for dil in dils:
        out_specs.append(pl.BlockSpec((dil, tm // dil, tn), lambda i, j: (0, i, j)))
        out_shape.append(jax.ShapeDtypeStruct((dil, t // dil, n), BF16))
    if dils:
        scratch.append(pltpu.VMEM((tn // 128, tm, 128), F32))
    outs = pl.pallas_call(
        functools.partial(_norm_matmul_kernel, dils=tuple(dils)),
        grid=(t // tm, n // tn),
        in_specs=[
            pl.BlockSpec((tm, d), lambda i, j: (i, 0)),
            pl.BlockSpec((1, d), lambda i, j: (0, 0)),
            pl.BlockSpec((1, 1, d), lambda i, j: (brow(i, j), 0, k_sc)),
            pl.BlockSpec((1, 1, d), lambda i, j: (brow(i, j), 0, k_sh)),
            pl.BlockSpec((d, tn), lambda i, j: (0, j)),
            pl.BlockSpec((1, tn), lambda i, j: (0, j)),
        ],
        out_specs=out_specs,
        out_shape=out_shape,
        scratch_shapes=scratch,
        compiler_params=_cparams(("parallel", "arbitrary"), 52),
        name="norm_qkv_proj",
    )(x2, g, mod, mod, w, colscale)
    return outs if dils else outs[0]


def _out_res_kernel(o_ref, w_ref, x_ref, gate_ref, out_ref):
    out_ref[...] = x_ref[...] + gate_ref[0] * _dot(o_ref[...], w_ref[...])


def _out_residual(o2, w, x2, mod, row0, seq, k_gate, tm=1024, tn=1024):
    t, d = x2.shape
    kdim = o2.shape[1]
    tm = min(tm, seq)
    return pl.pallas_call(
        _out_res_kernel,
        grid=(t // tm, d // tn),
        in_specs=[
            pl.BlockSpec((tm, kdim), lambda i, j: (i, 0)),
            pl.BlockSpec((kdim, tn), lambda i, j: (0, j)),
            pl.BlockSpec((tm, tn), lambda i, j: (i, j)),
            pl.BlockSpec((1, 1, tn), lambda i, j: (row0 + (i * tm) // seq, 0, k_gate * (d // tn) + j)),
        ],
        out_specs=pl.BlockSpec((tm, tn), lambda i, j: (i, j)),
        out_shape=jax.ShapeDtypeStruct((t, d), F32),
        compiler_params=_cparams(("parallel", "parallel"), 48),
        name="out_proj_residual",
    )(o2, w, x2, mod)


def _ffn_kernel(xp_ref, x_ref, xn_ref, g_ref, sc_ref, sh_ref, gate_ref, wa_ref, wb_ref,
                cw_ref, cb_ref, wd_ref, out_ref, h_ref, *, tm, seq):
    i = pl.program_id(0)
    f = pl.program_id(1)
    n_ext = tm + 2 * HALO

    def partial_down(h_ext):
        a_ext = _dot(h_ext, wa_ref[...])
        b = _dot(h_ext[HALO:HALO + tm], wb_ref[...])
        a_prev = pltpu.roll(a_ext, 1, 0)[HALO:HALO + tm]
        a_next = pltpu.roll(a_ext, n_ext - 1, 0)[HALO:HALO + tm]
        a_cur = a_ext[HALO:HALO + tm]
        a = cb_ref[...] + (a_prev * cw_ref[0:1, :] + a_cur * cw_ref[1:2, :] + a_next * cw_ref[2:3, :])
        gelu = 0.5 * a * (1.0 + lax.erf(a * (2.0 ** -0.5)))
        return _dot((gelu * b).astype(BF16), wd_ref[...])

    @pl.when(f == 0)
    def _():
        g, sc, sh = g_ref[...], sc_ref[0], sh_ref[0]
        at_start = (i * tm) % seq == 0
        at_end = ((i + 1) * tm) % seq == 0
        hp = jnp.where(at_start, 0.0, _norm_mod(xp_ref[...], g, sc, sh))
        hn = jnp.where(at_end, 0.0, _norm_mod(xn_ref[...], g, sc, sh))
        h_ext = jnp.concatenate([hp, _norm_mod(x_ref[...], g, sc, sh), hn], axis=0).astype(BF16)
        h_ref[...] = h_ext
        out_ref[...] = partial_down(h_ext)

    @pl.when(f > 0)
    def _():
        out_ref[...] += partial_down(h_ref[...])

    @pl.when(f == pl.num_programs(1) - 1)
    def _():
        out_ref[...] = x_ref[...] + gate_ref[0] * out_ref[...]


def _ffn(x2, g, mod, row0, seq, w_up, conv_w, conv_b, w_down, tm=512, tf=FF_TILE):
    t, d = x2.shape
    dff = w_down.shape[0]
    nf = dff // tf
    tm = min(tm, seq)
    hb = tm // HALO
    brow = lambda i, f: row0 + (i * tm) // seq
    return pl.pallas_call(
        functools.partial(_ffn_kernel, tm=tm, seq=seq),
        grid=(t // tm, nf),
        in_specs=[
            pl.BlockSpec((HALO, d), lambda i, f: (jnp.maximum(i * hb - 1, 0), 0)),
            pl.BlockSpec((tm, d), lambda i, f: (i, 0)),
            pl.BlockSpec((HALO, d), lambda i, f: (jnp.minimum((i + 1) * hb, t // HALO - 1), 0)),
            pl.BlockSpec((1, d), lambda i, f: (0, 0)),
            pl.BlockSpec((1, 1, d), lambda i, f: (brow(i, f), 0, 4)),
            pl.BlockSpec((1, 1, d), lambda i, f: (brow(i, f), 0, 3)),
            pl.BlockSpec((1, 1, d), lambda i, f: (brow(i, f), 0, 5)),
            pl.BlockSpec((d, tf), lambda i, f: (0, f)),
            pl.BlockSpec((d, tf), lambda i, f: (0, nf + f)),
            pl.BlockSpec((3, tf), lambda i, f: (0, f)),
            pl.BlockSpec((1, tf), lambda i, f: (0, f)),
            pl.BlockSpec((tf, d), lambda i, f: (f, 0)),
        ],
        out_specs=pl.BlockSpec((tm, d), lambda i, f: (i, 0)),
        out_shape=jax.ShapeDtypeStruct((t, d), F32),
        scratch_shapes=[pltpu.VMEM((tm + 2 * HALO, d), BF16)],
        compiler_params=_cparams(("parallel", "arbitrary"), 56),
        name="conv_glu_mlp",
    )(x2, x2, x2, g, mod, mod, mod, w_up, w_up, conv_w, conv_b, w_down)


def _stack_heads(q, width=HEAD_DIM):
    n = q.shape[1] // width
    return jnp.concatenate([q[:, i * width:(i + 1) * width] for i in range(n)], axis=0)


def _unstack_heads(o, n):
    tq = o.shape[0] // n
    return jnp.concatenate([o[i * tq:(i + 1) * tq] for i in range(n)], axis=1)


def _rel_iota(tq, tk):
    return lax.broadcasted_iota(jnp.int32, (tq, tk), 1) - lax.broadcasted_iota(jnp.int32, (tq, tk), 0)


def _attn_specs(seq, tq, q_blk0, k_blk0, v_blk0, kv_width, lag):
    nq = seq // tq
    q_row = lambda b, qi: b * nq + jnp.minimum(qi, nq - 1)
    o_row = lambda b, qi: b * nq + jnp.maximum(qi - lag, 0)
    return [
        pl.BlockSpec((tq, 4 * HEAD_DIM), lambda b, kh, qi: (q_row(b, qi), q_blk0 + kh)),
        pl.BlockSpec((seq, kv_width), lambda b, kh, qi: (b, k_blk0 + kh)),
        pl.BlockSpec((seq, kv_width), lambda b, kh, qi: (b, v_blk0 + kh)),
    ], pl.BlockSpec((tq, 4 * HEAD_DIM), lambda b, kh, qi: (o_row(b, qi), kh))


_SMEM = pl.BlockSpec(memory_space=pltpu.SMEM)
_ATTN_SEM = ("parallel", "parallel", "arbitrary")


def _pipeline_step(qi, score_fn, v_ref, s_refs, macc_ref, finish, *, v_width, tk):
    nck, m_rows, _ = s_refs[0].shape

    @pl.when(qi == 0)
    def _():
        s_refs[1][...] = jnp.zeros(s_refs[1].shape, F32)
        macc_ref[1] = jnp.zeros(macc_ref.shape[1:], F32)

    def run(cur, prev):
        s_cur, s_prev = s_refs[cur], s_refs[prev]
        m_prev = jnp.max(macc_ref[prev], axis=-1, keepdims=True)

        def body(c, carry):
            macc, lacc, acc = carry
            p = jnp.exp2(s_prev[c] - m_prev)
            s = score_fn(c)
            s_cur[c] = s
            for j in range(tk // 128):
                macc = jnp.maximum(macc, s[:, j * 128:(j + 1) * 128])
                lacc = lacc + p[:, j * 128:(j + 1) * 128]
            off = pl.multiple_of(c * tk, tk)
            acc = acc + _dot(p.astype(BF16), v_ref[pl.ds(off, tk), :])
            return macc, lacc, acc

        init = (jnp.full((m_rows, 128), NEG, F32), jnp.zeros((m_rows, 128), F32),
                jnp.zeros((m_rows, v_width), F32))
        macc, lacc, acc = lax.fori_loop(0, nck, body, init, unroll=True)
        macc_ref[cur] = macc
        finish(acc, jnp.sum(lacc, axis=-1, keepdims=True))

    @pl.when(qi % 2 == 0)
    def _():
        run(0, 1)

    @pl.when(qi % 2 == 1)
    def _():
        run(1, 0)


def _pipelined_attention(name, kernel_fn, operands, in_specs, out_spec, batch, seq, tq, tk, m_rows, extra_scratch=()):
    nck = seq // tk
    return pl.pallas_call(
        kernel_fn,
        grid=(batch, N_KV_HEADS, seq // tq + 1),
        in_specs=in_specs,
        out_specs=out_spec,
        out_shape=jax.ShapeDtypeStruct((batch * seq, D_MODEL), BF16),
        scratch_shapes=[pltpu.VMEM((nck, m_rows, tk), F32), pltpu.VMEM((nck, m_rows, tk), F32),
                        pltpu.VMEM((2, m_rows, 128), F32), *extra_scratch],
        compiler_params=_cparams(_ATTN_SEM, 52),
        name=name,
    )(*operands)


def _prep_a_kernel(qkv_ref, cos_ref, sin_ref, w_ref, o_ref):
    tp = qkv_ref.shape[0]
    cos, sin = cos_ref[...], sin_ref[...]
    lane = lax.broadcasted_iota(jnp.int32, (tp, HEAD_DIM), 1)
    low_half = (lane % (HEAD_DIM // 2)) < (HEAD_DIM // 4)
    for h in range(N_HEADS + N_KV_HEADS):
        cols = slice(h * HEAD_DIM, (h + 1) * HEAD_DIM)
        x = qkv_ref[:, cols].astype(F32)
        is_q = h < N_HEADS
        y = x * lax.rsqrt(jnp.mean(x * x, axis=-1, keepdims=True) + EPS) * (w_ref[0:1, :] if is_q else w_ref[1:2, :])
        partner = jnp.where(low_half, pltpu.roll(y, 3 * HEAD_DIM // 4, 1), pltpu.roll(y, HEAD_DIM // 4, 1))
        r = y * cos + partner * sin
        if is_q:
            r = r * Q_SCALE_LOG2
        o_ref[:, cols] = r.astype(BF16)


def _prep_a(qkv, cos_t, sin_t, qk_norm, seq, tp=512):
    t = qkv.shape[0]
    tp = min(tp, seq)
    width = (N_HEADS + N_KV_HEADS) * HEAD_DIM
    return pl.pallas_call(
        _prep_a_kernel,
        grid=(t // tp,),
        in_specs=[
            pl.BlockSpec((tp, width), lambda i: (i, 0)),
            pl.BlockSpec((tp, HEAD_DIM), lambda i: (i % (seq // tp), 0)),
            pl.BlockSpec((tp, HEAD_DIM), lambda i: (i % (seq // tp), 0)),
            pl.BlockSpec((2, HEAD_DIM), lambda i: (0, 0)),
        ],
        out_specs=pl.BlockSpec((tp, width), lambda i: (i, 0)),
        out_shape=jax.ShapeDtypeStruct((t, width), BF16),
        compiler_params=_cparams(("parallel",), 32),
        name="rope_qk_norm",
    )(qkv, cos_t, sin_t, qk_norm)


def _attn_a_kernel(q_ref, k_ref, v_ref, o_ref, s0_ref, s1_ref, macc_ref, *, tk):
    qs = _stack_heads(q_ref[...])

    def score(c):
        return _dot_nt(qs, k_ref[pl.ds(pl.multiple_of(c * tk, tk), tk), :])

    def finish(acc, l):
        o_ref[...] = _unstack_heads(acc / l, GQA_GROUP).astype(o_ref.dtype)

    _pipeline_step(pl.program_id(2), score, v_ref, (s0_ref, s1_ref), macc_ref, finish, v_width=HEAD_DIM, tk=tk)


def _attn_a(qk_rot, qkv, batch, seq, tq=128, tk=1024):
    tk = min(tk, seq)
    in_specs, out_spec = _attn_specs(seq, tq, 0, N_HEADS, N_HEADS + N_KV_HEADS, HEAD_DIM, 1)
    return _pipelined_attention("attn_axial", functools.partial(_attn_a_kernel, tk=tk), (qk_rot, qk_rot, qkv),
                                in_specs, out_spec, batch, seq, tq, tk, GQA_GROUP * tq)


def _window_kernel(slopes_ref, sink_ref, q_ref, k_ref, v_ref, o_ref, lse_ref, s0_ref, s1_ref, macc_ref,
                   *, tq, band, window, nq, seq_c, dil, use_sink, n_steps):
    n = pl.program_id(0)

    def head_and_block(step):
        return (step // nq) % N_KV_HEADS, step % nq

    def band_start(qi):
        if band == seq_c:
            return 0
        return pl.multiple_of(jnp.clip(qi * tq - window, 0, seq_c - band), window)

    @pl.when(n == 0)
    def _():
        s1_ref[...] = jnp.zeros(s1_ref.shape, F32)
        macc_ref[1] = jnp.zeros(macc_ref.shape[1:], F32)

    def run(cur, prev):
        s_cur, s_prev = (s0_ref, s1_ref) if cur == 0 else (s1_ref, s0_ref)
        kh, qi = head_and_block(jnp.minimum(n, n_steps - 1))
        ws = band_start(qi)
        qs = _stack_heads(q_ref[...])
        rel = jnp.abs(_rel_iota(tq, band) + (ws - qi * tq))
        in_window = rel <= window
        dist = rel.astype(F32)
        bias = jnp.concatenate(
            [jnp.where(in_window, -(slopes_ref[kh * GQA_GROUP + g] * (dil * LOG2E)) * dist, NEG)
             for g in range(GQA_GROUP)], axis=0)
        s = _dot_nt(qs, k_ref[pl.ds(ws, band), :]) + bias
        s_cur[...] = s
        macc = s[:, 0:128]
        for j in range(1, band // 128):
            macc = jnp.maximum(macc, s[:, j * 128:(j + 1) * 128])
        macc_ref[cur] = macc
        khp, qip = head_and_block(jnp.maximum(n - 1, 0))
        wsp = band_start(qip)
        m = jnp.max(macc_ref[prev], axis=-1, keepdims=True)
        if use_sink:
            sink = jnp.concatenate(
                [jnp.full((tq, 1), sink_ref[khp * GQA_GROUP + g] * LOG2E, F32) for g in range(GQA_GROUP)], axis=0)
            m = jnp.maximum(m, sink)
        p = jnp.exp2(s_prev[...] - m)
        lacc = p[:, 0:128]
        for j in range(1, band // 128):
            lacc = lacc + p[:, j * 128:(j + 1) * 128]
        l = jnp.sum(lacc, axis=-1, keepdims=True)
        if use_sink:
            l = l + jnp.exp2(sink - m)
        o = _dot(p.astype(BF16), v_ref[pl.ds(wsp, band), :]) / l
        o_ref[...] = _unstack_heads(o, GQA_GROUP).astype(o_ref.dtype)
        lse = m + jnp.log2(l)
        lse_ref[...] = jnp.concatenate(
            [jnp.broadcast_to(lse[g * tq:(g + 1) * tq], (tq, 128 // GQA_GROUP)) for g in range(GQA_GROUP)], axis=1)

    @pl.when(n % 2 == 0)
    def _():
        run(0, 1)

    @pl.when(n % 2 == 1)
    def _():
        run(1, 0)


def _attn_window(qkv_c, slopes, sink, batch, seq, *, window, use_sink):
    dil, t_c, width = qkv_c.shape
    seq_c = seq // dil
    tq = min(256, seq_c)
    band = min(tq + 2 * window, seq_c)
    nq = seq_c // tq
    assert band % 128 == 0 and seq_c % tq == 0, "bands are reduced in 128-lane blocks"
    n_steps = batch * dil * N_KV_HEADS * nq
    qw = GQA_GROUP * HEAD_DIM

    def split(step):
        qi = step % nq
        rest = step // nq
        kh = rest % N_KV_HEADS
        rest = rest // N_KV_HEADS
        return rest // dil, rest % dil, kh, qi

    def q_map(n):
        b, r, kh, qi = split(jnp.minimum(n, n_steps - 1))
        return r, b * nq + qi, kh

    def k_map(n):
        b, r, kh, _ = split(jnp.minimum(n, n_steps - 1))
        return r, b, N_HEADS + kh

    def v_map(n):
        b, r, kh, _ = split(jnp.maximum(n - 1, 0))
        return r, b, N_HEADS + N_KV_HEADS + kh

    def o_map(n):
        b, r, kh, qi = split(jnp.maximum(n - 1, 0))
        return r, b * nq + qi, kh

    return pl.pallas_call(
        functools.partial(_window_kernel, tq=tq, band=band, window=window, nq=nq, seq_c=seq_c, dil=dil,
                          use_sink=use_sink, n_steps=n_steps),
        grid=(n_steps + 1,),
        in_specs=[_SMEM, _SMEM,
                  pl.BlockSpec((None, tq, qw), q_map),
                  pl.BlockSpec((None, seq_c, HEAD_DIM), k_map),
                  pl.BlockSpec((None, seq_c, HEAD_DIM), v_map)],
        out_specs=[pl.BlockSpec((None, tq, qw), o_map), pl.BlockSpec((None, tq, 128), o_map)],
        out_shape=[jax.ShapeDtypeStruct((dil, t_c, D_MODEL), BF16),
                   jax.ShapeDtypeStruct((dil, t_c, N_KV_HEADS * 128), F32)],
        scratch_shapes=[pltpu.VMEM((GQA_GROUP * tq, band), F32), pltpu.VMEM((GQA_GROUP * tq, band), F32),
                        pltpu.VMEM((2, GQA_GROUP * tq, 128), F32)],
        compiler_params=_cparams(("arbitrary",), 40),
        name=f"attn_window_d{dil}",
    )(slopes, sink, qkv_c, qkv_c, qkv_c)


def _attn_c_kernel(slopes_ref, q_ref, k_ref, v_ref, lam_ref, subln_ref, o_ref, s0_ref, s1_ref, macc_ref, bias_ref,
                   *, tq, tk, seq, lam_init):
    kh, qi = pl.program_id(1), pl.program_id(2)
    start = jnp.minimum(qi, seq // tq - 1) * tq
    q = q_ref[...]
    h = HEAD_DIM
    qs0 = jnp.concatenate([q[:, 0:h], q[:, 2 * h:3 * h]], axis=0)
    qs1 = jnp.concatenate([q[:, h:2 * h], q[:, 3 * h:4 * h]], axis=0)
    n_blk = seq // 128

    @pl.when(qi == 0)
    def _():
        lane_minus_row = _rel_iota(tq, 128)
        for u in range(2 * n_blk - 1):
            dist = jnp.abs(lane_minus_row + (u - (n_blk - 1)) * 128).astype(F32)
            for g in range(2):
                bias_ref[g, u] = slopes_ref[2 * kh + g] * dist

    def score(c):
        off = pl.multiple_of(c * tk, tk)
        k = k_ref[pl.ds(off, tk), :]
        u0 = (off - start) // 128 + (n_blk - 1)
        bias = jnp.concatenate(
            [jnp.concatenate([bias_ref[g, u0 + j] for j in range(tk // 128)], axis=1) for g in range(2)], axis=0)
        return jnp.concatenate([_dot_nt(qs0, k[:, :h]) - bias, _dot_nt(qs1, k[:, h:]) - bias], axis=0)

    lp = lam_ref[...]
    lam = (jnp.exp(jnp.sum(lp[0:1] * lp[1:2], axis=-1, keepdims=True))
           - jnp.exp(jnp.sum(lp[2:3] * lp[3:4], axis=-1, keepdims=True)) + lam_init)

    def finish(acc, l):
        a = acc / l
        o = a[:2 * tq] - lam * a[2 * tq:]
        y = o * lax.rsqrt(jnp.mean(o * o, axis=-1, keepdims=True) + EPS) * subln_ref[...] * (1.0 - lam_init)
        o_ref[...] = _unstack_heads(y, 2).astype(o_ref.dtype)

    _pipeline_step(qi, score, v_ref, (s0_ref, s1_ref), macc_ref, finish, v_width=2 * h, tk=tk)


def _attn_c(qkv, slopes_log2, lam_params, subln, batch, seq, lam_init, tq=128, tk=1024):
    assert tq == 128, "the bias table is built for 128-row query blocks"
    tk = min(tk, seq)
    in_specs, out_spec = _attn_specs(seq, tq, 0, 8, 12, 2 * HEAD_DIM, 1)
    in_specs = [_SMEM] + in_specs + [
        pl.BlockSpec((4, HEAD_DIM), lambda b, kh, qi: (0, 0)),
        pl.BlockSpec((1, 2 * HEAD_DIM), lambda b, kh, qi: (0, 0)),
    ]
    return _pipelined_attention(
        "attn_diff", functools.partial(_attn_c_kernel, tq=tq, tk=tk, seq=seq, lam_init=lam_init),
        (slopes_log2, qkv, qkv, qkv, lam_params, subln), in_specs, out_spec, batch, seq, tq, tk, 4 * tq,
        extra_scratch=[pltpu.VMEM((2, 2 * (seq // 128) - 1, tq, 128), F32)])


def _combine_kernel(*refs):
    nb = len(DILATIONS)
    o_refs, l_refs, out_ref, scratch = refs[:nb], refs[nb:2 * nb], refs[2 * nb], refs[2 * nb + 1:]
    tc = out_ref.shape[0]
    o_nat, l_nat = [], []
    for b, (_, dil) in enumerate(DILATIONS):
        if dil == 1:
            o_nat.append(o_refs[b][0].astype(F32))
            l_nat.append(l_refs[b][0])
            continue
        so, sl = scratch[2 * b], scratch[2 * b + 1]
        rows = tc // dil
        for r in range(dil):
            o_r, l_r = o_refs[b][r].astype(F32), l_refs[b][r]
            for jb in range(so.shape[0]):
                so[jb, pl.ds(r, rows, stride=dil), :] = o_r[:, jb * 128:(jb + 1) * 128]
            for jb in range(sl.shape[0]):
                sl[jb, pl.ds(r, rows, stride=dil), :] = l_r[:, jb * 128:(jb + 1) * 128]
        o_nat.append(jnp.concatenate([so[jb] for jb in range(so.shape[0])], axis=1))
        l_nat.append(jnp.concatenate([sl[jb] for jb in range(sl.shape[0])], axis=1))
    lanes = 128 // GQA_GROUP
    for h in range(N_HEADS):
        e = [l[:, h * lanes:h * lanes + 1] for l in l_nat]
        m = functools.reduce(jnp.maximum, e)
        w = [jnp.exp2(x - m) for x in e]
        cols = slice(h * HEAD_DIM, (h + 1) * HEAD_DIM)
        mix = sum(wi * o[:, cols] for wi, o in zip(w, o_nat))
        out_ref[:, cols] = (mix / sum(w)).astype(out_ref.dtype)


def _attn_d(qkv_classes, slopes, batch, seq, tc=256):
    t = qkv_classes[0].shape[1]
    dummy_sink = jnp.zeros((N_HEADS,), F32)
    branches = [_attn_window(qkv_c, slopes, dummy_sink, batch, seq, window=(win // 2) // dil, use_sink=False)
                for qkv_c, (win, dil) in zip(qkv_classes, DILATIONS)]
    lw = N_KV_HEADS * 128
    o_specs = [pl.BlockSpec((dil, tc // dil, D_MODEL), lambda i: (0, i, 0)) for _, dil in DILATIONS]
    l_specs = [pl.BlockSpec((dil, tc // dil, lw), lambda i: (0, i, 0)) for _, dil in DILATIONS]
    scratch = []
    for _ in DILATIONS:
        scratch += [pltpu.VMEM((D_MODEL // 128, tc, 128), F32), pltpu.VMEM((lw // 128, tc, 128), F32)]
    return pl.pallas_call(
        _combine_kernel,
        grid=(t // tc,),
        in_specs=o_specs + l_specs,
        out_specs=pl.BlockSpec((tc, D_MODEL), lambda i: (i, 0)),
        out_shape=jax.ShapeDtypeStruct((t, D_MODEL), BF16),
        scratch_shapes=scratch,
        compiler_params=_cparams(("parallel",), 40),
        name="dilated_combine",
    )(*[b[0] for b in branches], *[b[1] for b in branches])


def _final_norm_kernel(x_ref, g_ref, o_ref):
    x = x_ref[...]
    o_ref[...] = x * lax.rsqrt(jnp.mean(x * x, axis=-1, keepdims=True) + EPS) * g_ref[...]


def _final_norm(x2, g, tm=512):
    t, d = x2.shape
    return pl.pallas_call(
        _final_norm_kernel,
        grid=(t // tm,),
        in_specs=[pl.BlockSpec((tm, d), lambda i: (i, 0)), pl.BlockSpec((1, d), lambda i: (0, 0))],
        out_specs=pl.BlockSpec((tm, d), lambda i: (i, 0)),
        out_shape=jax.ShapeDtypeStruct((t, d), F32),
        compiler_params=_cparams(("parallel",), 32),
        name="final_norm",
    )(x2, g)


def _alibi_slopes(n):
    return 2.0 ** (-8.0 * jnp.arange(1, n + 1, dtype=F32) / n)


def _lambda_init(layer):
    return 0.8 - 0.6 * math.exp(-0.3 * layer)


def _rope_tables(seq):
    rows = seq // GRID_W
    row = jnp.repeat(jnp.arange(rows, dtype=F32), GRID_W)
    col = jnp.tile(jnp.arange(GRID_W, dtype=F32), rows)
    n_freq = HEAD_DIM // 4
    inv_freq = ROPE_THETA ** (-jnp.arange(n_freq, dtype=F32) / n_freq)
    ang = jnp.stack([row, col], axis=-1)[..., None] * inv_freq
    cos, sin = jnp.cos(ang), jnp.sin(ang)
    cos_t = jnp.concatenate([cos[:, 0], cos[:, 0], cos[:, 1], cos[:, 1]], axis=-1)
    sin_t = jnp.concatenate([-sin[:, 0], sin[:, 0], -sin[:, 1], sin[:, 1]], axis=-1)
    return cos_t, sin_t


def _q_colscale(scale, n_q_cols, n_cols):
    return jnp.concatenate([jnp.full((1, n_q_cols), scale, F32), jnp.ones((1, n_cols - n_q_cols), F32)], axis=1)


def _prepare_weights(p):
    pad = D_FF_PAD - D_FF
    w_up = p['ffn_w_up']
    w_up = jnp.concatenate([jnp.pad(w_up[..., :D_FF], ((0, 0), (0, 0), (0, pad))),
                            jnp.pad(w_up[..., D_FF:], ((0, 0), (0, 0), (0, pad)))], axis=-1).astype(BF16)
    return dict(
        a_wqkv=p['a_wqkv'].astype(BF16), a_wo=p['a_wo'].astype(BF16),
        b_wqkv=p['b_wqkv'].astype(BF16), b_wo=p['b_wo'].astype(BF16),
        c_wqkv=p['c_wqkv'].astype(BF16), c_wo=p['c_wo'].astype(BF16),
        d_wqkv=p['d_wqkv'].astype(BF16), d_wo=p['d_wo'].astype(BF16),
        ffn_w_up=w_up,
        ffn_w_down=jnp.pad(p['ffn_w_down'], ((0, 0), (0, pad), (0, 0))).astype(BF16),
        ffn_conv_w=jnp.pad(p['ffn_conv_w'], ((0, 0), (0, 0), (0, pad))),
        ffn_conv_b=jnp.pad(p['ffn_conv_b'], ((0, 0), (0, pad)))[:, None, :],
    )


def _trunk(x, mod_all, row0, p, w):
    batch, seq, d = x.shape
    x2 = x.reshape(batch * seq, d)
    slopes16, slopes8 = _alibi_slopes(N_HEADS), _alibi_slopes(DIFF_HEADS)
    n_qkv = (N_HEADS + 2 * N_KV_HEADS) * HEAD_DIM
    cs_plain = jnp.ones((1, n_qkv), F32)
    cs_q_log2 = _q_colscale(Q_SCALE_LOG2, N_HEADS * HEAD_DIM, n_qkv)
    cs_q_diff = _q_colscale(Q_SCALE_LOG2, DIFF_HEADS * 2 * HEAD_DIM, 2 * D_MODEL)
    for i in range(DEPTH):
        mod = mod_all[i][:, None, :]
        g_attn = p['norm_attn'][i][None, :]
        m, j = i % 4, i // 4
        if m == 0:
            qkv = _norm_matmul(x2, g_attn, mod, row0, seq, 1, 0, w['a_wqkv'][j], cs_plain)
            cos_t, sin_t = _rope_tables(seq)
            qk_norm = jnp.stack([p['a_q_norm'][j], p['a_k_norm'][j]])
            o = _attn_a(_prep_a(qkv, cos_t, sin_t, qk_norm, seq), qkv, batch, seq)
            wo = w['a_wo'][j]
        elif m == 1:
            qkv = _norm_matmul(x2, g_attn, mod, row0, seq, 1, 0, w['b_wqkv'][j], cs_q_log2)
            o, _ = _attn_window(qkv[None], slopes16, p['b_sink'][j].astype(F32), batch, seq, window=WINDOW,
                                use_sink=True)
            o = o[0]
            wo = w['b_wo'][j]
        elif m == 2:
            qkv = _norm_matmul(x2, g_attn, mod, row0, seq, 1, 0, w['c_wqkv'][j], cs_q_diff)
            o = _attn_c(qkv, slopes8 * LOG2E, p['c_lambda'][j].astype(F32), p['c_subln'][j][None, :],
                        batch, seq, _lambda_init(i))
            wo = w['c_wo'][j]
        else:
            dils = tuple(dil for _, dil in DILATIONS if dil > 1)
            qkv, *qkv_classes = _norm_matmul(x2, g_attn, mod, row0, seq, 1, 0, w['d_wqkv'][j], cs_q_log2, dils=dils)
            o = _attn_d([qkv[None], *qkv_classes], slopes16, batch, seq)
            wo = w['d_wo'][j]
        x2 = _out_residual(o, wo, x2, mod, row0, seq, 2)
        x2 = _ffn(x2, p['norm_ffn'][i][None, :], mod, row0, seq, w['ffn_w_up'][i], w['ffn_conv_w'][i],
                  w['ffn_conv_b'][i], w['ffn_w_down'][i])
    return _final_norm(x2, p['norm_final'][None, :]).reshape(batch, seq, d)


def kernel(x_prompt, x_sample, c_prompt, c_sample, norm_attn, norm_ffn, w_ada, b_ada, a_wqkv, a_q_norm, a_k_norm, a_wo, b_wqkv, b_sink, b_wo, c_wqkv, c_lambda, c_subln, c_wo, d_wqkv, d_wo, ffn_w_up, ffn_conv_w, ffn_conv_b, ffn_w_down, norm_final):
    p = dict(norm_attn=norm_attn, norm_ffn=norm_ffn, a_q_norm=a_q_norm, a_k_norm=a_k_norm, b_sink=b_sink,
             c_lambda=c_lambda, c_subln=c_subln, norm_final=norm_final,
             a_wqkv=a_wqkv, a_wo=a_wo, b_wqkv=b_wqkv, b_wo=b_wo, c_wqkv=c_wqkv, c_wo=c_wo,
             d_wqkv=d_wqkv, d_wo=d_wo, ffn_w_up=ffn_w_up, ffn_conv_w=ffn_conv_w, ffn_conv_b=ffn_conv_b,
             ffn_w_down=ffn_w_down)
    w = _prepare_weights(p)
    n_prompt, n_sample = c_prompt.shape[0], c_sample.shape[0]
    rows = -(-(n_prompt + n_sample) // 8) * 8
    c_all = jnp.concatenate([c_prompt, c_sample, jnp.zeros((rows - n_prompt - n_sample, D_MODEL), F32)], axis=0)
    mod_all = _modulation(c_all, w_ada, b_ada)
    y_prompt = _trunk(x_prompt, mod_all, 0, p, w)
    y_sample = _trunk(x_sample, mod_all, n_prompt, p, w)
    return (y_prompt, y_sample)
```

```python
import functools
import math

import jax
import jax.numpy as jnp
from jax import lax
from jax.experimental import pallas as pl
from jax.experimental.pallas import tpu as pltpu

F32 = jnp.float32
BF16 = jnp.bfloat16

D_MODEL = 2048
HEAD_DIM = 128
N_HEADS = 16
N_KV_HEADS = 4
GQA_GROUP = 4
DIFF_HEADS = 8
D_FF = 5504
FF_TILE = 512
D_FF_PAD = 5632
N_MOD = 6
DEPTH = 4
GRID_W = 64
ROPE_THETA = 10000.0
EPS = 1e-6
NEG = -1e30
WINDOW = 128
DILATIONS = ((128, 1), (512, 4), (2048, 16))
Q_SCALE = HEAD_DIM ** -0.5
LOG2E = math.log2(math.e)
Q_SCALE_LOG2 = Q_SCALE * LOG2E
HALO = 16
MIB = 1024 * 1024


def _cparams(sem, vmem_mib):
    return pltpu.CompilerParams(dimension_semantics=sem, vmem_limit_bytes=vmem_mib * MIB)


def _dot(a, b):
    return jnp.dot(a, b, preferred_element_type=F32)


def _dot_nt(a, b):
    return lax.dot_general(a, b, (((1,), (1,)), ((), ())), preferred_element_type=F32)


def _norm_mod(x, g, sc, sh):
    y = x * lax.rsqrt(jnp.mean(x * x, axis=-1, keepdims=True) + EPS) * g
    return y * (1.0 + sc) + sh


def _mod_kernel(c_ref, w_ref, b_ref, o_ref):
    c = c_ref[...]
    a = (c / (1.0 + jnp.exp(-c))).astype(BF16)
    o_ref[0] = _dot(a, w_ref[0].astype(BF16)) + b_ref[0]


def _modulation(c_all, w_ada, b_ada, tn=1024):
    nb, d = c_all.shape
    depth, _, n = w_ada.shape
    return pl.pallas_call(
        _mod_kernel,
        grid=(depth, n // tn),
        in_specs=[
            pl.BlockSpec((nb, d), lambda l, j: (0, 0)),
            pl.BlockSpec((1, d, tn), lambda l, j: (l, 0, j)),
            pl.BlockSpec((1, 1, tn), lambda l, j: (l, 0, j)),
        ],
        out_specs=pl.BlockSpec((1, nb, tn), lambda l, j: (l, 0, j)),
        out_shape=jax.ShapeDtypeStruct((depth, nb, n), F32),
        compiler_params=_cparams(("parallel", "parallel"), 40),
        name="adaln_modulation",
    )(c_all, w_ada, b_ada.reshape(depth, 1, n))


def _norm_matmul_kernel(x_ref, g_ref, sc_ref, sh_ref, w_ref, cs_ref, *refs, dils):
    o_ref, class_refs, h_ref = refs[0], refs[1:1 + len(dils)], refs[1 + len(dils)]

    def project(h):
        y = _dot(h, w_ref[...]) * cs_ref[...]
        o_ref[...] = y.astype(o_ref.dtype)
        if dils:
            y_ref = refs[-1]
            nblk, rows, _ = y_ref.shape
            for jb in range(nblk):
                y_ref[jb] = y[:, jb * 128:(jb + 1) * 128]
            for dil, c_ref in zip(dils, class_refs):
                for r in range(dil):
                    c_ref[r] = jnp.concatenate(
                        [y_ref[jb, pl.ds(r, rows // dil, stride=dil), :] for jb in range(nblk)],
                        axis=1).astype(c_ref.dtype)

    @pl.when(pl.program_id(1) == 0)
    def _():
        h = _norm_mod(x_ref[...], g_ref[...], sc_ref[0], sh_ref[0]).astype(BF16)
        h_ref[...] = h
        project(h)

    @pl.when(pl.program_id(1) > 0)
    def _():
        project(h_ref[...])


def _norm_matmul(x2, g, mod, row0, seq, k_sc, k_sh, w, colscale, tm=1024, tn=1024, dils=()):
    t, d = x2.shape
    n = w.shape[1]
    tm = min(tm, seq)
    if dils:
        tn //= 2
    brow = lambda i, j: row0 + (i * tm) // seq
    out_specs = [pl.BlockSpec((tm, tn), lambda i, j: (i, j))]
    out_shape = [jax.ShapeDtypeStruct((t, n), BF16)]
    scratch = [pltpu.VMEM((tm, d), BF16)]
    for dil in dils:
        out_specs.append(pl.BlockSpec((dil, tm // dil, tn), lambda i, j: (0, i, j)))
        out_shape.append(jax.ShapeDtypeStruct((dil, t // dil, n), BF16))
    if dils:
        scratch.append(pltpu.VMEM((tn // 128, tm, 128), F32))
    outs = pl.pallas_call(
        functools.partial(_norm_matmul_kernel, dils=tuple(dils)),
        grid=(t // tm, n // tn),
        in_specs=[
            pl.BlockSpec((tm, d), lambda i, j: (i, 0)),
            pl.BlockSpec((1, d), lambda i, j: (0, 0)),
            pl.BlockSpec((1, 1, d), lambda i, j: (brow(i, j), 0, k_sc)),
            pl.BlockSpec((1, 1, d), lambda i, j: (brow(i, j), 0, k_sh)),
            pl.BlockSpec((d, tn), lambda i, j: (0, j)),
            pl.BlockSpec((1, tn), lambda i, j: (0, j)),
        ],
        out_specs=out_specs,
        out_shape=out_shape,
        scratch_shapes=scratch,
        compiler_params=_cparams(("parallel", "arbitrary"), 52),
        name="norm_qkv_proj",
    )(x2, g, mod, mod, w, colscale)
    return outs if dils else outs[0]


def _out_res_kernel(o_ref, w_ref, x_ref, gate_ref, out_ref):
    out_ref[...] = x_ref[...] + gate_ref[0] * _dot(o_ref[...], w_ref[...])


def _out_residual(o2, w, x2, mod, row0, seq, k_gate, tm=1024, tn=1024):
    t, d = x2.shape
    kdim = o2.shape[1]
    tm = min(tm, seq)
    return pl.pallas_call(
        _out_res_kernel,
        grid=(t // tm, d // tn),
        in_specs=[
            pl.BlockSpec((tm, kdim), lambda i, j: (i, 0)),
            pl.BlockSpec((kdim, tn), lambda i, j: (0, j)),
            pl.BlockSpec((tm, tn), lambda i, j: (i, j)),
            pl.BlockSpec((1, 1, tn), lambda i, j: (row0 + (i * tm) // seq, 0, k_gate * (d // tn) + j)),
        ],
        out_specs=pl.BlockSpec((tm, tn), lambda i, j: (i, j)),
        out_shape=jax.ShapeDtypeStruct((t, d), F32),
        compiler_params=_cparams(("parallel", "parallel"), 48),
        name="out_proj_residual",
    )(o2, w, x2, mod)


def _ffn_kernel(xp_ref, x_ref, xn_ref, g_ref, sc_ref, sh_ref, gate_ref, wa_ref, wb_ref,
                cw_ref, cb_ref, wd_ref, out_ref, h_ref, *, tm, seq):
    i = pl.program_id(0)
    f = pl.program_id(1)
    n_ext = tm + 2 * HALO

    def partial_down(h_ext):
        a_ext = _dot(h_ext, wa_ref[...])
        b = _dot(h_ext[HALO:HALO + tm], wb_ref[...])
        a_prev = pltpu.roll(a_ext, 1, 0)[HALO:HALO + tm]
        a_next = pltpu.roll(a_ext, n_ext - 1, 0)[HALO:HALO + tm]
        a_cur = a_ext[HALO:HALO + tm]
        a = cb_ref[...] + (a_prev * cw_ref[0:1, :] + a_cur * cw_ref[1:2, :] + a_next * cw_ref[2:3, :])
        gelu = 0.5 * a * (1.0 + lax.erf(a * (2.0 ** -0.5)))
        return _dot((gelu * b).astype(BF16), wd_ref[...])

    @pl.when(f == 0)
    def _():
        g, sc, sh = g_ref[...], sc_ref[0], sh_ref[0]
        at_start = (i * tm) % seq == 0
        at_end = ((i + 1) * tm) % seq == 0
        hp = jnp.where(at_start, 0.0, _norm_mod(xp_ref[...], g, sc, sh))
        hn = jnp.where(at_end, 0.0, _norm_mod(xn_ref[...], g, sc, sh))
        h_ext = jnp.concatenate([hp, _norm_mod(x_ref[...], g, sc, sh), hn], axis=0).astype(BF16)
        h_ref[...] = h_ext
        out_ref[...] = partial_down(h_ext)

    @pl.when(f > 0)
    def _():
        out_ref[...] += partial_down(h_ref[...])

    @pl.when(f == pl.num_programs(1) - 1)
    def _():
        out_ref[...] = x_ref[...] + gate_ref[0] * out_ref[...]


def _ffn(x2, g, mod, row0, seq, w_up, conv_w, conv_b, w_down, tm=512, tf=FF_TILE):
    t, d = x2.shape
    dff = w_down.shape[0]
    nf = dff // tf
    tm = min(tm, seq)
    hb = tm // HALO
    brow = lambda i, f: row0 + (i * tm) // seq
    return pl.pallas_call(
        functools.partial(_ffn_kernel, tm=tm, seq=seq),
        grid=(t // tm, nf),
        in_specs=[
            pl.BlockSpec((HALO, d), lambda i, f: (jnp.maximum(i * hb - 1, 0), 0)),
            pl.BlockSpec((tm, d), lambda i, f: (i, 0)),
            pl.BlockSpec((HALO, d), lambda i, f: (jnp.minimum((i + 1) * hb, t // HALO - 1), 0)),
            pl.BlockSpec((1, d), lambda i, f: (0, 0)),
            pl.BlockSpec((1, 1, d), lambda i, f: (brow(i, f), 0, 4)),
            pl.BlockSpec((1, 1, d), lambda i, f: (brow(i, f), 0, 3)),
            pl.BlockSpec((1, 1, d), lambda i, f: (brow(i, f), 0, 5)),
            pl.BlockSpec((d, tf), lambda i, f: (0, f)),
            pl.BlockSpec((d, tf), lambda i, f: (0, nf + f)),
            pl.BlockSpec((3, tf), lambda i, f: (0, f)),
            pl.BlockSpec((1, tf), lambda i, f: (0, f)),
            pl.BlockSpec((tf, d), lambda i, f: (f, 0)),
        ],
        out_specs=pl.BlockSpec((tm, d), lambda i, f: (i, 0)),
        out_shape=jax.ShapeDtypeStruct((t, d), F32),
        scratch_shapes=[pltpu.VMEM((tm + 2 * HALO, d), BF16)],
        compiler_params=_cparams(("parallel", "arbitrary"), 56),
        name="conv_glu_mlp",
    )(x2, x2, x2, g, mod, mod, mod, w_up, w_up, conv_w, conv_b, w_down)


def _stack_heads(q, width=HEAD_DIM):
    n = q.shape[1] // width
    return jnp.concatenate([q[:, i * width:(i + 1) * width] for i in range(n)], axis=0)


def _unstack_heads(o, n):
    tq = o.shape[0] // n
    return jnp.concatenate([o[i * tq:(i + 1) * tq] for i in range(n)], axis=1)


def _rel_iota(tq, tk):
    return lax.broadcasted_iota(jnp.int32, (tq, tk), 1) - lax.broadcasted_iota(jnp.int32, (tq, tk), 0)


def _attn_specs(seq, tq, q_blk0, k_blk0, v_blk0, kv_width, lag):
    nq = seq // tq
    q_row = lambda b, qi: b * nq + jnp.minimum(qi, nq - 1)
    o_row = lambda b, qi: b * nq + jnp.maximum(qi - lag, 0)
    return [
        pl.BlockSpec((tq, 4 * HEAD_DIM), lambda b, kh, qi: (q_row(b, qi), q_blk0 + kh)),
        pl.BlockSpec((seq, kv_width), lambda b, kh, qi: (b, k_blk0 + kh)),
        pl.BlockSpec((seq, kv_width), lambda b, kh, qi: (b, v_blk0 + kh)),
    ], pl.BlockSpec((tq, 4 * HEAD_DIM), lambda b, kh, qi: (o_row(b, qi), kh))


_SMEM = pl.BlockSpec(memory_space=pltpu.SMEM)
_ATTN_SEM = ("parallel", "parallel", "arbitrary")


def _with_ones(v):
    return jnp.concatenate([v, jnp.ones(v.shape, v.dtype)], axis=1)


def _pipeline_step(qi, score_fn, v_ref, s_ref, macc_ref, finish, *, v_width, tk, mxu_row_sums):
    nck, m_rows, _ = s_ref.shape

    @pl.when(qi == 0)
    def _():
        s_ref[...] = jnp.zeros(s_ref.shape, F32)
        macc_ref[...] = jnp.zeros(macc_ref.shape, F32)

    m_prev = jnp.max(macc_ref[...], axis=-1, keepdims=True)

    def body(c, carry):
        macc, lacc, acc = carry
        x = s_ref[c] - m_prev
        s = score_fn(c)
        s_ref[c] = s
        for j in range(tk // 128):
            macc = jnp.maximum(macc, s[:, j * 128:(j + 1) * 128])
        v = v_ref[pl.ds(pl.multiple_of(c * tk, tk), tk), :]
        if mxu_row_sums:
            acc = acc + _dot(jnp.exp2(x.astype(BF16)), _with_ones(v))
        else:
            p = jnp.exp2(x)
            for j in range(tk // 128):
                lacc = lacc + p[:, j * 128:(j + 1) * 128]
            acc = acc + _dot(p.astype(BF16), v)
        return macc, lacc, acc

    acc_width = 2 * v_width if mxu_row_sums else v_width
    init = (jnp.full((m_rows, 128), NEG, F32), jnp.zeros((m_rows, 128), F32),
            jnp.zeros((m_rows, acc_width), F32))
    macc, lacc, acc = lax.fori_loop(0, nck, body, init, unroll=True)
    macc_ref[...] = macc
    if mxu_row_sums:
        finish(acc[:, :v_width], acc[:, v_width:v_width + 1])
    else:
        finish(acc, jnp.sum(lacc, axis=-1, keepdims=True))


def _pipelined_attention(name, kernel_fn, operands, in_specs, out_spec, batch, seq, tq, tk, m_rows, extra_scratch=()):
    nck = seq // tk
    return pl.pallas_call(
        kernel_fn,
        grid=(batch, N_KV_HEADS, seq // tq + 1),
        in_specs=in_specs,
        out_specs=out_spec,
        out_shape=jax.ShapeDtypeStruct((batch * seq, D_MODEL), BF16),
        scratch_shapes=[pltpu.VMEM((nck, m_rows, tk), F32), pltpu.VMEM((m_rows, 128), F32), *extra_scratch],
        compiler_params=_cparams(_ATTN_SEM, 52),
        name=name,
    )(*operands)


def _prep_a_kernel(qkv_ref, cos_ref, sin_ref, w_ref, o_ref):
    tp = qkv_ref.shape[0]
    cos, sin = cos_ref[...], sin_ref[...]
    lane = lax.broadcasted_iota(jnp.int32, (tp, HEAD_DIM), 1)
    low_half = (lane % (HEAD_DIM // 2)) < (HEAD_DIM // 4)
    for h in range(N_HEADS + N_KV_HEADS):
        cols = slice(h * HEAD_DIM, (h + 1) * HEAD_DIM)
        x = qkv_ref[:, cols].astype(F32)
        is_q = h < N_HEADS
        y = x * lax.rsqrt(jnp.mean(x * x, axis=-1, keepdims=True) + EPS) * (w_ref[0:1, :] if is_q else w_ref[1:2, :])
        partner = jnp.where(low_half, pltpu.roll(y, 3 * HEAD_DIM // 4, 1), pltpu.roll(y, HEAD_DIM // 4, 1))
        r = y * cos + partner * sin
        if is_q:
            r = r * Q_SCALE_LOG2
        o_ref[:, cols] = r.astype(BF16)


def _prep_a(qkv, cos_t, sin_t, qk_norm, seq, tp=512):
    t = qkv.shape[0]
    tp = min(tp, seq)
    width = (N_HEADS + N_KV_HEADS) * HEAD_DIM
    return pl.pallas_call(
        _prep_a_kernel,
        grid=(t // tp,),
        in_specs=[
            pl.BlockSpec((tp, width), lambda i: (i, 0)),
            pl.BlockSpec((tp, HEAD_DIM), lambda i: (i % (seq // tp), 0)),
            pl.BlockSpec((tp, HEAD_DIM), lambda i: (i % (seq // tp), 0)),
            pl.BlockSpec((2, HEAD_DIM), lambda i: (0, 0)),
        ],
        out_specs=pl.BlockSpec((tp, width), lambda i: (i, 0)),
        out_shape=jax.ShapeDtypeStruct((t, width), BF16),
        compiler_params=_cparams(("parallel",), 32),
        name="rope_qk_norm",
    )(qkv, cos_t, sin_t, qk_norm)


def _attn_a_kernel(q_ref, k_ref, v_ref, o_ref, s_ref, macc_ref, *, tk):
    qs = _stack_heads(q_ref[...])

    def score(c):
        return _dot_nt(qs, k_ref[pl.ds(pl.multiple_of(c * tk, tk), tk), :])

    def finish(acc, l):
        o_ref[...] = _unstack_heads(acc / l, GQA_GROUP).astype(o_ref.dtype)

    _pipeline_step(pl.program_id(2), score, v_ref, s_ref, macc_ref, finish, v_width=HEAD_DIM, tk=tk,
                   mxu_row_sums=True)


def _attn_a(qk_rot, qkv, batch, seq, tq=128, tk=1024):
    tk = min(tk, seq)
    in_specs, out_spec = _attn_specs(seq, tq, 0, N_HEADS, N_HEADS + N_KV_HEADS, HEAD_DIM, 1)
    return _pipelined_attention("attn_axial", functools.partial(_attn_a_kernel, tk=tk), (qk_rot, qk_rot, qkv),
                                in_specs, out_spec, batch, seq, tq, tk, GQA_GROUP * tq)


def _window_kernel(slopes_ref, sink_ref, q_ref, k_ref, v_ref, o_ref, lse_ref, s0_ref, s1_ref, macc_ref,
                   *, tq, band, window, nq, seq_c, dil, use_sink, n_steps):
    n = pl.program_id(0)

    def head_and_block(step):
        return (step // nq) % N_KV_HEADS, step % nq

    def band_start(qi):
        if band == seq_c:
            return 0
        return pl.multiple_of(jnp.clip(qi * tq - window, 0, seq_c - band), window)

    @pl.when(n == 0)
    def _():
        s1_ref[...] = jnp.zeros(s1_ref.shape, F32)
        macc_ref[1] = jnp.zeros(macc_ref.shape[1:], F32)

    def run(cur, prev):
        s_cur, s_prev = (s0_ref, s1_ref) if cur == 0 else (s1_ref, s0_ref)
        kh, qi = head_and_block(jnp.minimum(n, n_steps - 1))
        ws = band_start(qi)
        qs = _stack_heads(q_ref[...])
        rel = jnp.abs(_rel_iota(tq, band) + (ws - qi * tq))
        in_window = rel <= window
        dist = rel.astype(F32)
        bias = jnp.concatenate(
            [jnp.where(in_window, -(slopes_ref[kh * GQA_GROUP + g] * (dil * LOG2E)) * dist, NEG)
             for g in range(GQA_GROUP)], axis=0)
        s = _dot_nt(qs, k_ref[pl.ds(ws, band), :]) + bias
        s_cur[...] = s
        macc = s[:, 0:128]
        for j in range(1, band // 128):
            macc = jnp.maximum(macc, s[:, j * 128:(j + 1) * 128])
        macc_ref[cur] = macc
        khp, qip = head_and_block(jnp.maximum(n - 1, 0))
        wsp = band_start(qip)
        m = jnp.max(macc_ref[prev], axis=-1, keepdims=True)
        if use_sink:
            sink = jnp.concatenate(
                [jnp.full((tq, 1), sink_ref[khp * GQA_GROUP + g] * LOG2E, F32) for g in range(GQA_GROUP)], axis=0)
            m = jnp.maximum(m, sink)
        p = jnp.exp2((s_prev[...] - m).astype(BF16))
        ol = _dot(p, _with_ones(v_ref[pl.ds(wsp, band), :]))
        l = ol[:, HEAD_DIM:HEAD_DIM + 1]
        if use_sink:
            l = l + jnp.exp2(sink - m)
        o = ol[:, :HEAD_DIM] / l
        o_ref[...] = _unstack_heads(o, GQA_GROUP).astype(o_ref.dtype)
        lse = m + jnp.log2(l)
        lse_ref[...] = jnp.concatenate(
            [jnp.broadcast_to(lse[g * tq:(g + 1) * tq], (tq, 128 // GQA_GROUP)) for g in range(GQA_GROUP)], axis=1)

    @pl.when(n % 2 == 0)
    def _():
        run(0, 1)

    @pl.when(n % 2 == 1)
    def _():
        run(1, 0)


def _attn_window(qkv_c, slopes, sink, batch, seq, *, window, use_sink):
    dil, t_c, width = qkv_c.shape
    seq_c = seq // dil
    tq = min(256, seq_c)
    band = min(tq + 2 * window, seq_c)
    nq = seq_c // tq
    assert band % 128 == 0 and seq_c % tq == 0, "bands are reduced in 128-lane blocks"
    n_steps = batch * dil * N_KV_HEADS * nq
    qw = GQA_GROUP * HEAD_DIM

    def split(step):
        qi = step % nq
        rest = step // nq
        kh = rest % N_KV_HEADS
        rest = rest // N_KV_HEADS
        return rest // dil, rest % dil, kh, qi

    def q_map(n):
        b, r, kh, qi = split(jnp.minimum(n, n_steps - 1))
        return r, b * nq + qi, kh

    def k_map(n):
        b, r, kh, _ = split(jnp.minimum(n, n_steps - 1))
        return r, b, N_HEADS + kh

    def v_map(n):
        b, r, kh, _ = split(jnp.maximum(n - 1, 0))
        return r, b, N_HEADS + N_KV_HEADS + kh

    def o_map(n):
        b, r, kh, qi = split(jnp.maximum(n - 1, 0))
        return r, b * nq + qi, kh

    return pl.pallas_call(
        functools.partial(_window_kernel, tq=tq, band=band, window=window, nq=nq, seq_c=seq_c, dil=dil,
                          use_sink=use_sink, n_steps=n_steps),
        grid=(n_steps + 1,),
        in_specs=[_SMEM, _SMEM,
                  pl.BlockSpec((None, tq, qw), q_map),
                  pl.BlockSpec((None, seq_c, HEAD_DIM), k_map),
                  pl.BlockSpec((None, seq_c, HEAD_DIM), v_map)],
        out_specs=[pl.BlockSpec((None, tq, qw), o_map), pl.BlockSpec((None, tq, 128), o_map)],
        out_shape=[jax.ShapeDtypeStruct((dil, t_c, D_MODEL), BF16),
                   jax.ShapeDtypeStruct((dil, t_c, N_KV_HEADS * 128), F32)],
        scratch_shapes=[pltpu.VMEM((GQA_GROUP * tq, band), F32), pltpu.VMEM((GQA_GROUP * tq, band), F32),
                        pltpu.VMEM((2, GQA_GROUP * tq, 128), F32)],
        compiler_params=_cparams(("arbitrary",), 40),
        name=f"attn_window_d{dil}",
    )(slopes, sink, qkv_c, qkv_c, qkv_c)


def _attn_c_kernel(slopes_ref, q_ref, k_ref, v_ref, lam_ref, subln_ref, o_ref, s_ref, macc_ref, bias_ref,
                   *, tq, tk, seq, lam_init):
    kh, qi = pl.program_id(1), pl.program_id(2)
    start = jnp.minimum(qi, seq // tq - 1) * tq
    q = q_ref[...]
    h = HEAD_DIM
    qs0 = jnp.concatenate([q[:, 0:h], q[:, 2 * h:3 * h]], axis=0)
    qs1 = jnp.concatenate([q[:, h:2 * h], q[:, 3 * h:4 * h]], axis=0)
    n_blk = seq // 128

    @pl.when(qi == 0)
    def _():
        lane_minus_row = _rel_iota(128, 128)
        for u in range(2 * n_blk - 1):
            dist = jnp.abs(lane_minus_row + (u - (n_blk - 1)) * 128).astype(F32)
            for g in range(2):
                bias_ref[g, u] = slopes_ref[2 * kh + g] * dist

    def score(c):
        off = pl.multiple_of(c * tk, tk)
        k = k_ref[pl.ds(off, tk), :]
        u0 = (off - start) // 128 + (n_blk - 1)
        bias = jnp.concatenate(
            [jnp.concatenate([bias_ref[g, u0 + j - rb] for j in range(tk // 128)], axis=1)
             for g in range(2) for rb in range(tq // 128)], axis=0)
        return jnp.concatenate([_dot_nt(qs0, k[:, :h]) - bias, _dot_nt(qs1, k[:, h:]) - bias], axis=0)

    lp = lam_ref[...]
    lam = (jnp.exp(jnp.sum(lp[0:1] * lp[1:2], axis=-1, keepdims=True))
           - jnp.exp(jnp.sum(lp[2:3] * lp[3:4], axis=-1, keepdims=True)) + lam_init)

    def finish(acc, l):
        a = acc / l
        o = a[:2 * tq] - lam * a[2 * tq:]
        y = o * lax.rsqrt(jnp.mean(o * o, axis=-1, keepdims=True) + EPS) * subln_ref[...] * (1.0 - lam_init)
        o_ref[...] = _unstack_heads(y, 2).astype(o_ref.dtype)

    _pipeline_step(qi, score, v_ref, s_ref, macc_ref, finish, v_width=2 * h, tk=tk, mxu_row_sums=False)


def _attn_c(qkv, slopes_log2, lam_params, subln, batch, seq, lam_init, tq=128, tk=1024):
    assert tq % 128 == 0, "the bias table holds 128 x 128 tiles"
    tk = min(tk, seq)
    in_specs, out_spec = _attn_specs(seq, tq, 0, 8, 12, 2 * HEAD_DIM, 1)
    in_specs = [_SMEM] + in_specs + [
        pl.BlockSpec((4, HEAD_DIM), lambda b, kh, qi: (0, 0)),
        pl.BlockSpec((1, 2 * HEAD_DIM), lambda b, kh, qi: (0, 0)),
    ]
    return _pipelined_attention(
        "attn_diff", functools.partial(_attn_c_kernel, tq=tq, tk=tk, seq=seq, lam_init=lam_init),
        (slopes_log2, qkv, qkv, qkv, lam_params, subln), in_specs, out_spec, batch, seq, tq, tk, 4 * tq,
        extra_scratch=[pltpu.VMEM((2, 2 * (seq // 128) - 1, 128, 128), F32)])


def _combine_kernel(*refs):
    nb = len(DILATIONS)
    o_refs, l_refs, out_ref, scratch = refs[:nb], refs[nb:2 * nb], refs[2 * nb], refs[2 * nb + 1:]
    tc = out_ref.shape[0]
    o_nat, l_nat = [], []
    for b, (_, dil) in enumerate(DILATIONS):
        if dil == 1:
            o_nat.append(o_refs[b][0].astype(F32))
            l_nat.append(l_refs[b][0])
            continue
        so, sl = scratch[2 * b], scratch[2 * b + 1]
        rows = tc // dil
        for r in range(dil):
            o_r, l_r = o_refs[b][r].astype(F32), l_refs[b][r]
            for jb in range(so.shape[0]):
                so[jb, pl.ds(r, rows, stride=dil), :] = o_r[:, jb * 128:(jb + 1) * 128]
            for jb in range(sl.shape[0]):
                sl[jb, pl.ds(r, rows, stride=dil), :] = l_r[:, jb * 128:(jb + 1) * 128]
        o_nat.append(jnp.concatenate([so[jb] for jb in range(so.shape[0])], axis=1))
        l_nat.append(jnp.concatenate([sl[jb] for jb in range(sl.shape[0])], axis=1))
    lanes = 128 // GQA_GROUP
    for h in range(N_HEADS):
        e = [l[:, h * lanes:h * lanes + 1] for l in l_nat]
        m = functools.reduce(jnp.maximum, e)
        w = [jnp.exp2(x - m) for x in e]
        cols = slice(h * HEAD_DIM, (h + 1) * HEAD_DIM)
        mix = sum(wi * o[:, cols] for wi, o in zip(w, o_nat))
        out_ref[:, cols] = (mix / sum(w)).astype(out_ref.dtype)


def _attn_d(qkv_classes, slopes, batch, seq, tc=256):
    t = qkv_classes[0].shape[1]
    dummy_sink = jnp.zeros((N_HEADS,), F32)
    branches = [_attn_window(qkv_c, slopes, dummy_sink, batch, seq, window=(win // 2) // dil, use_sink=False)
                for qkv_c, (win, dil) in zip(qkv_classes, DILATIONS)]
    lw = N_KV_HEADS * 128
    o_specs = [pl.BlockSpec((dil, tc // dil, D_MODEL), lambda i: (0, i, 0)) for _, dil in DILATIONS]
    l_specs = [pl.BlockSpec((dil, tc // dil, lw), lambda i: (0, i, 0)) for _, dil in DILATIONS]
    scratch = []
    for _ in DILATIONS:
        scratch += [pltpu.VMEM((D_MODEL // 128, tc, 128), F32), pltpu.VMEM((lw // 128, tc, 128), F32)]
    return pl.pallas_call(
        _combine_kernel,
        grid=(t // tc,),
        in_specs=o_specs + l_specs,
        out_specs=pl.BlockSpec((tc, D_MODEL), lambda i: (i, 0)),
        out_shape=jax.ShapeDtypeStruct((t, D_MODEL), BF16),
        scratch_shapes=scratch,
        compiler_params=_cparams(("parallel",), 40),
        name="dilated_combine",
    )(*[b[0] for b in branches], *[b[1] for b in branches])


def _final_norm_kernel(x_ref, g_ref, o_ref):
    x = x_ref[...]
    o_ref[...] = x * lax.rsqrt(jnp.mean(x * x, axis=-1, keepdims=True) + EPS) * g_ref[...]


def _final_norm(x2, g, tm=512):
    t, d = x2.shape
    return pl.pallas_call(
        _final_norm_kernel,
        grid=(t // tm,),
        in_specs=[pl.BlockSpec((tm, d), lambda i: (i, 0)), pl.BlockSpec((1, d), lambda i: (0, 0))],
        out_specs=pl.BlockSpec((tm, d), lambda i: (i, 0)),
        out_shape=jax.ShapeDtypeStruct((t, d), F32),
        compiler_params=_cparams(("parallel",), 32),
        name="final_norm",
    )(x2, g)


def _alibi_slopes(n):
    return 2.0 ** (-8.0 * jnp.arange(1, n + 1, dtype=F32) / n)


def _lambda_init(layer):
    return 0.8 - 0.6 * math.exp(-0.3 * layer)


def _rope_tables(seq):
    rows = seq // GRID_W
    row = jnp.repeat(jnp.arange(rows, dtype=F32), GRID_W)
    col = jnp.tile(jnp.arange(GRID_W, dtype=F32), rows)
    n_freq = HEAD_DIM // 4
    inv_freq = ROPE_THETA ** (-jnp.arange(n_freq, dtype=F32) / n_freq)
    ang = jnp.stack([row, col], axis=-1)[..., None] * inv_freq
    cos, sin = jnp.cos(ang), jnp.sin(ang)
    cos_t = jnp.concatenate([cos[:, 0], cos[:, 0], cos[:, 1], cos[:, 1]], axis=-1)
    sin_t = jnp.concatenate([-sin[:, 0], sin[:, 0], -sin[:, 1], sin[:, 1]], axis=-1)
    return cos_t, sin_t


def _q_colscale(scale, n_q_cols, n_cols):
    return jnp.concatenate([jnp.full((1, n_q_cols), scale, F32), jnp.ones((1, n_cols - n_q_cols), F32)], axis=1)


def _prepare_weights(p):
    pad = D_FF_PAD - D_FF
    w_up = p['ffn_w_up']
    w_up = jnp.concatenate([jnp.pad(w_up[..., :D_FF], ((0, 0), (0, 0), (0, pad))),
                            jnp.pad(w_up[..., D_FF:], ((0, 0), (0, 0), (0, pad)))], axis=-1).astype(BF16)
    return dict(
        a_wqkv=p['a_wqkv'].astype(BF16), a_wo=p['a_wo'].astype(BF16),
        b_wqkv=p['b_wqkv'].astype(BF16), b_wo=p['b_wo'].astype(BF16),
        c_wqkv=p['c_wqkv'].astype(BF16), c_wo=p['c_wo'].astype(BF16),
        d_wqkv=p['d_wqkv'].astype(BF16), d_wo=p['d_wo'].astype(BF16),
        ffn_w_up=w_up,
        ffn_w_down=jnp.pad(p['ffn_w_down'], ((0, 0), (0, pad), (0, 0))).astype(BF16),
        ffn_conv_w=jnp.pad(p['ffn_conv_w'], ((0, 0), (0, 0), (0, pad))),
        ffn_conv_b=jnp.pad(p['ffn_conv_b'], ((0, 0), (0, pad)))[:, None, :],
    )


def _trunk(x, mod_all, row0, p, w):
    batch, seq, d = x.shape
    x2 = x.reshape(batch * seq, d)
    slopes16, slopes8 = _alibi_slopes(N_HEADS), _alibi_slopes(DIFF_HEADS)
    n_qkv = (N_HEADS + 2 * N_KV_HEADS) * HEAD_DIM
    cs_plain = jnp.ones((1, n_qkv), F32)
    cs_q_log2 = _q_colscale(Q_SCALE_LOG2, N_HEADS * HEAD_DIM, n_qkv)
    cs_q_diff = _q_colscale(Q_SCALE_LOG2, DIFF_HEADS * 2 * HEAD_DIM, 2 * D_MODEL)
    for i in range(DEPTH):
        mod = mod_all[i][:, None, :]
        g_attn = p['norm_attn'][i][None, :]
        m, j = i % 4, i // 4
        if m == 0:
            qkv = _norm_matmul(x2, g_attn, mod, row0, seq, 1, 0, w['a_wqkv'][j], cs_plain)
            cos_t, sin_t = _rope_tables(seq)
            qk_norm = jnp.stack([p['a_q_norm'][j], p['a_k_norm'][j]])
            o = _attn_a(_prep_a(qkv, cos_t, sin_t, qk_norm, seq), qkv, batch, seq)
            wo = w['a_wo'][j]
        elif m == 1:
            qkv = _norm_matmul(x2, g_attn, mod, row0, seq, 1, 0, w['b_wqkv'][j], cs_q_log2)
            o, _ = _attn_window(qkv[None], slopes16, p['b_sink'][j].astype(F32), batch, seq, window=WINDOW,
                                use_sink=True)
            o = o[0]
            wo = w['b_wo'][j]
        elif m == 2:
            qkv = _norm_matmul(x2, g_attn, mod, row0, seq, 1, 0, w['c_wqkv'][j], cs_q_diff)
            o = _attn_c(qkv, slopes8 * LOG2E, p['c_lambda'][j].astype(F32), p['c_subln'][j][None, :],
                        batch, seq, _lambda_init(i))
            wo = w['c_wo'][j]
        else:
            dils = tuple(dil for _, dil in DILATIONS if dil > 1)
            qkv, *qkv_classes = _norm_matmul(x2, g_attn, mod, row0, seq, 1, 0, w['d_wqkv'][j], cs_q_log2, dils=dils)
            o = _attn_d([qkv[None], *qkv_classes], slopes16, batch, seq)
            wo = w['d_wo'][j]
        x2 = _out_residual(o, wo, x2, mod, row0, seq, 2)
        x2 = _ffn(x2, p['norm_ffn'][i][None, :], mod, row0, seq, w['ffn_w_up'][i], w['ffn_conv_w'][i],
                  w['ffn_conv_b'][i], w['ffn_w_down'][i])
    return _final_norm(x2, p['norm_final'][None, :]).reshape(batch, seq, d)


def kernel(x_prompt, x_sample, c_prompt, c_sample, norm_attn, norm_ffn, w_ada, b_ada, a_wqkv, a_q_norm, a_k_norm, a_wo, b_wqkv, b_sink, b_wo, c_wqkv, c_lambda, c_subln, c_wo, d_wqkv, d_wo, ffn_w_up, ffn_conv_w, ffn_conv_b, ffn_w_down, norm_final):
    p = dict(norm_attn=norm_attn, norm_ffn=norm_ffn, a_q_norm=a_q_norm, a_k_norm=a_k_norm, b_sink=b_sink,
             c_lambda=c_lambda, c_subln=c_subln, norm_final=norm_final,
             a_wqkv=a_wqkv, a_wo=a_wo, b_wqkv=b_wqkv, b_wo=b_wo, c_wqkv=c_wqkv, c_wo=c_wo,
             d_wqkv=d_wqkv, d_wo=d_wo, ffn_w_up=ffn_w_up, ffn_conv_w=ffn_conv_w, ffn_conv_b=ffn_conv_b,
             ffn_w_down=ffn_w_down)
    w = _prepare_weights(p)
    n_prompt, n_sample = c_prompt.shape[0], c_sample.shape[0]
    rows = -(-(n_prompt + n_sample) // 8) * 8
    c_all = jnp.concatenate([c_prompt, c_sample, jnp.zeros((rows - n_prompt - n_sample, D_MODEL), F32)], axis=0)
    mod_all = _modulation(c_all, w_ada, b_ada)
    y_prompt = _trunk(x_prompt, mod_all, 0, p, w)
    y_sample = _trunk(x_sample, mod_all, n_prompt, p, w)
    return (y_prompt, y_sample)
```

```python
import functools
import math

import jax
import jax.numpy as jnp
from jax import lax
from jax.experimental import pallas as pl
from jax.experimental.pallas import tpu as pltpu

F32 = jnp.float32
BF16 = jnp.bfloat16

D_MODEL = 2048
HEAD_DIM = 128
N_HEADS = 16
N_KV_HEADS = 4
GQA_GROUP = 4
DIFF_HEADS = 8
D_FF = 5504
FF_TILE = 512
D_FF_PAD = 5632
N_MOD = 6
DEPTH = 4
GRID_W = 64
ROPE_THETA = 10000.0
EPS = 1e-6
NEG = -1e30
WINDOW = 128
DILATIONS = ((128, 1), (512, 4), (2048, 16))
Q_SCALE = HEAD_DIM ** -0.5
LOG2E = math.log2(math.e)
Q_SCALE_LOG2 = Q_SCALE * LOG2E
HALO = 16
MIB = 1024 * 1024


def _cparams(sem, vmem_mib):
    return pltpu.CompilerParams(dimension_semantics=sem, vmem_limit_bytes=vmem_mib * MIB)


def _dot(a, b):
    return jnp.dot(a, b, preferred_element_type=F32)


def _dot_nt(a, b):
    return lax.dot_general(a, b, (((1,), (1,)), ((), ())), preferred_element_type=F32)


def _norm_mod(x, g, sc, sh):
    y = x * lax.rsqrt(jnp.mean(x * x, axis=-1, keepdims=True) + EPS) * g
    return y * (1.0 + sc) + sh


def _mod_kernel(c_ref, w_ref, b_ref, o_ref):
    c = c_ref[...]
    a = (c / (1.0 + jnp.exp(-c))).astype(BF16)
    o_ref[0] = _dot(a, w_ref[0].astype(BF16)) + b_ref[0]


def _modulation(c_all, w_ada, b_ada, tn=1024):
    nb, d = c_all.shape
    depth, _, n = w_ada.shape
    return pl.pallas_call(
        _mod_kernel,
        grid=(depth, n // tn),
        in_specs=[
            pl.BlockSpec((nb, d), lambda l, j: (0, 0)),
            pl.BlockSpec((1, d, tn), lambda l, j: (l, 0, j)),
            pl.BlockSpec((1, 1, tn), lambda l, j: (l, 0, j)),
        ],
        out_specs=pl.BlockSpec((1, nb, tn), lambda l, j: (l, 0, j)),
        out_shape=jax.ShapeDtypeStruct((depth, nb, n), F32),
        compiler_params=_cparams(("parallel", "parallel"), 40),
        name="adaln_modulation",
    )(c_all, w_ada, b_ada.reshape(depth, 1, n))


def _norm_matmul_kernel(x_ref, g_ref, sc_ref, sh_ref, w_ref, cs_ref, *refs, dils):
    o_ref, class_refs, h_ref = refs[0], refs[1:1 + len(dils)], refs[1 + len(dils)]

    def project(h):
        y = _dot(h, w_ref[...]) * cs_ref[...]
        o_ref[...] = y.astype(o_ref.dtype)
        if dils:
            y_ref = refs[-1]
            nblk, rows, _ = y_ref.shape
            for jb in range(nblk):
                y_ref[jb] = y[:, jb * 128:(jb + 1) * 128]
            for dil, c_ref in zip(dils, class_refs):
                for r in range(dil):
                    c_ref[r] = jnp.concatenate(
                        [y_ref[jb, pl.ds(r, rows // dil, stride=dil), :] for jb in range(nblk)],
                        axis=1).astype(c_ref.dtype)

    @pl.when(pl.program_id(1) == 0)
    def _():
        h = _norm_mod(x_ref[...], g_ref[...], sc_ref[0], sh_ref[0]).astype(BF16)
        h_ref[...] = h
        project(h)

    @pl.when(pl.program_id(1) > 0)
    def _():
        project(h_ref[...])


def _norm_matmul(x2, g, mod, row0, seq, k_sc, k_sh, w, colscale, tm=1024, tn=1024, dils=()):
    t, d = x2.shape
    n = w.shape[1]
    tm = min(tm, seq)
    if dils:
        tn //= 2
    brow = lambda i, j: row0 + (i * tm) // seq
    out_specs = [pl.BlockSpec((tm, tn), lambda i, j: (i, j))]
    out_shape = [jax.ShapeDtypeStruct((t, n), BF16)]
    scratch = [pltpu.VMEM((tm, d), BF16)]
    for dil in dils:
        out_specs.append(pl.BlockSpec((dil, tm // dil, tn), lambda i, j: (0, i, j)))
        out_shape.append(jax.ShapeDtypeStruct((dil, t // dil, n), BF16))
    if dils:
        scratch.append(pltpu.VMEM((tn // 128, tm, 128), F32))
    outs = pl.pallas_call(
        functools.partial(_norm_matmul_kernel, dils=tuple(dils)),
        grid=(t // tm, n // tn),
        in_specs=[
            pl.BlockSpec((tm, d), lambda i, j: (i, 0)),
            pl.BlockSpec((1, d), lambda i, j: (0, 0)),
            pl.BlockSpec((1, 1, d), lambda i, j: (brow(i, j), 0, k_sc)),
            pl.BlockSpec((1, 1, d), lambda i, j: (brow(i, j), 0, k_sh)),
            pl.BlockSpec((d, tn), lambda i, j: (0, j)),
            pl.BlockSpec((1, tn), lambda i, j: (0, j)),
        ],
        out_specs=out_specs,
        out_shape=out_shape,
        scratch_shapes=scratch,
        compiler_params=_cparams(("parallel", "arbitrary"), 52),
        name="norm_qkv_proj",
    )(x2, g, mod, mod, w, colscale)
    return outs if dils else outs[0]


def _out_res_kernel(o_ref, w_ref, x_ref, gate_ref, out_ref):
    out_ref[...] = x_ref[...] + gate_ref[0] * _dot(o_ref[...], w_ref[...])


def _out_residual(o2, w, x2, mod, row0, seq, k_gate, tm=1024, tn=1024):
    t, d = x2.shape
    kdim = o2.shape[1]
    tm = min(tm, seq)
    return pl.pallas_call(
        _out_res_kernel,
        grid=(t // tm, d // tn),
        in_specs=[
            pl.BlockSpec((tm, kdim), lambda i, j: (i, 0)),
            pl.BlockSpec((kdim, tn), lambda i, j: (0, j)),
            pl.BlockSpec((tm, tn), lambda i, j: (i, j)),
            pl.BlockSpec((1, 1, tn), lambda i, j: (row0 + (i * tm) // seq, 0, k_gate * (d // tn) + j)),
        ],
        out_specs=pl.BlockSpec((tm, tn), lambda i, j: (i, j)),
        out_shape=jax.ShapeDtypeStruct((t, d), F32),
        compiler_params=_cparams(("parallel", "parallel"), 48),
        name="out_proj_residual",
    )(o2, w, x2, mod)


def _ffn_kernel(xp_ref, x_ref, xn_ref, g_ref, sc_ref, sh_ref, gate_ref, wa_ref, wb_ref,
                cw_ref, cb_ref, wd_ref, gf_ref, out_ref, h_ref, *, tm, seq, final_norm):
    i = pl.program_id(0)
    f = pl.program_id(1)
    n_ext = tm + 2 * HALO

    def partial_down(h_ext):
        a_ext = _dot(h_ext, wa_ref[...])
        b = _dot(h_ext[HALO:HALO + tm], wb_ref[...])
        a_prev = pltpu.roll(a_ext, 1, 0)[HALO:HALO + tm]
        a_next = pltpu.roll(a_ext, n_ext - 1, 0)[HALO:HALO + tm]
        a_cur = a_ext[HALO:HALO + tm]
        a = cb_ref[...] + (a_prev * cw_ref[0:1, :] + a_cur * cw_ref[1:2, :] + a_next * cw_ref[2:3, :])
        gelu = 0.5 * a * (1.0 + lax.erf(a * (2.0 ** -0.5)))
        return _dot((gelu * b).astype(BF16), wd_ref[...])

    @pl.when(f == 0)
    def _():
        g, sc, sh = g_ref[...], sc_ref[0], sh_ref[0]
        at_start = (i * tm) % seq == 0
        at_end = ((i + 1) * tm) % seq == 0
        hp = jnp.where(at_start, 0.0, _norm_mod(xp_ref[...], g, sc, sh))
        hn = jnp.where(at_end, 0.0, _norm_mod(xn_ref[...], g, sc, sh))
        h_ext = jnp.concatenate([hp, _norm_mod(x_ref[...], g, sc, sh), hn], axis=0).astype(BF16)
        h_ref[...] = h_ext
        out_ref[...] = partial_down(h_ext)

    @pl.when(f > 0)
    def _():
        out_ref[...] += partial_down(h_ref[...])

    @pl.when(f == pl.num_programs(1) - 1)
    def _():
        y = x_ref[...] + gate_ref[0] * out_ref[...]
        if final_norm:
            y = y * lax.rsqrt(jnp.mean(y * y, axis=-1, keepdims=True) + EPS) * gf_ref[...]
        out_ref[...] = y


def _ffn(x2, g, mod, row0, seq, w_up, conv_w, conv_b, w_down, g_final, final_norm, tm=512, tf=FF_TILE):
    t, d = x2.shape
    dff = w_down.shape[0]
    nf = dff // tf
    tm = min(tm, seq)
    hb = tm // HALO
    brow = lambda i, f: row0 + (i * tm) // seq
    return pl.pallas_call(
        functools.partial(_ffn_kernel, tm=tm, seq=seq, final_norm=final_norm),
        grid=(t // tm, nf),
        in_specs=[
            pl.BlockSpec((HALO, d), lambda i, f: (jnp.maximum(i * hb - 1, 0), 0)),
            pl.BlockSpec((tm, d), lambda i, f: (i, 0)),
            pl.BlockSpec((HALO, d), lambda i, f: (jnp.minimum((i + 1) * hb, t // HALO - 1), 0)),
            pl.BlockSpec((1, d), lambda i, f: (0, 0)),
            pl.BlockSpec((1, 1, d), lambda i, f: (brow(i, f), 0, 4)),
            pl.BlockSpec((1, 1, d), lambda i, f: (brow(i, f), 0, 3)),
            pl.BlockSpec((1, 1, d), lambda i, f: (brow(i, f), 0, 5)),
            pl.BlockSpec((d, tf), lambda i, f: (0, f)),
            pl.BlockSpec((d, tf), lambda i, f: (0, nf + f)),
            pl.BlockSpec((3, tf), lambda i, f: (0, f)),
            pl.BlockSpec((1, tf), lambda i, f: (0, f)),
            pl.BlockSpec((tf, d), lambda i, f: (f, 0)),
            pl.BlockSpec((1, d), lambda i, f: (0, 0)),
        ],
        out_specs=pl.BlockSpec((tm, d), lambda i, f: (i, 0)),
        out_shape=jax.ShapeDtypeStruct((t, d), F32),
        scratch_shapes=[pltpu.VMEM((tm + 2 * HALO, d), BF16)],
        compiler_params=_cparams(("parallel", "arbitrary"), 56),
        name="conv_glu_mlp",
    )(x2, x2, x2, g, mod, mod, mod, w_up, w_up, conv_w, conv_b, w_down, g_final)


def _stack_heads(q, width=HEAD_DIM):
    n = q.shape[1] // width
    return jnp.concatenate([q[:, i * width:(i + 1) * width] for i in range(n)], axis=0)


def _unstack_heads(o, n):
    tq = o.shape[0] // n
    return jnp.concatenate([o[i * tq:(i + 1) * tq] for i in range(n)], axis=1)


def _rel_iota(tq, tk):
    return lax.broadcasted_iota(jnp.int32, (tq, tk), 1) - lax.broadcasted_iota(jnp.int32, (tq, tk), 0)


def _attn_specs(seq, tq, q_blk0, k_blk0, v_blk0, kv_width, lag):
    nq = seq // tq
    q_row = lambda b, qi: b * nq + jnp.minimum(qi, nq - 1)
    o_row = lambda b, qi: b * nq + jnp.maximum(qi - lag, 0)
    return [
        pl.BlockSpec((tq, 4 * HEAD_DIM), lambda b, kh, qi: (q_row(b, qi), q_blk0 + kh)),
        pl.BlockSpec((seq, kv_width), lambda b, kh, qi: (b, k_blk0 + kh)),
        pl.BlockSpec((seq, kv_width), lambda b, kh, qi: (b, v_blk0 + kh)),
    ], pl.BlockSpec((tq, 4 * HEAD_DIM), lambda b, kh, qi: (o_row(b, qi), kh))


_SMEM = pl.BlockSpec(memory_space=pltpu.SMEM)
_ATTN_SEM = ("parallel", "parallel", "arbitrary")


def _with_ones(v):
    return jnp.concatenate([v, jnp.ones(v.shape, v.dtype)], axis=1)


def _pipeline_step(qi, score_fn, v_ref, s_ref, macc_ref, finish, *, v_width, tk, mxu_row_sums):
    nck, m_rows, _ = s_ref.shape

    @pl.when(qi == 0)
    def _():
        s_ref[...] = jnp.zeros(s_ref.shape, F32)
        macc_ref[...] = jnp.zeros(macc_ref.shape, F32)

    m_prev = jnp.max(macc_ref[...], axis=-1, keepdims=True)

    def body(c, carry):
        macc, lacc, acc = carry
        x = s_ref[c] - m_prev
        s = score_fn(c)
        s_ref[c] = s
        for j in range(tk // 128):
            macc = jnp.maximum(macc, s[:, j * 128:(j + 1) * 128])
        v = v_ref[pl.ds(pl.multiple_of(c * tk, tk), tk), :]
        if mxu_row_sums:
            acc = acc + _dot(jnp.exp2(x.astype(BF16)), _with_ones(v))
        else:
            p = jnp.exp2(x)
            for j in range(tk // 128):
                lacc = lacc + p[:, j * 128:(j + 1) * 128]
            acc = acc + _dot(p.astype(BF16), v)
        return macc, lacc, acc

    acc_width = 2 * v_width if mxu_row_sums else v_width
    init = (jnp.full((m_rows, 128), NEG, F32), jnp.zeros((m_rows, 128), F32),
            jnp.zeros((m_rows, acc_width), F32))
    macc, lacc, acc = lax.fori_loop(0, nck, body, init, unroll=True)
    macc_ref[...] = macc
    if mxu_row_sums:
        finish(acc[:, :v_width], acc[:, v_width:v_width + 1])
    else:
        finish(acc, jnp.sum(lacc, axis=-1, keepdims=True))


def _pipelined_attention(name, kernel_fn, operands, in_specs, out_spec, batch, seq, tq, tk, m_rows, extra_scratch=()):
    nck = seq // tk
    return pl.pallas_call(
        kernel_fn,
        grid=(batch, N_KV_HEADS, seq // tq + 1),
        in_specs=in_specs,
        out_specs=out_spec,
        out_shape=jax.ShapeDtypeStruct((batch * seq, D_MODEL), BF16),
        scratch_shapes=[pltpu.VMEM((nck, m_rows, tk), F32), pltpu.VMEM((m_rows, 128), F32), *extra_scratch],
        compiler_params=_cparams(_ATTN_SEM, 52),
        name=name,
    )(*operands)


def _prep_a_kernel(qkv_ref, cos_ref, sin_ref, w_ref, o_ref):
    tp = qkv_ref.shape[0]
    cos, sin = cos_ref[...], sin_ref[...]
    lane = lax.broadcasted_iota(jnp.int32, (tp, HEAD_DIM), 1)
    low_half = (lane % (HEAD_DIM // 2)) < (HEAD_DIM // 4)
    for h in range(N_HEADS + N_KV_HEADS):
        cols = slice(h * HEAD_DIM, (h + 1) * HEAD_DIM)
        x = qkv_ref[:, cols].astype(F32)
        is_q = h < N_HEADS
        y = x * lax.rsqrt(jnp.mean(x * x, axis=-1, keepdims=True) + EPS) * (w_ref[0:1, :] if is_q else w_ref[1:2, :])
        partner = jnp.where(low_half, pltpu.roll(y, 3 * HEAD_DIM // 4, 1), pltpu.roll(y, HEAD_DIM // 4, 1))
        r = y * cos + partner * sin
        if is_q:
            r = r * Q_SCALE_LOG2
        o_ref[:, cols] = r.astype(BF16)


def _prep_a(qkv, cos_t, sin_t, qk_norm, seq, tp=512):
    t = qkv.shape[0]
    tp = min(tp, seq)
    width = (N_HEADS + N_KV_HEADS) * HEAD_DIM
    return pl.pallas_call(
        _prep_a_kernel,
        grid=(t // tp,),
        in_specs=[
            pl.BlockSpec((tp, width), lambda i: (i, 0)),
            pl.BlockSpec((tp, HEAD_DIM), lambda i: (i % (seq // tp), 0)),
            pl.BlockSpec((tp, HEAD_DIM), lambda i: (i % (seq // tp), 0)),
            pl.BlockSpec((2, HEAD_DIM), lambda i: (0, 0)),
        ],
        out_specs=pl.BlockSpec((tp, width), lambda i: (i, 0)),
        out_shape=jax.ShapeDtypeStruct((t, width), BF16),
        compiler_params=_cparams(("parallel",), 32),
        name="rope_qk_norm",
    )(qkv, cos_t, sin_t, qk_norm)


def _attn_a_kernel(q_ref, k_ref, v_ref, o_ref, s_ref, macc_ref, *, tk):
    qs = _stack_heads(q_ref[...])

    def score(c):
        return _dot_nt(qs, k_ref[pl.ds(pl.multiple_of(c * tk, tk), tk), :])

    def finish(acc, l):
        o_ref[...] = _unstack_heads(acc / l, GQA_GROUP).astype(o_ref.dtype)

    _pipeline_step(pl.program_id(2), score, v_ref, s_ref, macc_ref, finish, v_width=HEAD_DIM, tk=tk,
                   mxu_row_sums=True)


def _attn_a(qk_rot, qkv, batch, seq, tq=128, tk=1024):
    tk = min(tk, seq)
    in_specs, out_spec = _attn_specs(seq, tq, 0, N_HEADS, N_HEADS + N_KV_HEADS, HEAD_DIM, 1)
    return _pipelined_attention("attn_axial", functools.partial(_attn_a_kernel, tk=tk), (qk_rot, qk_rot, qkv),
                                in_specs, out_spec, batch, seq, tq, tk, GQA_GROUP * tq)


KV_PER_STEP = 2


def _window_kernel(slopes_ref, sink_ref, q_ref, k_ref, v_ref, o_ref, lse_ref, s_ref, macc_ref,
                   *, tq, band, window, nq, seq_c, dil, use_sink, n_steps):
    n = pl.program_id(0)
    pairs = N_KV_HEADS // KV_PER_STEP

    def pair_and_block(step):
        return (step // nq) % pairs, step % nq

    def band_start(qi):
        if band == seq_c:
            return 0
        return pl.multiple_of(jnp.clip(qi * tq - window, 0, seq_c - band), window)

    @pl.when(n == 0)
    def _():
        s_ref[...] = jnp.zeros(s_ref.shape, F32)
        macc_ref[...] = jnp.zeros(macc_ref.shape, F32)

    kp, qi = pair_and_block(jnp.minimum(n, n_steps - 1))
    kpp, qip = pair_and_block(jnp.maximum(n - 1, 0))
    ws, wsp = band_start(qi), band_start(qip)
    rel = jnp.abs(_rel_iota(tq, band) + (ws - qi * tq))
    in_window = rel <= window
    dist = rel.astype(F32)
    for kk in range(KV_PER_STEP):
        kcols = slice(kk * HEAD_DIM, (kk + 1) * HEAD_DIM)
        qcols = slice(kk * GQA_GROUP * HEAD_DIM, (kk + 1) * GQA_GROUP * HEAD_DIM)
        head0 = (kpp * KV_PER_STEP + kk) * GQA_GROUP
        m = jnp.max(macc_ref[kk], axis=-1, keepdims=True)
        if use_sink:
            sink = jnp.concatenate(
                [jnp.full((tq, 1), sink_ref[head0 + g] * LOG2E, F32) for g in range(GQA_GROUP)], axis=0)
            m = jnp.maximum(m, sink)
        x = s_ref[kk] - m
        head0 = (kp * KV_PER_STEP + kk) * GQA_GROUP
        bias = jnp.concatenate(
            [jnp.where(in_window, -(slopes_ref[head0 + g] * (dil * LOG2E)) * dist, NEG)
             for g in range(GQA_GROUP)], axis=0)
        s = _dot_nt(_stack_heads(q_ref[:, qcols]), k_ref[pl.ds(ws, band), kcols]) + bias
        s_ref[kk] = s
        macc = s[:, 0:128]
        for j in range(1, band // 128):
            macc = jnp.maximum(macc, s[:, j * 128:(j + 1) * 128])
        macc_ref[kk] = macc
        ol = _dot(jnp.exp2(x.astype(BF16)), _with_ones(v_ref[pl.ds(wsp, band), kcols]))
        l = ol[:, HEAD_DIM:HEAD_DIM + 1]
        if use_sink:
            l = l + jnp.exp2(sink - m)
        o_ref[:, qcols] = _unstack_heads(ol[:, :HEAD_DIM] / l, GQA_GROUP).astype(o_ref.dtype)
        lse = m + jnp.log2(l)
        lse_ref[:, kcols] = jnp.concatenate(
            [jnp.broadcast_to(lse[g * tq:(g + 1) * tq], (tq, 128 // GQA_GROUP)) for g in range(GQA_GROUP)], axis=1)


def _attn_window(qkv_c, slopes, sink, batch, seq, *, window, use_sink):
    dil, t_c, width = qkv_c.shape
    seq_c = seq // dil
    tq = min(256, seq_c)
    band = min(tq + 2 * window, seq_c)
    nq = seq_c // tq
    assert band % 128 == 0 and seq_c % tq == 0, "bands are reduced in 128-lane blocks"
    pairs = N_KV_HEADS // KV_PER_STEP
    n_steps = batch * dil * pairs * nq
    qw = KV_PER_STEP * GQA_GROUP * HEAD_DIM
    kw = KV_PER_STEP * HEAD_DIM

    def split(step):
        qi = step % nq
        rest = step // nq
        kp = rest % pairs
        rest = rest // pairs
        return rest // dil, rest % dil, kp, qi

    def q_map(n):
        b, r, kp, qi = split(jnp.minimum(n, n_steps - 1))
        return r, b * nq + qi, kp

    def k_map(n):
        b, r, kp, _ = split(jnp.minimum(n, n_steps - 1))
        return r, b, N_HEADS // KV_PER_STEP + kp

    def v_map(n):
        b, r, kp, _ = split(jnp.maximum(n - 1, 0))
        return r, b, (N_HEADS + N_KV_HEADS) // KV_PER_STEP + kp

    def o_map(n):
        b, r, kp, qi = split(jnp.maximum(n - 1, 0))
        return r, b * nq + qi, kp

    return pl.pallas_call(
        functools.partial(_window_kernel, tq=tq, band=band, window=window, nq=nq, seq_c=seq_c, dil=dil,
                          use_sink=use_sink, n_steps=n_steps),
        grid=(n_steps + 1,),
        in_specs=[_SMEM, _SMEM,
                  pl.BlockSpec((None, tq, qw), q_map),
                  pl.BlockSpec((None, seq_c, kw), k_map),
                  pl.BlockSpec((None, seq_c, kw), v_map)],
        out_specs=[pl.BlockSpec((None, tq, qw), o_map), pl.BlockSpec((None, tq, kw), o_map)],
        out_shape=[jax.ShapeDtypeStruct((dil, t_c, D_MODEL), BF16),
                   jax.ShapeDtypeStruct((dil, t_c, N_KV_HEADS * 128), F32)],
        scratch_shapes=[pltpu.VMEM((KV_PER_STEP, GQA_GROUP * tq, band), F32),
                        pltpu.VMEM((KV_PER_STEP, GQA_GROUP * tq, 128), F32)],
        compiler_params=_cparams(("arbitrary",), 40),
        name=f"attn_window_d{dil}",
    )(slopes, sink, qkv_c, qkv_c, qkv_c)


def _attn_c_kernel(slopes_ref, q_ref, k_ref, v_ref, lam_ref, subln_ref, o_ref, s_ref, macc_ref, bias_ref,
                   *, tq, tk, seq, lam_init):
    kh, qi = pl.program_id(1), pl.program_id(2)
    start = jnp.minimum(qi, seq // tq - 1) * tq
    q = q_ref[...]
    h = HEAD_DIM
    qs0 = jnp.concatenate([q[:, 0:h], q[:, 2 * h:3 * h]], axis=0)
    qs1 = jnp.concatenate([q[:, h:2 * h], q[:, 3 * h:4 * h]], axis=0)
    n_blk = seq // 128

    @pl.when(qi == 0)
    def _():
        lane_minus_row = _rel_iota(128, 128)
        for u in range(2 * n_blk - 1):
            dist = jnp.abs(lane_minus_row + (u - (n_blk - 1)) * 128).astype(F32)
            for g in range(2):
                bias_ref[g, u] = slopes_ref[2 * kh + g] * dist

    def score(c):
        off = pl.multiple_of(c * tk, tk)
        k = k_ref[pl.ds(off, tk), :]
        u0 = (off - start) // 128 + (n_blk - 1)
        bias = jnp.concatenate(
            [jnp.concatenate([bias_ref[g, u0 + j - rb] for j in range(tk // 128)], axis=1)
             for g in range(2) for rb in range(tq // 128)], axis=0)
        return jnp.concatenate([_dot_nt(qs0, k[:, :h]) - bias, _dot_nt(qs1, k[:, h:]) - bias], axis=0)

    lp = lam_ref[...]
    lam = (jnp.exp(jnp.sum(lp[0:1] * lp[1:2], axis=-1, keepdims=True))
           - jnp.exp(jnp.sum(lp[2:3] * lp[3:4], axis=-1, keepdims=True)) + lam_init)

    def finish(acc, l):
        a = acc / l
        o = a[:2 * tq] - lam * a[2 * tq:]
        y = o * lax.rsqrt(jnp.mean(o * o, axis=-1, keepdims=True) + EPS) * subln_ref[...] * (1.0 - lam_init)
        o_ref[...] = _unstack_heads(y, 2).astype(o_ref.dtype)

    _pipeline_step(qi, score, v_ref, s_ref, macc_ref, finish, v_width=2 * h, tk=tk, mxu_row_sums=False)


def _attn_c(qkv, slopes_log2, lam_params, subln, batch, seq, lam_init, tq=128, tk=1024):
    assert tq % 128 == 0, "the bias table holds 128 x 128 tiles"
    tk = min(tk, seq)
    in_specs, out_spec = _attn_specs(seq, tq, 0, 8, 12, 2 * HEAD_DIM, 1)
    in_specs = [_SMEM] + in_specs + [
        pl.BlockSpec((4, HEAD_DIM), lambda b, kh, qi: (0, 0)),
        pl.BlockSpec((1, 2 * HEAD_DIM), lambda b, kh, qi: (0, 0)),
    ]
    return _pipelined_attention(
        "attn_diff", functools.partial(_attn_c_kernel, tq=tq, tk=tk, seq=seq, lam_init=lam_init),
        (slopes_log2, qkv, qkv, qkv, lam_params, subln), in_specs, out_spec, batch, seq, tq, tk, 4 * tq,
        extra_scratch=[pltpu.VMEM((2, 2 * (seq // 128) - 1, 128, 128), F32)])


def _combine_kernel(*refs):
    nb = len(DILATIONS)
    o_refs, l_refs, out_ref, scratch = refs[:nb], refs[nb:2 * nb], refs[2 * nb], refs[2 * nb + 1:]
    tc = out_ref.shape[0]
    o_nat, l_nat = [], []
    for b, (_, dil) in enumerate(DILATIONS):
        if dil == 1:
            o_nat.append(o_refs[b][0].astype(F32))
            l_nat.append(l_refs[b][0])
            continue
        so, sl = scratch[2 * b], scratch[2 * b + 1]
        rows = tc // dil
        for r in range(dil):
            o_r, l_r = o_refs[b][r].astype(F32), l_refs[b][r]
            for jb in range(so.shape[0]):
                so[jb, pl.ds(r, rows, stride=dil), :] = o_r[:, jb * 128:(jb + 1) * 128]
            for jb in range(sl.shape[0]):
                sl[jb, pl.ds(r, rows, stride=dil), :] = l_r[:, jb * 128:(jb + 1) * 128]
        o_nat.append(jnp.concatenate([so[jb] for jb in range(so.shape[0])], axis=1))
        l_nat.append(jnp.concatenate([sl[jb] for jb in range(sl.shape[0])], axis=1))
    lanes = 128 // GQA_GROUP
    for h in range(N_HEADS):
        e = [l[:, h * lanes:h * lanes + 1] for l in l_nat]
        m = functools.reduce(jnp.maximum, e)
        w = [jnp.exp2(x - m) for x in e]
        cols = slice(h * HEAD_DIM, (h + 1) * HEAD_DIM)
        mix = sum(wi * o[:, cols] for wi, o in zip(w, o_nat))
        out_ref[:, cols] = (mix / sum(w)).astype(out_ref.dtype)


def _attn_d(qkv_classes, slopes, batch, seq, tc=256):
    t = qkv_classes[0].shape[1]
    dummy_sink = jnp.zeros((N_HEADS,), F32)
    branches = [_attn_window(qkv_c, slopes, dummy_sink, batch, seq, window=(win // 2) // dil, use_sink=False)
                for qkv_c, (win, dil) in zip(qkv_classes, DILATIONS)]
    lw = N_KV_HEADS * 128
    o_specs = [pl.BlockSpec((dil, tc // dil, D_MODEL), lambda i: (0, i, 0)) for _, dil in DILATIONS]
    l_specs = [pl.BlockSpec((dil, tc // dil, lw), lambda i: (0, i, 0)) for _, dil in DILATIONS]
    scratch = []
    for _ in DILATIONS:
        scratch += [pltpu.VMEM((D_MODEL // 128, tc, 128), F32), pltpu.VMEM((lw // 128, tc, 128), F32)]
    return pl.pallas_call(
        _combine_kernel,
        grid=(t // tc,),
        in_specs=o_specs + l_specs,
        out_specs=pl.BlockSpec((tc, D_MODEL), lambda i: (i, 0)),
        out_shape=jax.ShapeDtypeStruct((t, D_MODEL), BF16),
        scratch_shapes=scratch,
        compiler_params=_cparams(("parallel",), 40),
        name="dilated_combine",
    )(*[b[0] for b in branches], *[b[1] for b in branches])


def _alibi_slopes(n):
    return 2.0 ** (-8.0 * jnp.arange(1, n + 1, dtype=F32) / n)


def _lambda_init(layer):
    return 0.8 - 0.6 * math.exp(-0.3 * layer)


def _rope_tables(seq):
    rows = seq // GRID_W
    row = jnp.repeat(jnp.arange(rows, dtype=F32), GRID_W)
    col = jnp.tile(jnp.arange(GRID_W, dtype=F32), rows)
    n_freq = HEAD_DIM // 4
    inv_freq = ROPE_THETA ** (-jnp.arange(n_freq, dtype=F32) / n_freq)
    ang = jnp.stack([row, col], axis=-1)[..., None] * inv_freq
    cos, sin = jnp.cos(ang), jnp.sin(ang)
    cos_t = jnp.concatenate([cos[:, 0], cos[:, 0], cos[:, 1], cos[:, 1]], axis=-1)
    sin_t = jnp.concatenate([-sin[:, 0], sin[:, 0], -sin[:, 1], sin[:, 1]], axis=-1)
    return cos_t, sin_t


def _q_colscale(scale, n_q_cols, n_cols):
    return jnp.concatenate([jnp.full((1, n_q_cols), scale, F32), jnp.ones((1, n_cols - n_q_cols), F32)], axis=1)


def _prepare_weights(p):
    pad = D_FF_PAD - D_FF
    w_up = p['ffn_w_up']
    w_up = jnp.concatenate([jnp.pad(w_up[..., :D_FF], ((0, 0), (0, 0), (0, pad))),
                            jnp.pad(w_up[..., D_FF:], ((0, 0), (0, 0), (0, pad)))], axis=-1).astype(BF16)
    return dict(
        a_wqkv=p['a_wqkv'].astype(BF16), a_wo=p['a_wo'].astype(BF16),
        b_wqkv=p['b_wqkv'].astype(BF16), b_wo=p['b_wo'].astype(BF16),
        c_wqkv=p['c_wqkv'].astype(BF16), c_wo=p['c_wo'].astype(BF16),
        d_wqkv=p['d_wqkv'].astype(BF16), d_wo=p['d_wo'].astype(BF16),
        ffn_w_up=w_up,
        ffn_w_down=jnp.pad(p['ffn_w_down'], ((0, 0), (0, pad), (0, 0))).astype(BF16),
        ffn_conv_w=jnp.pad(p['ffn_conv_w'], ((0, 0), (0, 0), (0, pad))),
        ffn_conv_b=jnp.pad(p['ffn_conv_b'], ((0, 0), (0, pad)))[:, None, :],
    )


def _trunk(x, mod_all, row0, p, w):
    batch, seq, d = x.shape
    x2 = x.reshape(batch * seq, d)
    slopes16, slopes8 = _alibi_slopes(N_HEADS), _alibi_slopes(DIFF_HEADS)
    n_qkv = (N_HEADS + 2 * N_KV_HEADS) * HEAD_DIM
    cs_plain = jnp.ones((1, n_qkv), F32)
    cs_q_log2 = _q_colscale(Q_SCALE_LOG2, N_HEADS * HEAD_DIM, n_qkv)
    cs_q_diff = _q_colscale(Q_SCALE_LOG2, DIFF_HEADS * 2 * HEAD_DIM, 2 * D_MODEL)
    for i in range(DEPTH):
        mod = mod_all[i][:, None, :]
        g_attn = p['norm_attn'][i][None, :]
        m, j = i % 4, i // 4
        if m == 0:
            qkv = _norm_matmul(x2, g_attn, mod, row0, seq, 1, 0, w['a_wqkv'][j], cs_plain)
            cos_t, sin_t = _rope_tables(seq)
            qk_norm = jnp.stack([p['a_q_norm'][j], p['a_k_norm'][j]])
            o = _attn_a(_prep_a(qkv, cos_t, sin_t, qk_norm, seq), qkv, batch, seq)
            wo = w['a_wo'][j]
        elif m == 1:
            qkv = _norm_matmul(x2, g_attn, mod, row0, seq, 1, 0, w['b_wqkv'][j], cs_q_log2)
            o, _ = _attn_window(qkv[None], slopes16, p['b_sink'][j].astype(F32), batch, seq, window=WINDOW,
                                use_sink=True)
            o = o[0]
            wo = w['b_wo'][j]
        elif m == 2:
            qkv = _norm_matmul(x2, g_attn, mod, row0, seq, 1, 0, w['c_wqkv'][j], cs_q_diff)
            o = _attn_c(qkv, slopes8 * LOG2E, p['c_lambda'][j].astype(F32), p['c_subln'][j][None, :],
                        batch, seq, _lambda_init(i))
            wo = w['c_wo'][j]
        else:
            dils = tuple(dil for _, dil in DILATIONS if dil > 1)
            qkv, *qkv_classes = _norm_matmul(x2, g_attn, mod, row0, seq, 1, 0, w['d_wqkv'][j], cs_q_log2, dils=dils)
            o = _attn_d([qkv[None], *qkv_classes], slopes16, batch, seq)
            wo = w['d_wo'][j]
        x2 = _out_residual(o, wo, x2, mod, row0, seq, 2)
        x2 = _ffn(x2, p['norm_ffn'][i][None, :], mod, row0, seq, w['ffn_w_up'][i], w['ffn_conv_w'][i],
                  w['ffn_conv_b'][i], w['ffn_w_down'][i], p['norm_final'][None, :], final_norm=(i == DEPTH - 1))
    return x2.reshape(batch, seq, d)


def kernel(x_prompt, x_sample, c_prompt, c_sample, norm_attn, norm_ffn, w_ada, b_ada, a_wqkv, a_q_norm, a_k_norm, a_wo, b_wqkv, b_sink, b_wo, c_wqkv, c_lambda, c_subln, c_wo, d_wqkv, d_wo, ffn_w_up, ffn_conv_w, ffn_conv_b, ffn_w_down, norm_final):
    p = dict(norm_attn=norm_attn, norm_ffn=norm_ffn, a_q_norm=a_q_norm, a_k_norm=a_k_norm, b_sink=b_sink,
             c_lambda=c_lambda, c_subln=c_subln, norm_final=norm_final,
             a_wqkv=a_wqkv, a_wo=a_wo, b_wqkv=b_wqkv, b_wo=b_wo, c_wqkv=c_wqkv, c_wo=c_wo,
             d_wqkv=d_wqkv, d_wo=d_wo, ffn_w_up=ffn_w_up, ffn_conv_w=ffn_conv_w, ffn_conv_b=ffn_conv_b,
             ffn_w_down=ffn_w_down)
    w = _prepare_weights(p)
    n_prompt, n_sample = c_prompt.shape[0], c_sample.shape[0]
    rows = -(-(n_prompt + n_sample) // 8) * 8
    c_all = jnp.concatenate([c_prompt, c_sample, jnp.zeros((rows - n_prompt - n_sample, D_MODEL), F32)], axis=0)
    mod_all = _modulation(c_all, w_ada, b_ada)
    y_prompt = _trunk(x_prompt, mod_all, 0, p, w)
    y_sample = _trunk(x_sample, mod_all, n_prompt, p, w)
    return (y_prompt, y_sample)
```

```python
import functools
import math

import jax
import jax.numpy as jnp
from jax import lax
from jax.experimental import pallas as pl
from jax.experimental.pallas import tpu as pltpu

F32 = jnp.float32
BF16 = jnp.bfloat16

D_MODEL = 2048
HEAD_DIM = 128
N_HEADS = 16
N_KV_HEADS = 4
GQA_GROUP = 4
DIFF_HEADS = 8
D_FF = 5504
FF_TILE = 512
D_FF_PAD = 5632
N_MOD = 6
DEPTH = 4
GRID_W = 64
ROPE_THETA = 10000.0
EPS = 1e-6
NEG = -1e30
WINDOW = 128
DILATIONS = ((128, 1), (512, 4), (2048, 16))
Q_SCALE = HEAD_DIM ** -0.5
LOG2E = math.log2(math.e)
Q_SCALE_LOG2 = Q_SCALE * LOG2E
HALO = 16
MIB = 1024 * 1024


def _cparams(sem, vmem_mib):
    return pltpu.CompilerParams(dimension_semantics=sem, vmem_limit_bytes=vmem_mib * MIB)


def _dot(a, b):
    return jnp.dot(a, b, preferred_element_type=F32)


def _dot_nt(a, b):
    return lax.dot_general(a, b, (((1,), (1,)), ((), ())), preferred_element_type=F32)


def _norm_mod(x, g, sc, sh):
    y = x * lax.rsqrt(jnp.mean(x * x, axis=-1, keepdims=True) + EPS) * g
    return y * (1.0 + sc) + sh


def _mod_kernel(c_ref, w_ref, b_ref, o_ref):
    c = c_ref[...]
    a = (c / (1.0 + jnp.exp(-c))).astype(BF16)
    o_ref[0] = _dot(a, w_ref[0].astype(BF16)) + b_ref[0]


def _modulation(c_all, w_ada, b_ada, tn=1024):
    nb, d = c_all.shape
    depth, _, n = w_ada.shape
    return pl.pallas_call(
        _mod_kernel,
        grid=(depth, n // tn),
        in_specs=[
            pl.BlockSpec((nb, d), lambda l, j: (0, 0)),
            pl.BlockSpec((1, d, tn), lambda l, j: (l, 0, j)),
            pl.BlockSpec((1, 1, tn), lambda l, j: (l, 0, j)),
        ],
        out_specs=pl.BlockSpec((1, nb, tn), lambda l, j: (l, 0, j)),
        out_shape=jax.ShapeDtypeStruct((depth, nb, n), F32),
        compiler_params=_cparams(("parallel", "parallel"), 40),
        name="adaln_modulation",
    )(c_all, w_ada, b_ada.reshape(depth, 1, n))


def _norm_matmul_kernel(x_ref, g_ref, sc_ref, sh_ref, w_ref, cs_ref, *refs, dils):
    o_ref, class_refs, h_ref = refs[0], refs[1:1 + len(dils)], refs[1 + len(dils)]

    def project(h):
        y = _dot(h, w_ref[...]) * cs_ref[...]
        o_ref[...] = y.astype(o_ref.dtype)
        if dils:
            y_ref = refs[-1]
            nblk, rows, _ = y_ref.shape
            for jb in range(nblk):
                y_ref[jb] = y[:, jb * 128:(jb + 1) * 128]
            for dil, c_ref in zip(dils, class_refs):
                for r in range(dil):
                    c_ref[r] = jnp.concatenate(
                        [y_ref[jb, pl.ds(r, rows // dil, stride=dil), :] for jb in range(nblk)],
                        axis=1).astype(c_ref.dtype)

    @pl.when(pl.program_id(1) == 0)
    def _():
        h = _norm_mod(x_ref[...], g_ref[...], sc_ref[0], sh_ref[0]).astype(BF16)
        h_ref[...] = h
        project(h)

    @pl.when(pl.program_id(1) > 0)
    def _():
        project(h_ref[...])


def _norm_matmul(x2, g, mod, row0, seq, k_sc, k_sh, w, colscale, tm=1024, tn=1024, dils=()):
    t, d = x2.shape
    n = w.shape[1]
    tm = min(tm, seq)
    if dils:
        tn //= 2
    brow = lambda i, j: row0 + (i * tm) // seq
    out_specs = [pl.BlockSpec((tm, tn), lambda i, j: (i, j))]
    out_shape = [jax.ShapeDtypeStruct((t, n), BF16)]
    scratch = [pltpu.VMEM((tm, d), BF16)]
    for dil in dils:
        out_specs.append(pl.BlockSpec((dil, tm // dil, tn), lambda i, j: (0, i, j)))
        out_shape.append(jax.ShapeDtypeStruct((dil, t // dil, n), BF16))
    if dils:
        scratch.append(pltpu.VMEM((tn // 128, tm, 128), F32))
    outs = pl.pallas_call(
        functools.partial(_norm_matmul_kernel, dils=tuple(dils)),
        grid=(t // tm, n // tn),
        in_specs=[
            pl.BlockSpec((tm, d), lambda i, j: (i, 0)),
            pl.BlockSpec((1, d), lambda i, j: (0, 0)),
            pl.BlockSpec((1, 1, d), lambda i, j: (brow(i, j), 0, k_sc)),
            pl.BlockSpec((1, 1, d), lambda i, j: (brow(i, j), 0, k_sh)),
            pl.BlockSpec((d, tn), lambda i, j: (0, j)),
            pl.BlockSpec((1, tn), lambda i, j: (0, j)),
        ],
        out_specs=out_specs,
        out_shape=out_shape,
        scratch_shapes=scratch,
        compiler_params=_cparams(("parallel", "arbitrary"), 52),
        name="norm_qkv_proj",
    )(x2, g, mod, mod, w, colscale)
    return outs if dils else outs[0]


def _out_res_kernel(o_ref, w_ref, x_ref, gate_ref, out_ref):
    out_ref[...] = x_ref[...] + gate_ref[0] * _dot(o_ref[...], w_ref[...])


def _out_residual(o2, w, x2, mod, row0, seq, k_gate, tm=1024, tn=1024):
    t, d = x2.shape
    kdim = o2.shape[1]
    tm = min(tm, seq)
    return pl.pallas_call(
        _out_res_kernel,
        grid=(t // tm, d // tn),
        in_specs=[
            pl.BlockSpec((tm, kdim), lambda i, j: (i, 0)),
            pl.BlockSpec((kdim, tn), lambda i, j: (0, j)),
            pl.BlockSpec((tm, tn), lambda i, j: (i, j)),
            pl.BlockSpec((1, 1, tn), lambda i, j: (row0 + (i * tm) // seq, 0, k_gate * (d // tn) + j)),
        ],
        out_specs=pl.BlockSpec((tm, tn), lambda i, j: (i, j)),
        out_shape=jax.ShapeDtypeStruct((t, d), F32),
        compiler_params=_cparams(("parallel", "parallel"), 48),
        name="out_proj_residual",
    )(o2, w, x2, mod)


def _ffn_kernel(xp_ref, x_ref, xn_ref, g_ref, sc_ref, sh_ref, gate_ref, wa_ref, wb_ref,
                cw_ref, cb_ref, wd_ref, gf_ref, out_ref, h_ref, *, tm, seq, final_norm):
    i = pl.program_id(0)
    f = pl.program_id(1)
    n_ext = tm + 2 * HALO

    def partial_down(h_ext):
        a_ext = _dot(h_ext, wa_ref[...])
        b = _dot(h_ext[HALO:HALO + tm], wb_ref[...])
        a_prev = pltpu.roll(a_ext, 1, 0)[HALO:HALO + tm]
        a_next = pltpu.roll(a_ext, n_ext - 1, 0)[HALO:HALO + tm]
        a_cur = a_ext[HALO:HALO + tm]
        a = cb_ref[...] + (a_prev * cw_ref[0:1, :] + a_cur * cw_ref[1:2, :] + a_next * cw_ref[2:3, :])
        gelu = 0.5 * a * (1.0 + lax.erf(a * (2.0 ** -0.5)))
        return _dot((gelu * b).astype(BF16), wd_ref[...])

    @pl.when(f == 0)
    def _():
        g, sc, sh = g_ref[...], sc_ref[0], sh_ref[0]
        at_start = (i * tm) % seq == 0
        at_end = ((i + 1) * tm) % seq == 0
        hp = jnp.where(at_start, 0.0, _norm_mod(xp_ref[...], g, sc, sh))
        hn = jnp.where(at_end, 0.0, _norm_mod(xn_ref[...], g, sc, sh))
        h_ext = jnp.concatenate([hp, _norm_mod(x_ref[...], g, sc, sh), hn], axis=0).astype(BF16)
        h_ref[...] = h_ext
        out_ref[...] = partial_down(h_ext)

    @pl.when(f > 0)
    def _():
        out_ref[...] += partial_down(h_ref[...])

    @pl.when(f == pl.num_programs(1) - 1)
    def _():
        y = x_ref[...] + gate_ref[0] * out_ref[...]
        if final_norm:
            y = y * lax.rsqrt(jnp.mean(y * y, axis=-1, keepdims=True) + EPS) * gf_ref[...]
        out_ref[...] = y


def _ffn(x2, g, mod, row0, seq, w_up, conv_w, conv_b, w_down, g_final, final_norm, tm=512, tf=FF_TILE):
    t, d = x2.shape
    dff = w_down.shape[0]
    nf = dff // tf
    tm = min(tm, seq)
    hb = tm // HALO
    brow = lambda i, f: row0 + (i * tm) // seq
    return pl.pallas_call(
        functools.partial(_ffn_kernel, tm=tm, seq=seq, final_norm=final_norm),
        grid=(t // tm, nf),
        in_specs=[
            pl.BlockSpec((HALO, d), lambda i, f: (jnp.maximum(i * hb - 1, 0), 0)),
            pl.BlockSpec((tm, d), lambda i, f: (i, 0)),
            pl.BlockSpec((HALO, d), lambda i, f: (jnp.minimum((i + 1) * hb, t // HALO - 1), 0)),
            pl.BlockSpec((1, d), lambda i, f: (0, 0)),
            pl.BlockSpec((1, 1, d), lambda i, f: (brow(i, f), 0, 4)),
            pl.BlockSpec((1, 1, d), lambda i, f: (brow(i, f), 0, 3)),
            pl.BlockSpec((1, 1, d), lambda i, f: (brow(i, f), 0, 5)),
            pl.BlockSpec((d, tf), lambda i, f: (0, f)),
            pl.BlockSpec((d, tf), lambda i, f: (0, nf + f)),
            pl.BlockSpec((3, tf), lambda i, f: (0, f)),
            pl.BlockSpec((1, tf), lambda i, f: (0, f)),
            pl.BlockSpec((tf, d), lambda i, f: (f, 0)),
            pl.BlockSpec((1, d), lambda i, f: (0, 0)),
        ],
        out_specs=pl.BlockSpec((tm, d), lambda i, f: (i, 0)),
        out_shape=jax.ShapeDtypeStruct((t, d), F32),
        scratch_shapes=[pltpu.VMEM((tm + 2 * HALO, d), BF16)],
        compiler_params=_cparams(("parallel", "arbitrary"), 56),
        name="conv_glu_mlp",
    )(x2, x2, x2, g, mod, mod, mod, w_up, w_up, conv_w, conv_b, w_down, g_final)


def _stack_heads(q, width=HEAD_DIM):
    n = q.shape[1] // width
    return jnp.concatenate([q[:, i * width:(i + 1) * width] for i in range(n)], axis=0)


def _unstack_heads(o, n):
    tq = o.shape[0] // n
    return jnp.concatenate([o[i * tq:(i + 1) * tq] for i in range(n)], axis=1)


def _rel_iota(tq, tk):
    return lax.broadcasted_iota(jnp.int32, (tq, tk), 1) - lax.broadcasted_iota(jnp.int32, (tq, tk), 0)


def _attn_specs(seq, tq, k_col0, v_col0, kv_width, kv_per_step):
    nq = seq // tq
    qw, kw = kv_per_step * 4 * HEAD_DIM, kv_per_step * kv_width
    q_row = lambda b, qi: b * nq + jnp.minimum(qi, nq - 1)
    o_row = lambda b, qi: b * nq + jnp.maximum(qi - 1, 0)
    return [
        pl.BlockSpec((tq, qw), lambda b, kh, qi: (q_row(b, qi), kh)),
        pl.BlockSpec((seq, kw), lambda b, kh, qi: (b, k_col0 // kw + kh)),
        pl.BlockSpec((seq, kw), lambda b, kh, qi: (b, v_col0 // kw + kh)),
    ], pl.BlockSpec((tq, qw), lambda b, kh, qi: (o_row(b, qi), kh))


_SMEM = pl.BlockSpec(memory_space=pltpu.SMEM)
_ATTN_SEM = ("parallel", "parallel", "arbitrary")


def _with_ones(v):
    return jnp.concatenate([v, jnp.ones(v.shape, v.dtype)], axis=1)


def _pipeline_init(qi, s_ref, macc_ref):
    @pl.when(qi == 0)
    def _():
        s_ref[...] = jnp.zeros(s_ref.shape, F32)
        macc_ref[...] = jnp.zeros(macc_ref.shape, F32)


def _pipeline_step(score_fn, value_fn, s_ref, macc_ref, finish, *, v_width, tk, mxu_row_sums):
    nck, m_rows, _ = s_ref.shape
    m_prev = jnp.max(macc_ref[...], axis=-1, keepdims=True)

    def body(c, carry):
        macc, lacc, acc = carry
        x = s_ref[c] - m_prev
        s = score_fn(c)
        s_ref[c] = s
        for j in range(tk // 128):
            macc = jnp.maximum(macc, s[:, j * 128:(j + 1) * 128])
        v = value_fn(c)
        if mxu_row_sums:
            acc = acc + _dot(jnp.exp2(x.astype(BF16)), _with_ones(v))
        else:
            p = jnp.exp2(x)
            for j in range(tk // 128):
                lacc = lacc + p[:, j * 128:(j + 1) * 128]
            acc = acc + _dot(p.astype(BF16), v)
        return macc, lacc, acc

    acc_width = 2 * v_width if mxu_row_sums else v_width
    init = (jnp.full((m_rows, 128), NEG, F32), jnp.zeros((m_rows, 128), F32),
            jnp.zeros((m_rows, acc_width), F32))
    macc, lacc, acc = lax.fori_loop(0, nck, body, init, unroll=True)
    macc_ref[...] = macc
    if mxu_row_sums:
        finish(acc[:, :v_width], acc[:, v_width:v_width + 1])
    else:
        finish(acc, jnp.sum(lacc, axis=-1, keepdims=True))


def _pipelined_attention(name, kernel_fn, operands, in_specs, out_spec, batch, seq, tq, tk, m_rows, kv_per_step,
                         extra_scratch=()):
    nck = seq // tk
    return pl.pallas_call(
        kernel_fn,
        grid=(batch, N_KV_HEADS // kv_per_step, seq // tq + 1),
        in_specs=in_specs,
        out_specs=out_spec,
        out_shape=jax.ShapeDtypeStruct((batch * seq, D_MODEL), BF16),
        scratch_shapes=[pltpu.VMEM((kv_per_step, nck, m_rows, tk), F32),
                        pltpu.VMEM((kv_per_step, m_rows, 128), F32), *extra_scratch],
        compiler_params=_cparams(_ATTN_SEM, 52),
        name=name,
    )(*operands)


def _prep_a_kernel(qkv_ref, cos_ref, sin_ref, w_ref, o_ref):
    tp = qkv_ref.shape[0]
    cos, sin = cos_ref[...], sin_ref[...]
    lane = lax.broadcasted_iota(jnp.int32, (tp, HEAD_DIM), 1)
    low_half = (lane % (HEAD_DIM // 2)) < (HEAD_DIM // 4)
    for h in range(N_HEADS + N_KV_HEADS):
        cols = slice(h * HEAD_DIM, (h + 1) * HEAD_DIM)
        x = qkv_ref[:, cols].astype(F32)
        is_q = h < N_HEADS
        y = x * lax.rsqrt(jnp.mean(x * x, axis=-1, keepdims=True) + EPS) * (w_ref[0:1, :] if is_q else w_ref[1:2, :])
        partner = jnp.where(low_half, pltpu.roll(y, 3 * HEAD_DIM // 4, 1), pltpu.roll(y, HEAD_DIM // 4, 1))
        r = y * cos + partner * sin
        if is_q:
            r = r * Q_SCALE_LOG2
        o_ref[:, cols] = r.astype(BF16)


def _prep_a(qkv, cos_t, sin_t, qk_norm, seq, tp=512):
    t = qkv.shape[0]
    tp = min(tp, seq)
    width = (N_HEADS + N_KV_HEADS) * HEAD_DIM
    return pl.pallas_call(
        _prep_a_kernel,
        grid=(t // tp,),
        in_specs=[
            pl.BlockSpec((tp, width), lambda i: (i, 0)),
            pl.BlockSpec((tp, HEAD_DIM), lambda i: (i % (seq // tp), 0)),
            pl.BlockSpec((tp, HEAD_DIM), lambda i: (i % (seq // tp), 0)),
            pl.BlockSpec((2, HEAD_DIM), lambda i: (0, 0)),
        ],
        out_specs=pl.BlockSpec((tp, width), lambda i: (i, 0)),
        out_shape=jax.ShapeDtypeStruct((t, width), BF16),
        compiler_params=_cparams(("parallel",), 32),
        name="rope_qk_norm",
    )(qkv, cos_t, sin_t, qk_norm)


def _attn_a_kernel(q_ref, k_ref, v_ref, o_ref, s_ref, macc_ref, *, tk):
    _pipeline_init(pl.program_id(2), s_ref, macc_ref)
    for kk in range(s_ref.shape[0]):
        kcols = slice(kk * HEAD_DIM, (kk + 1) * HEAD_DIM)
        qcols = slice(kk * GQA_GROUP * HEAD_DIM, (kk + 1) * GQA_GROUP * HEAD_DIM)
        qs = _stack_heads(q_ref[:, qcols])

        def score(c, qs=qs, kcols=kcols):
            return _dot_nt(qs, k_ref[pl.ds(pl.multiple_of(c * tk, tk), tk), kcols])

        def value(c, kcols=kcols):
            return v_ref[pl.ds(pl.multiple_of(c * tk, tk), tk), kcols]

        def finish(acc, l, qcols=qcols):
            o_ref[:, qcols] = _unstack_heads(acc / l, GQA_GROUP).astype(o_ref.dtype)

        _pipeline_step(score, value, s_ref.at[kk], macc_ref.at[kk], finish, v_width=HEAD_DIM, tk=tk,
                       mxu_row_sums=True)


def _attn_a(qk_rot, qkv, batch, seq, tq=128, tk=1024, kv_per_step=2):
    tk = min(tk, seq)
    in_specs, out_spec = _attn_specs(seq, tq, N_HEADS * HEAD_DIM, (N_HEADS + N_KV_HEADS) * HEAD_DIM, HEAD_DIM,
                                     kv_per_step)
    return _pipelined_attention("attn_axial", functools.partial(_attn_a_kernel, tk=tk), (qk_rot, qk_rot, qkv),
                                in_specs, out_spec, batch, seq, tq, tk, GQA_GROUP * tq, kv_per_step)


KV_PER_STEP = 2


def _window_kernel(slopes_ref, sink_ref, q_ref, k_ref, v_ref, o_ref, lse_ref, s_ref, macc_ref,
                   *, tq, band, window, nq, seq_c, dil, use_sink, n_steps):
    n = pl.program_id(0)
    pairs = N_KV_HEADS // KV_PER_STEP

    def pair_and_block(step):
        return (step // nq) % pairs, step % nq

    def band_start(qi):
        if band == seq_c:
            return 0
        return pl.multiple_of(jnp.clip(qi * tq - window, 0, seq_c - band), window)

    @pl.when(n == 0)
    def _():
        s_ref[...] = jnp.zeros(s_ref.shape, F32)
        macc_ref[...] = jnp.zeros(macc_ref.shape, F32)

    kp, qi = pair_and_block(jnp.minimum(n, n_steps - 1))
    kpp, qip = pair_and_block(jnp.maximum(n - 1, 0))
    ws, wsp = band_start(qi), band_start(qip)
    rel = jnp.abs(_rel_iota(tq, band) + (ws - qi * tq))
    in_window = rel <= window
    dist = rel.astype(F32)
    for kk in range(KV_PER_STEP):
        kcols = slice(kk * HEAD_DIM, (kk + 1) * HEAD_DIM)
        qcols = slice(kk * GQA_GROUP * HEAD_DIM, (kk + 1) * GQA_GROUP * HEAD_DIM)
        head0 = (kpp * KV_PER_STEP + kk) * GQA_GROUP
        m = jnp.max(macc_ref[kk], axis=-1, keepdims=True)
        if use_sink:
            sink = jnp.concatenate(
                [jnp.full((tq, 1), sink_ref[head0 + g] * LOG2E, F32) for g in range(GQA_GROUP)], axis=0)
            m = jnp.maximum(m, sink)
        x = s_ref[kk] - m
        head0 = (kp * KV_PER_STEP + kk) * GQA_GROUP
        bias = jnp.concatenate(
            [jnp.where(in_window, -(slopes_ref[head0 + g] * (dil * LOG2E)) * dist, NEG)
             for g in range(GQA_GROUP)], axis=0)
        s = _dot_nt(_stack_heads(q_ref[:, qcols]), k_ref[pl.ds(ws, band), kcols]) + bias
        s_ref[kk] = s
        macc = s[:, 0:128]
        for j in range(1, band // 128):
            macc = jnp.maximum(macc, s[:, j * 128:(j + 1) * 128])
        macc_ref[kk] = macc
        ol = _dot(jnp.exp2(x.astype(BF16)), _with_ones(v_ref[pl.ds(wsp, band), kcols]))
        l = ol[:, HEAD_DIM:HEAD_DIM + 1]
        if use_sink:
            l = l + jnp.exp2(sink - m)
        o_ref[:, qcols] = _unstack_heads(ol[:, :HEAD_DIM] / l, GQA_GROUP).astype(o_ref.dtype)
        lse = m + jnp.log2(l)
        lse_ref[:, kcols] = jnp.concatenate(
            [jnp.broadcast_to(lse[g * tq:(g + 1) * tq], (tq, 128 // GQA_GROUP)) for g in range(GQA_GROUP)], axis=1)


def _attn_window(qkv_c, slopes, sink, batch, seq, *, window, use_sink):
    dil, t_c, width = qkv_c.shape
    seq_c = seq // dil
    tq = min(256, seq_c)
    band = min(tq + 2 * window, seq_c)
    nq = seq_c // tq
    assert band % 128 == 0 and seq_c % tq == 0, "bands are reduced in 128-lane blocks"
    pairs = N_KV_HEADS // KV_PER_STEP
    n_steps = batch * dil * pairs * nq
    qw = KV_PER_STEP * GQA_GROUP * HEAD_DIM
    kw = KV_PER_STEP * HEAD_DIM

    def split(step):
        qi = step % nq
        rest = step // nq
        kp = rest % pairs
        rest = rest // pairs
        return rest // dil, rest % dil, kp, qi

    def q_map(n):
        b, r, kp, qi = split(jnp.minimum(n, n_steps - 1))
        return r, b * nq + qi, kp

    def k_map(n):
        b, r, kp, _ = split(jnp.minimum(n, n_steps - 1))
        return r, b, N_HEADS // KV_PER_STEP + kp

    def v_map(n):
        b, r, kp, _ = split(jnp.maximum(n - 1, 0))
        return r, b, (N_HEADS + N_KV_HEADS) // KV_PER_STEP + kp

    def o_map(n):
        b, r, kp, qi = split(jnp.maximum(n - 1, 0))
        return r, b * nq + qi, kp

    return pl.pallas_call(
        functools.partial(_window_kernel, tq=tq, band=band, window=window, nq=nq, seq_c=seq_c, dil=dil,
                          use_sink=use_sink, n_steps=n_steps),
        grid=(n_steps + 1,),
        in_specs=[_SMEM, _SMEM,
                  pl.BlockSpec((None, tq, qw), q_map),
                  pl.BlockSpec((None, seq_c, kw), k_map),
                  pl.BlockSpec((None, seq_c, kw), v_map)],
        out_specs=[pl.BlockSpec((None, tq, qw), o_map), pl.BlockSpec((None, tq, kw), o_map)],
        out_shape=[jax.ShapeDtypeStruct((dil, t_c, D_MODEL), BF16),
                   jax.ShapeDtypeStruct((dil, t_c, N_KV_HEADS * 128), F32)],
        scratch_shapes=[pltpu.VMEM((KV_PER_STEP, GQA_GROUP * tq, band), F32),
                        pltpu.VMEM((KV_PER_STEP, GQA_GROUP * tq, 128), F32)],
        compiler_params=_cparams(("arbitrary",), 40),
        name=f"attn_window_d{dil}",
    )(slopes, sink, qkv_c, qkv_c, qkv_c)


def _attn_c_kernel(slopes_ref, q_ref, k_ref, v_ref, lam_ref, subln_ref, o_ref, s_ref, macc_ref, bias_ref,
                   *, tq, tk, seq, lam_init):
    kh, qi = pl.program_id(1), pl.program_id(2)
    start = jnp.minimum(qi, seq // tq - 1) * tq
    q = q_ref[...]
    h = HEAD_DIM
    qs0 = jnp.concatenate([q[:, 0:h], q[:, 2 * h:3 * h]], axis=0)
    qs1 = jnp.concatenate([q[:, h:2 * h], q[:, 3 * h:4 * h]], axis=0)
    n_blk = seq // 128

    @pl.when(qi == 0)
    def _():
        lane_minus_row = _rel_iota(128, 128)
        for u in range(2 * n_blk - 1):
            dist = jnp.abs(lane_minus_row + (u - (n_blk - 1)) * 128).astype(F32)
            for g in range(2):
                bias_ref[g, u] = slopes_ref[2 * kh + g] * dist

    def score(c):
        off = pl.multiple_of(c * tk, tk)
        k = k_ref[pl.ds(off, tk), :]
        u0 = (off - start) // 128 + (n_blk - 1)
        bias = jnp.concatenate(
            [jnp.concatenate([bias_ref[g, u0 + j - rb] for j in range(tk // 128)], axis=1)
             for g in range(2) for rb in range(tq // 128)], axis=0)
        return jnp.concatenate([_dot_nt(qs0, k[:, :h]) - bias, _dot_nt(qs1, k[:, h:]) - bias], axis=0)

    lp = lam_ref[...]
    lam = (jnp.exp(jnp.sum(lp[0:1] * lp[1:2], axis=-1, keepdims=True))
           - jnp.exp(jnp.sum(lp[2:3] * lp[3:4], axis=-1, keepdims=True)) + lam_init)

    def finish(acc, l):
        a = acc / l
        o = a[:2 * tq] - lam * a[2 * tq:]
        y = o * lax.rsqrt(jnp.mean(o * o, axis=-1, keepdims=True) + EPS) * subln_ref[...] * (1.0 - lam_init)
        o_ref[...] = _unstack_heads(y, 2).astype(o_ref.dtype)

    def value(c):
        return v_ref[pl.ds(pl.multiple_of(c * tk, tk), tk), :]

    _pipeline_init(qi, s_ref, macc_ref)
    _pipeline_step(score, value, s_ref.at[0], macc_ref.at[0], finish, v_width=2 * h, tk=tk, mxu_row_sums=False)


def _attn_c(qkv, slopes_log2, lam_params, subln, batch, seq, lam_init, tq=128, tk=1024):
    assert tq % 128 == 0, "the bias table holds 128 x 128 tiles"
    tk = min(tk, seq)
    in_specs, out_spec = _attn_specs(seq, tq, D_MODEL, D_MODEL + N_KV_HEADS * 2 * HEAD_DIM, 2 * HEAD_DIM, 1)
    in_specs = [_SMEM] + in_specs + [
        pl.BlockSpec((4, HEAD_DIM), lambda b, kh, qi: (0, 0)),
        pl.BlockSpec((1, 2 * HEAD_DIM), lambda b, kh, qi: (0, 0)),
    ]
    return _pipelined_attention(
        "attn_diff", functools.partial(_attn_c_kernel, tq=tq, tk=tk, seq=seq, lam_init=lam_init),
        (slopes_log2, qkv, qkv, qkv, lam_params, subln), in_specs, out_spec, batch, seq, tq, tk, 4 * tq, 1,
        extra_scratch=[pltpu.VMEM((2, 2 * (seq // 128) - 1, 128, 128), F32)])


def _combine_kernel(*refs):
    nb = len(DILATIONS)
    o_refs, l_refs, out_ref, scratch = refs[:nb], refs[nb:2 * nb], refs[2 * nb], refs[2 * nb + 1:]
    tc = out_ref.shape[0]
    o_nat, l_nat = [], []
    for b, (_, dil) in enumerate(DILATIONS):
        if dil == 1:
            o_nat.append(o_refs[b][0].astype(F32))
            l_nat.append(l_refs[b][0])
            continue
        so, sl = scratch[2 * b], scratch[2 * b + 1]
        rows = tc // dil
        for r in range(dil):
            o_r, l_r = o_refs[b][r].astype(F32), l_refs[b][r]
            for jb in range(so.shape[0]):
                so[jb, pl.ds(r, rows, stride=dil), :] = o_r[:, jb * 128:(jb + 1) * 128]
            for jb in range(sl.shape[0]):
                sl[jb, pl.ds(r, rows, stride=dil), :] = l_r[:, jb * 128:(jb + 1) * 128]
        o_nat.append(jnp.concatenate([so[jb] for jb in range(so.shape[0])], axis=1))
        l_nat.append(jnp.concatenate([sl[jb] for jb in range(sl.shape[0])], axis=1))
    lanes = 128 // GQA_GROUP
    for h in range(N_HEADS):
        e = [l[:, h * lanes:h * lanes + 1] for l in l_nat]
        m = functools.reduce(jnp.maximum, e)
        w = [jnp.exp2(x - m) for x in e]
        cols = slice(h * HEAD_DIM, (h + 1) * HEAD_DIM)
        mix = sum(wi * o[:, cols] for wi, o in zip(w, o_nat))
        out_ref[:, cols] = (mix / sum(w)).astype(out_ref.dtype)


def _attn_d(qkv_classes, slopes, batch, seq, tc=256):
    t = qkv_classes[0].shape[1]
    dummy_sink = jnp.zeros((N_HEADS,), F32)
    branches = [_attn_window(qkv_c, slopes, dummy_sink, batch, seq, window=(win // 2) // dil, use_sink=False)
                for qkv_c, (win, dil) in zip(qkv_classes, DILATIONS)]
    lw = N_KV_HEADS * 128
    o_specs = [pl.BlockSpec((dil, tc // dil, D_MODEL), lambda i: (0, i, 0)) for _, dil in DILATIONS]
    l_specs = [pl.BlockSpec((dil, tc // dil, lw), lambda i: (0, i, 0)) for _, dil in DILATIONS]
    scratch = []
    for _ in DILATIONS:
        scratch += [pltpu.VMEM((D_MODEL // 128, tc, 128), F32), pltpu.VMEM((lw // 128, tc, 128), F32)]
    return pl.pallas_call(
        _combine_kernel,
        grid=(t // tc,),
        in_specs=o_specs + l_specs,
        out_specs=pl.BlockSpec((tc, D_MODEL), lambda i: (i, 0)),
        out_shape=jax.ShapeDtypeStruct((t, D_MODEL), BF16),
        scratch_shapes=scratch,
        compiler_params=_cparams(("parallel",), 40),
        name="dilated_combine",
    )(*[b[0] for b in branches], *[b[1] for b in branches])


def _alibi_slopes(n):
    return 2.0 ** (-8.0 * jnp.arange(1, n + 1, dtype=F32) / n)


def _lambda_init(layer):
    return 0.8 - 0.6 * math.exp(-0.3 * layer)


def _rope_tables(seq):
    rows = seq // GRID_W
    row = jnp.repeat(jnp.arange(rows, dtype=F32), GRID_W)
    col = jnp.tile(jnp.arange(GRID_W, dtype=F32), rows)
    n_freq = HEAD_DIM // 4
    inv_freq = ROPE_THETA ** (-jnp.arange(n_freq, dtype=F32) / n_freq)
    ang = jnp.stack([row, col], axis=-1)[..., None] * inv_freq
    cos, sin = jnp.cos(ang), jnp.sin(ang)
    cos_t = jnp.concatenate([cos[:, 0], cos[:, 0], cos[:, 1], cos[:, 1]], axis=-1)
    sin_t = jnp.concatenate([-sin[:, 0], sin[:, 0], -sin[:, 1], sin[:, 1]], axis=-1)
    return cos_t, sin_t


def _q_colscale(scale, n_q_cols, n_cols):
    return jnp.concatenate([jnp.full((1, n_q_cols), scale, F32), jnp.ones((1, n_cols - n_q_cols), F32)], axis=1)


def _prepare_weights(p):
    pad = D_FF_PAD - D_FF
    w_up = p['ffn_w_up']
    w_up = jnp.concatenate([jnp.pad(w_up[..., :D_FF], ((0, 0), (0, 0), (0, pad))),
                            jnp.pad(w_up[..., D_FF:], ((0, 0), (0, 0), (0, pad)))], axis=-1).astype(BF16)
    return dict(
        a_wqkv=p['a_wqkv'].astype(BF16), a_wo=p['a_wo'].astype(BF16),
        b_wqkv=p['b_wqkv'].astype(BF16), b_wo=p['b_wo'].astype(BF16),
        c_wqkv=p['c_wqkv'].astype(BF16), c_wo=p['c_wo'].astype(BF16),
        d_wqkv=p['d_wqkv'].astype(BF16), d_wo=p['d_wo'].astype(BF16),
        ffn_w_up=w_up,
        ffn_w_down=jnp.pad(p['ffn_w_down'], ((0, 0), (0, pad), (0, 0))).astype(BF16),
        ffn_conv_w=jnp.pad(p['ffn_conv_w'], ((0, 0), (0, 0), (0, pad))),
        ffn_conv_b=jnp.pad(p['ffn_conv_b'], ((0, 0), (0, pad)))[:, None, :],
    )


def _trunk(x, mod_all, row0, p, w):
    batch, seq, d = x.shape
    x2 = x.reshape(batch * seq, d)
    slopes16, slopes8 = _alibi_slopes(N_HEADS), _alibi_slopes(DIFF_HEADS)
    n_qkv = (N_HEADS + 2 * N_KV_HEADS) * HEAD_DIM
    cs_plain = jnp.ones((1, n_qkv), F32)
    cs_q_log2 = _q_colscale(Q_SCALE_LOG2, N_HEADS * HEAD_DIM, n_qkv)
    cs_q_diff = _q_colscale(Q_SCALE_LOG2, DIFF_HEADS * 2 * HEAD_DIM, 2 * D_MODEL)
    for i in range(DEPTH):
        mod = mod_all[i][:, None, :]
        g_attn = p['norm_attn'][i][None, :]
        m, j = i % 4, i // 4
        if m == 0:
            qkv = _norm_matmul(x2, g_attn, mod, row0, seq, 1, 0, w['a_wqkv'][j], cs_plain)
            cos_t, sin_t = _rope_tables(seq)
            qk_norm = jnp.stack([p['a_q_norm'][j], p['a_k_norm'][j]])
            o = _attn_a(_prep_a(qkv, cos_t, sin_t, qk_norm, seq), qkv, batch, seq)
            wo = w['a_wo'][j]
        elif m == 1:
            qkv = _norm_matmul(x2, g_attn, mod, row0, seq, 1, 0, w['b_wqkv'][j], cs_q_log2)
            o, _ = _attn_window(qkv[None], slopes16, p['b_sink'][j].astype(F32), batch, seq, window=WINDOW,
                                use_sink=True)
            o = o[0]
            wo = w['b_wo'][j]
        elif m == 2:
            qkv = _norm_matmul(x2, g_attn, mod, row0, seq, 1, 0, w['c_wqkv'][j], cs_q_diff)
            o = _attn_c(qkv, slopes8 * LOG2E, p['c_lambda'][j].astype(F32), p['c_subln'][j][None, :],
                        batch, seq, _lambda_init(i))
            wo = w['c_wo'][j]
        else:
            dils = tuple(dil for _, dil in DILATIONS if dil > 1)
            qkv, *qkv_classes = _norm_matmul(x2, g_attn, mod, row0, seq, 1, 0, w['d_wqkv'][j], cs_q_log2, dils=dils)
            o = _attn_d([qkv[None], *qkv_classes], slopes16, batch, seq)
            wo = w['d_wo'][j]
        x2 = _out_residual(o, wo, x2, mod, row0, seq, 2)
        x2 = _ffn(x2, p['norm_ffn'][i][None, :], mod, row0, seq, w['ffn_w_up'][i], w['ffn_conv_w'][i],
                  w['ffn_conv_b'][i], w['ffn_w_down'][i], p['norm_final'][None, :], final_norm=(i == DEPTH - 1))
    return x2.reshape(batch, seq, d)


def kernel(x_prompt, x_sample, c_prompt, c_sample, norm_attn, norm_ffn, w_ada, b_ada, a_wqkv, a_q_norm, a_k_norm, a_wo, b_wqkv, b_sink, b_wo, c_wqkv, c_lambda, c_subln, c_wo, d_wqkv, d_wo, ffn_w_up, ffn_conv_w, ffn_conv_b, ffn_w_down, norm_final):
    p = dict(norm_attn=norm_attn, norm_ffn=norm_ffn, a_q_norm=a_q_norm, a_k_norm=a_k_norm, b_sink=b_sink,
             c_lambda=c_lambda, c_subln=c_subln, norm_final=norm_final,
             a_wqkv=a_wqkv, a_wo=a_wo, b_wqkv=b_wqkv, b_wo=b_wo, c_wqkv=c_wqkv, c_wo=c_wo,
             d_wqkv=d_wqkv, d_wo=d_wo, ffn_w_up=ffn_w_up, ffn_conv_w=ffn_conv_w, ffn_conv_b=ffn_conv_b,
             ffn_w_down=ffn_w_down)
    w = _prepare_weights(p)
    n_prompt, n_sample = c_prompt.shape[0], c_sample.shape[0]
    rows = -(-(n_prompt + n_sample) // 8) * 8
    c_all = jnp.concatenate([c_prompt, c_sample, jnp.zeros((rows - n_prompt - n_sample, D_MODEL), F32)], axis=0)
    mod_all = _modulation(c_all, w_ada, b_ada)
    y_prompt = _trunk(x_prompt, mod_all, 0, p, w)
    y_sample = _trunk(x_sample, mod_all, n_prompt, p, w)
    return (y_prompt, y_sample)
```

```python
import functools
import math

import jax
import jax.numpy as jnp
from jax import lax
from jax.experimental import pallas as pl
from jax.experimental.pallas import tpu as pltpu

F32 = jnp.float32
BF16 = jnp.bfloat16

D_MODEL = 2048
HEAD_DIM = 128
N_HEADS = 16
N_KV_HEADS = 4
GQA_GROUP = 4
DIFF_HEADS = 8
D_FF = 5504
FF_TILE = 512
D_FF_PAD = 5632
N_MOD = 6
DEPTH = 4
GRID_W = 64
ROPE_THETA = 10000.0
EPS = 1e-6
NEG = -1e30
WINDOW = 128
DILATIONS = ((128, 1), (512, 4), (2048, 16))
Q_SCALE = HEAD_DIM ** -0.5
LOG2E = math.log2(math.e)
Q_SCALE_LOG2 = Q_SCALE * LOG2E
HALO = 16
MIB = 1024 * 1024


def _cparams(sem, vmem_mib):
    return pltpu.CompilerParams(dimension_semantics=sem, vmem_limit_bytes=vmem_mib * MIB)


def _dot(a, b):
    return jnp.dot(a, b, preferred_element_type=F32)


def _dot_nt(a, b):
    return lax.dot_general(a, b, (((1,), (1,)), ((), ())), preferred_element_type=F32)


def _norm_mod(x, g, sc, sh):
    y = x * lax.rsqrt(jnp.mean(x * x, axis=-1, keepdims=True) + EPS) * g
    return y * (1.0 + sc) + sh


def _mod_kernel(c_ref, w_ref, b_ref, o_ref):
    c = c_ref[...]
    a = (c / (1.0 + jnp.exp(-c))).astype(BF16)
    o_ref[0] = _dot(a, w_ref[0].astype(BF16)) + b_ref[0]


def _modulation(c_all, w_ada, b_ada, tn=1024):
    nb, d = c_all.shape
    depth, _, n = w_ada.shape
    return pl.pallas_call(
        _mod_kernel,
        grid=(depth, n // tn),
        in_specs=[
            pl.BlockSpec((nb, d), lambda l, j: (0, 0)),
            pl.BlockSpec((1, d, tn), lambda l, j: (l, 0, j)),
            pl.BlockSpec((1, 1, tn), lambda l, j: (l, 0, j)),
        ],
        out_specs=pl.BlockSpec((1, nb, tn), lambda l, j: (l, 0, j)),
        out_shape=jax.ShapeDtypeStruct((depth, nb, n), F32),
        compiler_params=_cparams(("parallel", "parallel"), 40),
        name="adaln_modulation",
    )(c_all, w_ada, b_ada.reshape(depth, 1, n))


def _norm_matmul_kernel(x_ref, g_ref, sc_ref, sh_ref, w_ref, cs_ref, *refs, dils):
    o_ref, class_refs, h_ref = refs[0], refs[1:1 + len(dils)], refs[1 + len(dils)]

    def project(h):
        y = _dot(h, w_ref[...]) * cs_ref[...]
        o_ref[...] = y.astype(o_ref.dtype)
        if dils:
            y_ref = refs[-1]
            nblk, rows, _ = y_ref.shape
            for jb in range(nblk):
                y_ref[jb] = y[:, jb * 128:(jb + 1) * 128]
            for dil, c_ref in zip(dils, class_refs):
                for r in range(dil):
                    c_ref[r] = jnp.concatenate(
                        [y_ref[jb, pl.ds(r, rows // dil, stride=dil), :] for jb in range(nblk)],
                        axis=1).astype(c_ref.dtype)

    @pl.when(pl.program_id(1) == 0)
    def _():
        h = _norm_mod(x_ref[...], g_ref[...], sc_ref[0], sh_ref[0]).astype(BF16)
        h_ref[...] = h
        project(h)

    @pl.when(pl.program_id(1) > 0)
    def _():
        project(h_ref[...])


def _norm_matmul(x2, g, mod, row0, seq, k_sc, k_sh, w, colscale, tm=1024, tn=1024, dils=()):
    t, d = x2.shape
    n = w.shape[1]
    tm = min(tm, seq)
    if dils:
        tn //= 2
    brow = lambda i, j: row0 + (i * tm) // seq
    out_specs = [pl.BlockSpec((tm, tn), lambda i, j: (i, j))]
    out_shape = [jax.ShapeDtypeStruct((t, n), BF16)]
    scratch = [pltpu.VMEM((tm, d), BF16)]
    for dil in dils:
        out_specs.append(pl.BlockSpec((dil, tm // dil, tn), lambda i, j: (0, i, j)))
        out_shape.append(jax.ShapeDtypeStruct((dil, t // dil, n), BF16))
    if dils:
        scratch.append(pltpu.VMEM((tn // 128, tm, 128), F32))
    outs = pl.pallas_call(
        functools.partial(_norm_matmul_kernel, dils=tuple(dils)),
        grid=(t // tm, n // tn),
        in_specs=[
            pl.BlockSpec((tm, d), lambda i, j: (i, 0)),
            pl.BlockSpec((1, d), lambda i, j: (0, 0)),
            pl.BlockSpec((1, 1, d), lambda i, j: (brow(i, j), 0, k_sc)),
            pl.BlockSpec((1, 1, d), lambda i, j: (brow(i, j), 0, k_sh)),
            pl.BlockSpec((d, tn), lambda i, j: (0, j)),
            pl.BlockSpec((1, tn), lambda i, j: (0, j)),
        ],
        out_specs=out_specs,
        out_shape=out_shape,
        scratch_shapes=scratch,
        compiler_params=_cparams(("parallel", "arbitrary"), 52),
        name="norm_qkv_proj",
    )(x2, g, mod, mod, w, colscale)
    return outs if dils else outs[0]


def _out_res_kernel(o_ref, w_ref, x_ref, gate_ref, out_ref):
    out_ref[...] = x_ref[...] + gate_ref[0] * _dot(o_ref[...], w_ref[...])


def _out_residual(o2, w, x2, mod, row0, seq, k_gate, tm=1024, tn=1024):
    t, d = x2.shape
    kdim = o2.shape[1]
    tm = min(tm, seq)
    return pl.pallas_call(
        _out_res_kernel,
        grid=(t // tm, d // tn),
        in_specs=[
            pl.BlockSpec((tm, kdim), lambda i, j: (i, 0)),
            pl.BlockSpec((kdim, tn), lambda i, j: (0, j)),
            pl.BlockSpec((tm, tn), lambda i, j: (i, j)),
            pl.BlockSpec((1, 1, tn), lambda i, j: (row0 + (i * tm) // seq, 0, k_gate * (d // tn) + j)),
        ],
        out_specs=pl.BlockSpec((tm, tn), lambda i, j: (i, j)),
        out_shape=jax.ShapeDtypeStruct((t, d), F32),
        compiler_params=_cparams(("parallel", "parallel"), 48),
        name="out_proj_residual",
    )(o2, w, x2, mod)


def _ffn_kernel(xp_ref, x_ref, xn_ref, g_ref, sc_ref, sh_ref, gate_ref, wa_ref, wb_ref,
                cw_ref, cb_ref, wd_ref, gf_ref, out_ref, h_ref, *, tm, seq, final_norm):
    i = pl.program_id(0)
    f = pl.program_id(1)
    n_ext = tm + 2 * HALO

    def partial_down(h_ext):
        a_ext = _dot(h_ext, wa_ref[...])
        b = _dot(h_ext[HALO:HALO + tm], wb_ref[...])
        a_prev = pltpu.roll(a_ext, 1, 0)[HALO:HALO + tm]
        a_next = pltpu.roll(a_ext, n_ext - 1, 0)[HALO:HALO + tm]
        a_cur = a_ext[HALO:HALO + tm]
        a = cb_ref[...] + (a_prev * cw_ref[0:1, :] + a_cur * cw_ref[1:2, :] + a_next * cw_ref[2:3, :])
        gelu = 0.5 * a * (1.0 + lax.erf(a * (2.0 ** -0.5)))
        return _dot((gelu * b).astype(BF16), wd_ref[...])

    @pl.when(f == 0)
    def _():
        g, sc, sh = g_ref[...], sc_ref[0], sh_ref[0]
        at_start = (i * tm) % seq == 0
        at_end = ((i + 1) * tm) % seq == 0
        hp = jnp.where(at_start, 0.0, _norm_mod(xp_ref[...], g, sc, sh))
        hn = jnp.where(at_end, 0.0, _norm_mod(xn_ref[...], g, sc, sh))
        h_ext = jnp.concatenate([hp, _norm_mod(x_ref[...], g, sc, sh), hn], axis=0).astype(BF16)
        h_ref[...] = h_ext
        out_ref[...] = partial_down(h_ext)

    @pl.when(f > 0)
    def _():
        out_ref[...] += partial_down(h_ref[...])

    @pl.when(f == pl.num_programs(1) - 1)
    def _():
        y = x_ref[...] + gate_ref[0] * out_ref[...]
        if final_norm:
            y = y * lax.rsqrt(jnp.mean(y * y, axis=-1, keepdims=True) + EPS) * gf_ref[...]
        out_ref[...] = y


def _ffn(x2, g, mod, row0, seq, w_up, conv_w, conv_b, w_down, g_final, final_norm, tm=512, tf=FF_TILE):
    t, d = x2.shape
    dff = w_down.shape[0]
    nf = dff // tf
    tm = min(tm, seq)
    hb = tm // HALO
    brow = lambda i, f: row0 + (i * tm) // seq
    return pl.pallas_call(
        functools.partial(_ffn_kernel, tm=tm, seq=seq, final_norm=final_norm),
        grid=(t // tm, nf),
        in_specs=[
            pl.BlockSpec((HALO, d), lambda i, f: (jnp.maximum(i * hb - 1, 0), 0)),
            pl.BlockSpec((tm, d), lambda i, f: (i, 0)),
            pl.BlockSpec((HALO, d), lambda i, f: (jnp.minimum((i + 1) * hb, t // HALO - 1), 0)),
            pl.BlockSpec((1, d), lambda i, f: (0, 0)),
            pl.BlockSpec((1, 1, d), lambda i, f: (brow(i, f), 0, 4)),
            pl.BlockSpec((1, 1, d), lambda i, f: (brow(i, f), 0, 3)),
            pl.BlockSpec((1, 1, d), lambda i, f: (brow(i, f), 0, 5)),
            pl.BlockSpec((d, tf), lambda i, f: (0, f)),
            pl.BlockSpec((d, tf), lambda i, f: (0, nf + f)),
            pl.BlockSpec((3, tf), lambda i, f: (0, f)),
            pl.BlockSpec((1, tf), lambda i, f: (0, f)),
            pl.BlockSpec((tf, d), lambda i, f: (f, 0)),
            pl.BlockSpec((1, d), lambda i, f: (0, 0)),
        ],
        out_specs=pl.BlockSpec((tm, d), lambda i, f: (i, 0)),
        out_shape=jax.ShapeDtypeStruct((t, d), F32),
        scratch_shapes=[pltpu.VMEM((tm + 2 * HALO, d), BF16)],
        compiler_params=_cparams(("parallel", "arbitrary"), 56),
        name="conv_glu_mlp",
    )(x2, x2, x2, g, mod, mod, mod, w_up, w_up, conv_w, conv_b, w_down, g_final)


def _stack_heads(q, width=HEAD_DIM):
    n = q.shape[1] // width
    return jnp.concatenate([q[:, i * width:(i + 1) * width] for i in range(n)], axis=0)


def _unstack_heads(o, n):
    tq = o.shape[0] // n
    return jnp.concatenate([o[i * tq:(i + 1) * tq] for i in range(n)], axis=1)


def _rel_iota(tq, tk):
    return lax.broadcasted_iota(jnp.int32, (tq, tk), 1) - lax.broadcasted_iota(jnp.int32, (tq, tk), 0)


def _attn_specs(seq, tq, k_col0, v_col0, kv_width, kv_per_step):
    nq = seq // tq
    qw, kw = kv_per_step * 4 * HEAD_DIM, kv_per_step * kv_width
    q_row = lambda b, qi: b * nq + jnp.minimum(qi, nq - 1)
    o_row = lambda b, qi: b * nq + jnp.maximum(qi - 1, 0)
    return [
        pl.BlockSpec((tq, qw), lambda b, kh, qi: (q_row(b, qi), kh)),
        pl.BlockSpec((seq, kw), lambda b, kh, qi: (b, k_col0 // kw + kh)),
        pl.BlockSpec((seq, kw), lambda b, kh, qi: (b, v_col0 // kw + kh)),
    ], pl.BlockSpec((tq, qw), lambda b, kh, qi: (o_row(b, qi), kh))


_SMEM = pl.BlockSpec(memory_space=pltpu.SMEM)
_ATTN_SEM = ("parallel", "parallel", "arbitrary")


def _with_ones(v):
    return jnp.concatenate([v, jnp.ones(v.shape, v.dtype)], axis=1)


def _pipeline_init(qi, s_ref, macc_ref):
    @pl.when(qi == 0)
    def _():
        s_ref[...] = jnp.zeros(s_ref.shape, F32)
        macc_ref[...] = jnp.zeros(macc_ref.shape, F32)


def _pipeline_step(score_fn, value_fn, s_ref, macc_ref, finish, *, v_width, tk, mxu_row_sums):
    nck, m_rows, _ = s_ref.shape
    m_prev = jnp.max(macc_ref[...], axis=-1, keepdims=True)

    def body(c, carry):
        macc, lacc, acc = carry
        x = s_ref[c] - m_prev
        s = score_fn(c)
        s_ref[c] = s
        for j in range(tk // 128):
            macc = jnp.maximum(macc, s[:, j * 128:(j + 1) * 128])
        v = value_fn(c)
        if mxu_row_sums:
            acc = acc + _dot(jnp.exp2(x.astype(BF16)), _with_ones(v))
        else:
            p = jnp.exp2(x)
            for j in range(tk // 128):
                lacc = lacc + p[:, j * 128:(j + 1) * 128]
            acc = acc + _dot(p.astype(BF16), v)
        return macc, lacc, acc

    acc_width = 2 * v_width if mxu_row_sums else v_width
    init = (jnp.full((m_rows, 128), NEG, F32), jnp.zeros((m_rows, 128), F32),
            jnp.zeros((m_rows, acc_width), F32))
    macc, lacc, acc = lax.fori_loop(0, nck, body, init, unroll=True)
    macc_ref[...] = macc
    if mxu_row_sums:
        finish(acc[:, :v_width], acc[:, v_width:v_width + 1])
    else:
        finish(acc, jnp.sum(lacc, axis=-1, keepdims=True))


def _pipelined_attention(name, kernel_fn, operands, in_specs, out_spec, batch, seq, tq, tk, m_rows, kv_per_step,
                         extra_scratch=()):
    nck = seq // tk
    return pl.pallas_call(
        kernel_fn,
        grid=(batch, N_KV_HEADS // kv_per_step, seq // tq + 1),
        in_specs=in_specs,
        out_specs=out_spec,
        out_shape=jax.ShapeDtypeStruct((batch * seq, D_MODEL), BF16),
        scratch_shapes=[pltpu.VMEM((kv_per_step, nck, m_rows, tk), F32),
                        pltpu.VMEM((kv_per_step, m_rows, 128), F32), *extra_scratch],
        compiler_params=_cparams(_ATTN_SEM, 52),
        name=name,
    )(*operands)


def _prep_a_kernel(qkv_ref, cos_ref, sin_ref, w_ref, o_ref):
    tp = qkv_ref.shape[0]
    cos, sin = cos_ref[...], sin_ref[...]
    lane = lax.broadcasted_iota(jnp.int32, (tp, HEAD_DIM), 1)
    low_half = (lane % (HEAD_DIM // 2)) < (HEAD_DIM // 4)
    for h in range(N_HEADS + N_KV_HEADS):
        cols = slice(h * HEAD_DIM, (h + 1) * HEAD_DIM)
        x = qkv_ref[:, cols].astype(F32)
        is_q = h < N_HEADS
        y = x * lax.rsqrt(jnp.mean(x * x, axis=-1, keepdims=True) + EPS) * (w_ref[0:1, :] if is_q else w_ref[1:2, :])
        partner = jnp.where(low_half, pltpu.roll(y, 3 * HEAD_DIM // 4, 1), pltpu.roll(y, HEAD_DIM // 4, 1))
        r = y * cos + partner * sin
        if is_q:
            r = r * Q_SCALE_LOG2
        o_ref[:, cols] = r.astype(BF16)


def _prep_a(qkv, cos_t, sin_t, qk_norm, seq, tp=512):
    t = qkv.shape[0]
    tp = min(tp, seq)
    width = (N_HEADS + N_KV_HEADS) * HEAD_DIM
    return pl.pallas_call(
        _prep_a_kernel,
        grid=(t // tp,),
        in_specs=[
            pl.BlockSpec((tp, width), lambda i: (i, 0)),
            pl.BlockSpec((tp, HEAD_DIM), lambda i: (i % (seq // tp), 0)),
            pl.BlockSpec((tp, HEAD_DIM), lambda i: (i % (seq // tp), 0)),
            pl.BlockSpec((2, HEAD_DIM), lambda i: (0, 0)),
        ],
        out_specs=pl.BlockSpec((tp, width), lambda i: (i, 0)),
        out_shape=jax.ShapeDtypeStruct((t, width), BF16),
        compiler_params=_cparams(("parallel",), 32),
        name="rope_qk_norm",
    )(qkv, cos_t, sin_t, qk_norm)


def _attn_a_kernel(q_ref, k_ref, v_ref, o_ref, s_ref, macc_ref, *, tk):
    _pipeline_init(pl.program_id(2), s_ref, macc_ref)
    for kk in range(s_ref.shape[0]):
        kcols = slice(kk * HEAD_DIM, (kk + 1) * HEAD_DIM)
        qcols = slice(kk * GQA_GROUP * HEAD_DIM, (kk + 1) * GQA_GROUP * HEAD_DIM)
        qs = _stack_heads(q_ref[:, qcols])

        def score(c, qs=qs, kcols=kcols):
            return _dot_nt(qs, k_ref[pl.ds(pl.multiple_of(c * tk, tk), tk), kcols])

        def value(c, kcols=kcols):
            return v_ref[pl.ds(pl.multiple_of(c * tk, tk), tk), kcols]

        def finish(acc, l, qcols=qcols):
            o_ref[:, qcols] = _unstack_heads(acc / l, GQA_GROUP).astype(o_ref.dtype)

        _pipeline_step(score, value, s_ref.at[kk], macc_ref.at[kk], finish, v_width=HEAD_DIM, tk=tk,
                       mxu_row_sums=True)


def _attn_a(qk_rot, qkv, batch, seq, tq=128, tk=1024, kv_per_step=2):
    tk = min(tk, seq)
    in_specs, out_spec = _attn_specs(seq, tq, N_HEADS * HEAD_DIM, (N_HEADS + N_KV_HEADS) * HEAD_DIM, HEAD_DIM,
                                     kv_per_step)
    return _pipelined_attention("attn_axial", functools.partial(_attn_a_kernel, tk=tk), (qk_rot, qk_rot, qkv),
                                in_specs, out_spec, batch, seq, tq, tk, GQA_GROUP * tq, kv_per_step)


KV_PER_STEP = 2


def _window_kernel(slopes_ref, sink_ref, q_ref, k_ref, v_ref, o_ref, lse_ref, s_ref, macc_ref,
                   *, tq, band, window, nq, seq_c, dil, use_sink, n_steps):
    n = pl.program_id(0)
    pairs = N_KV_HEADS // KV_PER_STEP

    def pair_and_block(step):
        return (step // nq) % pairs, step % nq

    def band_start(qi):
        if band == seq_c:
            return 0
        return pl.multiple_of(jnp.clip(qi * tq - window, 0, seq_c - band), window)

    @pl.when(n == 0)
    def _():
        s_ref[...] = jnp.zeros(s_ref.shape, F32)
        macc_ref[...] = jnp.zeros(macc_ref.shape, F32)

    kp, qi = pair_and_block(jnp.minimum(n, n_steps - 1))
    kpp, qip = pair_and_block(jnp.maximum(n - 1, 0))
    ws, wsp = band_start(qi), band_start(qip)
    rel = jnp.abs(_rel_iota(tq, band) + (ws - qi * tq))
    in_window = rel <= window
    dist = rel.astype(F32)
    for kk in range(KV_PER_STEP):
        kcols = slice(kk * HEAD_DIM, (kk + 1) * HEAD_DIM)
        qcols = slice(kk * GQA_GROUP * HEAD_DIM, (kk + 1) * GQA_GROUP * HEAD_DIM)
        head0 = (kpp * KV_PER_STEP + kk) * GQA_GROUP
        m = jnp.max(macc_ref[kk], axis=-1, keepdims=True)
        if use_sink:
            sink = jnp.concatenate(
                [jnp.full((tq, 1), sink_ref[head0 + g] * LOG2E, F32) for g in range(GQA_GROUP)], axis=0)
            m = jnp.maximum(m, sink)
        x = s_ref[kk] - m
        head0 = (kp * KV_PER_STEP + kk) * GQA_GROUP
        bias = jnp.concatenate(
            [jnp.where(in_window, -(slopes_ref[head0 + g] * (dil * LOG2E)) * dist, NEG)
             for g in range(GQA_GROUP)], axis=0)
        s = _dot_nt(_stack_heads(q_ref[:, qcols]), k_ref[pl.ds(ws, band), kcols]) + bias
        s_ref[kk] = s
        macc = s[:, 0:128]
        for j in range(1, band // 128):
            macc = jnp.maximum(macc, s[:, j * 128:(j + 1) * 128])
        macc_ref[kk] = macc
        ol = _dot(jnp.exp2(x.astype(BF16)), _with_ones(v_ref[pl.ds(wsp, band), kcols]))
        l = ol[:, HEAD_DIM:HEAD_DIM + 1]
        if use_sink:
            l = l + jnp.exp2(sink - m)
        o_ref[:, qcols] = _unstack_heads(ol[:, :HEAD_DIM] / l, GQA_GROUP).astype(o_ref.dtype)
        lse = m + jnp.log2(l)
        lse_ref[:, kcols] = jnp.concatenate(
            [jnp.broadcast_to(lse[g * tq:(g + 1) * tq], (tq, 128 // GQA_GROUP)) for g in range(GQA_GROUP)], axis=1)


def _attn_window(qkv_c, slopes, sink, batch, seq, *, window, use_sink):
    dil, t_c, width = qkv_c.shape
    seq_c = seq // dil
    tq = min(256, seq_c)
    band = min(tq + 2 * window, seq_c)
    nq = seq_c // tq
    assert band % 128 == 0 and seq_c % tq == 0, "bands are reduced in 128-lane blocks"
    pairs = N_KV_HEADS // KV_PER_STEP
    n_steps = batch * dil * pairs * nq
    qw = KV_PER_STEP * GQA_GROUP * HEAD_DIM
    kw = KV_PER_STEP * HEAD_DIM

    def split(step):
        qi = step % nq
        rest = step // nq
        kp = rest % pairs
        rest = rest // pairs
        return rest // dil, rest % dil, kp, qi

    def q_map(n):
        b, r, kp, qi = split(jnp.minimum(n, n_steps - 1))
        return r, b * nq + qi, kp

    def k_map(n):
        b, r, kp, _ = split(jnp.minimum(n, n_steps - 1))
        return r, b, N_HEADS // KV_PER_STEP + kp

    def v_map(n):
        b, r, kp, _ = split(jnp.maximum(n - 1, 0))
        return r, b, (N_HEADS + N_KV_HEADS) // KV_PER_STEP + kp

    def o_map(n):
        b, r, kp, qi = split(jnp.maximum(n - 1, 0))
        return r, b * nq + qi, kp

    return pl.pallas_call(
        functools.partial(_window_kernel, tq=tq, band=band, window=window, nq=nq, seq_c=seq_c, dil=dil,
                          use_sink=use_sink, n_steps=n_steps),
        grid=(n_steps + 1,),
        in_specs=[_SMEM, _SMEM,
                  pl.BlockSpec((None, tq, qw), q_map),
                  pl.BlockSpec((None, seq_c, kw), k_map),
                  pl.BlockSpec((None, seq_c, kw), v_map)],
        out_specs=[pl.BlockSpec((None, tq, qw), o_map), pl.BlockSpec((None, tq, kw), o_map)],
        out_shape=[jax.ShapeDtypeStruct((dil, t_c, D_MODEL), BF16),
                   jax.ShapeDtypeStruct((dil, t_c, N_KV_HEADS * 128), F32)],
        scratch_shapes=[pltpu.VMEM((KV_PER_STEP, GQA_GROUP * tq, band), F32),
                        pltpu.VMEM((KV_PER_STEP, GQA_GROUP * tq, 128), F32)],
        compiler_params=_cparams(("arbitrary",), 40),
        name=f"attn_window_d{dil}",
    )(slopes, sink, qkv_c, qkv_c, qkv_c)


def _attn_c_kernel(slopes_ref, q_ref, k_ref, v_ref, lam_ref, subln_ref, o_ref, s_ref, macc_ref, dist_ref,
                   *, tq, tk, seq, lam_init):
    kp, qi = pl.program_id(1), pl.program_id(2)
    start = jnp.minimum(qi, seq // tq - 1) * tq
    h = HEAD_DIM
    n_blk = seq // 128
    kv_per_step = s_ref.shape[0]

    @pl.when(qi == 0)
    def _():
        lane_minus_row = _rel_iota(128, 128)
        for u in range(2 * n_blk - 1):
            dist_ref[u] = jnp.abs(lane_minus_row + (u - (n_blk - 1)) * 128).astype(F32)

    _pipeline_init(qi, s_ref, macc_ref)
    lp = lam_ref[...]
    lam = (jnp.exp(jnp.sum(lp[0:1] * lp[1:2], axis=-1, keepdims=True))
           - jnp.exp(jnp.sum(lp[2:3] * lp[3:4], axis=-1, keepdims=True)) + lam_init)

    for kk in range(kv_per_step):
        q = q_ref[:, kk * 4 * h:(kk + 1) * 4 * h]
        qs0 = jnp.concatenate([q[:, 0:h], q[:, 2 * h:3 * h]], axis=0)
        qs1 = jnp.concatenate([q[:, h:2 * h], q[:, 3 * h:4 * h]], axis=0)
        head0 = (kp * kv_per_step + kk) * 2
        slopes = [slopes_ref[head0 + g] for g in range(2)]
        kvcols = slice(kk * 2 * h, (kk + 1) * 2 * h)
        ocols = slice(kk * 4 * h, (kk + 1) * 4 * h)

        def score(c, qs0=qs0, qs1=qs1, slopes=slopes, kvcols=kvcols):
            off = pl.multiple_of(c * tk, tk)
            k = k_ref[pl.ds(off, tk), kvcols]
            u0 = (off - start) // 128 + (n_blk - 1)
            dist = jnp.concatenate(
                [jnp.concatenate([dist_ref[u0 + j - rb] for j in range(tk // 128)], axis=1)
                 for rb in range(tq // 128)], axis=0)
            bias = jnp.concatenate([slopes[0] * dist, slopes[1] * dist], axis=0)
            return jnp.concatenate([_dot_nt(qs0, k[:, :h]) - bias, _dot_nt(qs1, k[:, h:]) - bias], axis=0)

        def value(c, kvcols=kvcols):
            return v_ref[pl.ds(pl.multiple_of(c * tk, tk), tk), kvcols]

        def finish(acc, l, ocols=ocols):
            a = acc / l
            o = a[:2 * tq] - lam * a[2 * tq:]
            y = o * lax.rsqrt(jnp.mean(o * o, axis=-1, keepdims=True) + EPS) * subln_ref[...] * (1.0 - lam_init)
            o_ref[:, ocols] = _unstack_heads(y, 2).astype(o_ref.dtype)

        _pipeline_step(score, value, s_ref.at[kk], macc_ref.at[kk], finish, v_width=2 * h, tk=tk,
                       mxu_row_sums=False)


def _attn_c(qkv, slopes_log2, lam_params, subln, batch, seq, lam_init, tq=128, tk=1024, kv_per_step=2):
    assert tq % 128 == 0, "the distance table holds 128 x 128 tiles"
    tk = min(tk, seq)
    in_specs, out_spec = _attn_specs(seq, tq, D_MODEL, D_MODEL + N_KV_HEADS * 2 * HEAD_DIM, 2 * HEAD_DIM,
                                     kv_per_step)
    in_specs = [_SMEM] + in_specs + [
        pl.BlockSpec((4, HEAD_DIM), lambda b, kh, qi: (0, 0)),
        pl.BlockSpec((1, 2 * HEAD_DIM), lambda b, kh, qi: (0, 0)),
    ]
    return _pipelined_attention(
        "attn_diff", functools.partial(_attn_c_kernel, tq=tq, tk=tk, seq=seq, lam_init=lam_init),
        (slopes_log2, qkv, qkv, qkv, lam_params, subln), in_specs, out_spec, batch, seq, tq, tk, 4 * tq,
        kv_per_step, extra_scratch=[pltpu.VMEM((2 * (seq // 128) - 1, 128, 128), F32)])


def _combine_kernel(*refs):
    nb = len(DILATIONS)
    o_refs, l_refs, out_ref, scratch = refs[:nb], refs[nb:2 * nb], refs[2 * nb], refs[2 * nb + 1:]
    tc = out_ref.shape[0]
    o_nat, l_nat = [], []
    for b, (_, dil) in enumerate(DILATIONS):
        if dil == 1:
            o_nat.append(o_refs[b][0].astype(F32))
            l_nat.append(l_refs[b][0])
            continue
        so, sl = scratch[2 * b], scratch[2 * b + 1]
        rows = tc // dil
        for r in range(dil):
            o_r, l_r = o_refs[b][r].astype(F32), l_refs[b][r]
            for jb in range(so.shape[0]):
                so[jb, pl.ds(r, rows, stride=dil), :] = o_r[:, jb * 128:(jb + 1) * 128]
            for jb in range(sl.shape[0]):
                sl[jb, pl.ds(r, rows, stride=dil), :] = l_r[:, jb * 128:(jb + 1) * 128]
        o_nat.append(jnp.concatenate([so[jb] for jb in range(so.shape[0])], axis=1))
        l_nat.append(jnp.concatenate([sl[jb] for jb in range(sl.shape[0])], axis=1))
    lanes = 128 // GQA_GROUP
    for h in range(N_HEADS):
        e = [l[:, h * lanes:h * lanes + 1] for l in l_nat]
        m = functools.reduce(jnp.maximum, e)
        w = [jnp.exp2(x - m) for x in e]
        cols = slice(h * HEAD_DIM, (h + 1) * HEAD_DIM)
        mix = sum(wi * o[:, cols] for wi, o in zip(w, o_nat))
        out_ref[:, cols] = (mix / sum(w)).astype(out_ref.dtype)


def _attn_d(qkv_classes, slopes, batch, seq, tc=256):
    t = qkv_classes[0].shape[1]
    dummy_sink = jnp.zeros((N_HEADS,), F32)
    branches = [_attn_window(qkv_c, slopes, dummy_sink, batch, seq, window=(win // 2) // dil, use_sink=False)
                for qkv_c, (win, dil) in zip(qkv_classes, DILATIONS)]
    lw = N_KV_HEADS * 128
    o_specs = [pl.BlockSpec((dil, tc // dil, D_MODEL), lambda i: (0, i, 0)) for _, dil in DILATIONS]
    l_specs = [pl.BlockSpec((dil, tc // dil, lw), lambda i: (0, i, 0)) for _, dil in DILATIONS]
    scratch = []
    for _ in DILATIONS:
        scratch += [pltpu.VMEM((D_MODEL // 128, tc, 128), F32), pltpu.VMEM((lw // 128, tc, 128), F32)]
    return pl.pallas_call(
        _combine_kernel,
        grid=(t // tc,),
        in_specs=o_specs + l_specs,
        out_specs=pl.BlockSpec((tc, D_MODEL), lambda i: (i, 0)),
        out_shape=jax.ShapeDtypeStruct((t, D_MODEL), BF16),
        scratch_shapes=scratch,
        compiler_params=_cparams(("parallel",), 40),
        name="dilated_combine",
    )(*[b[0] for b in branches], *[b[1] for b in branches])


def _alibi_slopes(n):
    return 2.0 ** (-8.0 * jnp.arange(1, n + 1, dtype=F32) / n)


def _lambda_init(layer):
    return 0.8 - 0.6 * math.exp(-0.3 * layer)


def _rope_tables(seq):
    rows = seq // GRID_W
    row = jnp.repeat(jnp.arange(rows, dtype=F32), GRID_W)
    col = jnp.tile(jnp.arange(GRID_W, dtype=F32), rows)
    n_freq = HEAD_DIM // 4
    inv_freq = ROPE_THETA ** (-jnp.arange(n_freq, dtype=F32) / n_freq)
    ang = jnp.stack([row, col], axis=-1)[..., None] * inv_freq
    cos, sin = jnp.cos(ang), jnp.sin(ang)
    cos_t = jnp.concatenate([cos[:, 0], cos[:, 0], cos[:, 1], cos[:, 1]], axis=-1)
    sin_t = jnp.concatenate([-sin[:, 0], sin[:, 0], -sin[:, 1], sin[:, 1]], axis=-1)
    return cos_t, sin_t


def _q_colscale(scale, n_q_cols, n_cols):
    return jnp.concatenate([jnp.full((1, n_q_cols), scale, F32), jnp.ones((1, n_cols - n_q_cols), F32)], axis=1)


def _prepare_weights(p):
    pad = D_FF_PAD - D_FF
    w_up = p['ffn_w_up']
    w_up = jnp.concatenate([jnp.pad(w_up[..., :D_FF], ((0, 0), (0, 0), (0, pad))),
                            jnp.pad(w_up[..., D_FF:], ((0, 0), (0, 0), (0, pad)))], axis=-1).astype(BF16)
    return dict(
        a_wqkv=p['a_wqkv'].astype(BF16), a_wo=p['a_wo'].astype(BF16),
        b_wqkv=p['b_wqkv'].astype(BF16), b_wo=p['b_wo'].astype(BF16),
        c_wqkv=p['c_wqkv'].astype(BF16), c_wo=p['c_wo'].astype(BF16),
        d_wqkv=p['d_wqkv'].astype(BF16), d_wo=p['d_wo'].astype(BF16),
        ffn_w_up=w_up,
        ffn_w_down=jnp.pad(p['ffn_w_down'], ((0, 0), (0, pad), (0, 0))).astype(BF16),
        ffn_conv_w=jnp.pad(p['ffn_conv_w'], ((0, 0), (0, 0), (0, pad))),
        ffn_conv_b=jnp.pad(p['ffn_conv_b'], ((0, 0), (0, pad)))[:, None, :],
    )


def _trunk(x, mod_all, row0, p, w):
    batch, seq, d = x.shape
    x2 = x.reshape(batch * seq, d)
    slopes16, slopes8 = _alibi_slopes(N_HEADS), _alibi_slopes(DIFF_HEADS)
    n_qkv = (N_HEADS + 2 * N_KV_HEADS) * HEAD_DIM
    cs_plain = jnp.ones((1, n_qkv), F32)
    cs_q_log2 = _q_colscale(Q_SCALE_LOG2, N_HEADS * HEAD_DIM, n_qkv)
    cs_q_diff = _q_colscale(Q_SCALE_LOG2, DIFF_HEADS * 2 * HEAD_DIM, 2 * D_MODEL)
    for i in range(DEPTH):
        mod = mod_all[i][:, None, :]
        g_attn = p['norm_attn'][i][None, :]
        m, j = i % 4, i // 4
        if m == 0:
            qkv = _norm_matmul(x2, g_attn, mod, row0, seq, 1, 0, w['a_wqkv'][j], cs_plain)
            cos_t, sin_t = _rope_tables(seq)
            qk_norm = jnp.stack([p['a_q_norm'][j], p['a_k_norm'][j]])
            o = _attn_a(_prep_a(qkv, cos_t, sin_t, qk_norm, seq), qkv, batch, seq)
            wo = w['a_wo'][j]
        elif m == 1:
            qkv = _norm_matmul(x2, g_attn, mod, row0, seq, 1, 0, w['b_wqkv'][j], cs_q_log2)
            o, _ = _attn_window(qkv[None], slopes16, p['b_sink'][j].astype(F32), batch, seq, window=WINDOW,
                                use_sink=True)
            o = o[0]
            wo = w['b_wo'][j]
        elif m == 2:
            qkv = _norm_matmul(x2, g_attn, mod, row0, seq, 1, 0, w['c_wqkv'][j], cs_q_diff)
            o = _attn_c(qkv, slopes8 * LOG2E, p['c_lambda'][j].astype(F32), p['c_subln'][j][None, :],
                        batch, seq, _lambda_init(i))
            wo = w['c_wo'][j]
        else:
            dils = tuple(dil for _, dil in DILATIONS if dil > 1)
            qkv, *qkv_classes = _norm_matmul(x2, g_attn, mod, row0, seq, 1, 0, w['d_wqkv'][j], cs_q_log2, dils=dils)
            o = _attn_d([qkv[None], *qkv_classes], slopes16, batch, seq)
            wo = w['d_wo'][j]
        x2 = _out_residual(o, wo, x2, mod, row0, seq, 2)
        x2 = _ffn(x2, p['norm_ffn'][i][None, :], mod, row0, seq, w['ffn_w_up'][i], w['ffn_conv_w'][i],
                  w['ffn_conv_b'][i], w['ffn_w_down'][i], p['norm_final'][None, :], final_norm=(i == DEPTH - 1))
    return x2.reshape(batch, seq, d)


def kernel(x_prompt, x_sample, c_prompt, c_sample, norm_attn, norm_ffn, w_ada, b_ada, a_wqkv, a_q_norm, a_k_norm, a_wo, b_wqkv, b_sink, b_wo, c_wqkv, c_lambda, c_subln, c_wo, d_wqkv, d_wo, ffn_w_up, ffn_conv_w, ffn_conv_b, ffn_w_down, norm_final):
    p = dict(norm_attn=norm_attn, norm_ffn=norm_ffn, a_q_norm=a_q_norm, a_k_norm=a_k_norm, b_sink=b_sink,
             c_lambda=c_lambda, c_subln=c_subln, norm_final=norm_final,
             a_wqkv=a_wqkv, a_wo=a_wo, b_wqkv=b_wqkv, b_wo=b_wo, c_wqkv=c_wqkv, c_wo=c_wo,
             d_wqkv=d_wqkv, d_wo=d_wo, ffn_w_up=ffn_w_up, ffn_conv_w=ffn_conv_w, ffn_conv_b=ffn_conv_b,
             ffn_w_down=ffn_w_down)
    w = _prepare_weights(p)
    n_prompt, n_sample = c_prompt.shape[0], c_sample.shape[0]
    rows = -(-(n_prompt + n_sample) // 8) * 8
    c_all = jnp.concatenate([c_prompt, c_sample, jnp.zeros((rows - n_prompt - n_sample, D_MODEL), F32)], axis=0)
    mod_all = _modulation(c_all, w_ada, b_ada)
    y_prompt = _trunk(x_prompt, mod_all, 0, p, w)
    y_sample = _trunk(x_sample, mod_all, n_prompt, p, w)
    return (y_prompt, y_sample)
```

```python
import functools
import math

import jax
import jax.numpy as jnp
from jax import lax
from jax.experimental import pallas as pl
from jax.experimental.pallas import tpu as pltpu

F32 = jnp.float32
BF16 = jnp.bfloat16

D_MODEL = 2048
HEAD_DIM = 128
N_HEADS = 16
N_KV_HEADS = 4
GQA_GROUP = 4
DIFF_HEADS = 8
D_FF = 5504
FF_TILE = 512
D_FF_PAD = 5632
N_MOD = 6
DEPTH = 4
GRID_W = 64
ROPE_THETA = 10000.0
EPS = 1e-6
NEG = -1e30
WINDOW = 128
DILATIONS = ((128, 1), (512, 4), (2048, 16))
Q_SCALE = HEAD_DIM ** -0.5
LOG2E = math.log2(math.e)
Q_SCALE_LOG2 = Q_SCALE * LOG2E
HALO = 16
MIB = 1024 * 1024


def _cparams(sem, vmem_mib):
    return pltpu.CompilerParams(dimension_semantics=sem, vmem_limit_bytes=vmem_mib * MIB)


def _dot(a, b):
    return jnp.dot(a, b, preferred_element_type=F32)


def _dot_nt(a, b):
    return lax.dot_general(a, b, (((1,), (1,)), ((), ())), preferred_element_type=F32)


def _norm_mod(x, g, sc, sh):
    y = x * lax.rsqrt(jnp.mean(x * x, axis=-1, keepdims=True) + EPS) * g
    return y * (1.0 + sc) + sh


def _mod_kernel(c_ref, w_ref, b_ref, o_ref):
    c = c_ref[...]
    a = (c / (1.0 + jnp.exp(-c))).astype(BF16)
    o_ref[0] = _dot(a, w_ref[0].astype(BF16)) + b_ref[0]


def _modulation(c_all, w_ada, b_ada, tn=1024):
    nb, d = c_all.shape
    depth, _, n = w_ada.shape
    return pl.pallas_call(
        _mod_kernel,
        grid=(depth, n // tn),
        in_specs=[
            pl.BlockSpec((nb, d), lambda l, j: (0, 0)),
            pl.BlockSpec((1, d, tn), lambda l, j: (l, 0, j)),
            pl.BlockSpec((1, 1, tn), lambda l, j: (l, 0, j)),
        ],
        out_specs=pl.BlockSpec((1, nb, tn), lambda l, j: (l, 0, j)),
        out_shape=jax.ShapeDtypeStruct((depth, nb, n), F32),
        compiler_params=_cparams(("parallel", "parallel"), 40),
        name="adaln_modulation",
    )(c_all, w_ada, b_ada.reshape(depth, 1, n))


def _norm_matmul_kernel(x_ref, g_ref, sc_ref, sh_ref, w_ref, cs_ref, *refs, dils):
    o_ref, class_refs, h_ref = refs[0], refs[1:1 + len(dils)], refs[1 + len(dils)]

    def project(h):
        y = _dot(h, w_ref[...]) * cs_ref[...]
        o_ref[...] = y.astype(o_ref.dtype)
        if dils:
            y_ref = refs[-1]
            nblk, rows, _ = y_ref.shape
            for jb in range(nblk):
                y_ref[jb] = y[:, jb * 128:(jb + 1) * 128]
            for dil, c_ref in zip(dils, class_refs):
                for r in range(dil):
                    c_ref[r] = jnp.concatenate(
                        [y_ref[jb, pl.ds(r, rows // dil, stride=dil), :] for jb in range(nblk)],
                        axis=1).astype(c_ref.dtype)

    @pl.when(pl.program_id(1) == 0)
    def _():
        h = _norm_mod(x_ref[...], g_ref[...], sc_ref[0], sh_ref[0]).astype(BF16)
        h_ref[...] = h
        project(h)

    @pl.when(pl.program_id(1) > 0)
    def _():
        project(h_ref[...])


def _norm_matmul(x2, g, mod, row0, seq, k_sc, k_sh, w, colscale, tm=1024, tn=1024, dils=()):
    t, d = x2.shape
    n = w.shape[1]
    tm = min(tm, seq)
    if dils:
        tn //= 2
    brow = lambda i, j: row0 + (i * tm) // seq
    out_specs = [pl.BlockSpec((tm, tn), lambda i, j: (i, j))]
    out_shape = [jax.ShapeDtypeStruct((t, n), BF16)]
    scratch = [pltpu.VMEM((tm, d), BF16)]
    for dil in dils:
        out_specs.append(pl.BlockSpec((dil, tm // dil, tn), lambda i, j: (0, i, j)))
        out_shape.append(jax.ShapeDtypeStruct((dil, t // dil, n), BF16))
    if dils:
        scratch.append(pltpu.VMEM((tn // 128, tm, 128), F32))
    outs = pl.pallas_call(
        functools.partial(_norm_matmul_kernel, dils=tuple(dils)),
        grid=(t // tm, n // tn),
        in_specs=[
            pl.BlockSpec((tm, d), lambda i, j: (i, 0)),
            pl.BlockSpec((1, d), lambda i, j: (0, 0)),
            pl.BlockSpec((1, 1, d), lambda i, j: (brow(i, j), 0, k_sc)),
            pl.BlockSpec((1, 1, d), lambda i, j: (brow(i, j), 0, k_sh)),
            pl.BlockSpec((d, tn), lambda i, j: (0, j)),
            pl.BlockSpec((1, tn), lambda i, j: (0, j)),
        ],
        out_specs=out_specs,
        out_shape=out_shape,
        scratch_shapes=scratch,
        compiler_params=_cparams(("parallel", "arbitrary"), 52),
        name="norm_qkv_proj",
    )(x2, g, mod, mod, w, colscale)
    return outs if dils else outs[0]


def _out_res_kernel(o_ref, w_ref, x_ref, gate_ref, out_ref):
    out_ref[...] = x_ref[...] + gate_ref[0] * _dot(o_ref[...], w_ref[...])


def _out_residual(o2, w, x2, mod, row0, seq, k_gate, tm=1024, tn=1024):
    t, d = x2.shape
    kdim = o2.shape[1]
    tm = min(tm, seq)
    return pl.pallas_call(
        _out_res_kernel,
        grid=(t // tm, d // tn),
        in_specs=[
            pl.BlockSpec((tm, kdim), lambda i, j: (i, 0)),
            pl.BlockSpec((kdim, tn), lambda i, j: (0, j)),
            pl.BlockSpec((tm, tn), lambda i, j: (i, j)),
            pl.BlockSpec((1, 1, tn), lambda i, j: (row0 + (i * tm) // seq, 0, k_gate * (d // tn) + j)),
        ],
        out_specs=pl.BlockSpec((tm, tn), lambda i, j: (i, j)),
        out_shape=jax.ShapeDtypeStruct((t, d), F32),
        compiler_params=_cparams(("parallel", "parallel"), 48),
        name="out_proj_residual",
    )(o2, w, x2, mod)


def _ffn_kernel(xp_ref, x_ref, xn_ref, g_ref, sc_ref, sh_ref, gate_ref, wa_ref, wb_ref,
                cw_ref, cb_ref, wd_ref, gf_ref, out_ref, h_ref, *, tm, seq, final_norm):
    i = pl.program_id(0)
    f = pl.program_id(1)
    n_ext = tm + 2 * HALO

    def partial_down(h_ext):
        a_ext = _dot(h_ext, wa_ref[...])
        b = _dot(h_ext[HALO:HALO + tm], wb_ref[...])
        a_prev = pltpu.roll(a_ext, 1, 0)[HALO:HALO + tm]
        a_next = pltpu.roll(a_ext, n_ext - 1, 0)[HALO:HALO + tm]
        a_cur = a_ext[HALO:HALO + tm]
        a = cb_ref[...] + (a_prev * cw_ref[0:1, :] + a_cur * cw_ref[1:2, :] + a_next * cw_ref[2:3, :])
        gelu = 0.5 * a * (1.0 + lax.erf(a * (2.0 ** -0.5)))
        return _dot((gelu * b).astype(BF16), wd_ref[...])

    @pl.when(f == 0)
    def _():
        g, sc, sh = g_ref[...], sc_ref[0], sh_ref[0]
        at_start = (i * tm) % seq == 0
        at_end = ((i + 1) * tm) % seq == 0
        hp = jnp.where(at_start, 0.0, _norm_mod(xp_ref[...], g, sc, sh))
        hn = jnp.where(at_end, 0.0, _norm_mod(xn_ref[...], g, sc, sh))
        h_ext = jnp.concatenate([hp, _norm_mod(x_ref[...], g, sc, sh), hn], axis=0).astype(BF16)
        h_ref[...] = h_ext
        out_ref[...] = partial_down(h_ext)

    @pl.when(f > 0)
    def _():
        out_ref[...] += partial_down(h_ref[...])

    @pl.when(f == pl.num_programs(1) - 1)
    def _():
        y = x_ref[...] + gate_ref[0] * out_ref[...]
        if final_norm:
            y = y * lax.rsqrt(jnp.mean(y * y, axis=-1, keepdims=True) + EPS) * gf_ref[...]
        out_ref[...] = y


def _ffn(x2, g, mod, row0, seq, w_up, conv_w, conv_b, w_down, g_final, final_norm, tm=512, tf=FF_TILE):
    t, d = x2.shape
    dff = w_down.shape[0]
    nf = dff // tf
    tm = min(tm, seq)
    hb = tm // HALO
    brow = lambda i, f: row0 + (i * tm) // seq
    return pl.pallas_call(
        functools.partial(_ffn_kernel, tm=tm, seq=seq, final_norm=final_norm),
        grid=(t // tm, nf),
        in_specs=[
            pl.BlockSpec((HALO, d), lambda i, f: (jnp.maximum(i * hb - 1, 0), 0)),
            pl.BlockSpec((tm, d), lambda i, f: (i, 0)),
            pl.BlockSpec((HALO, d), lambda i, f: (jnp.minimum((i + 1) * hb, t // HALO - 1), 0)),
            pl.BlockSpec((1, d), lambda i, f: (0, 0)),
            pl.BlockSpec((1, 1, d), lambda i, f: (brow(i, f), 0, 4)),
            pl.BlockSpec((1, 1, d), lambda i, f: (brow(i, f), 0, 3)),
            pl.BlockSpec((1, 1, d), lambda i, f: (brow(i, f), 0, 5)),
            pl.BlockSpec((d, tf), lambda i, f: (0, f)),
            pl.BlockSpec((d, tf), lambda i, f: (0, nf + f)),
            pl.BlockSpec((3, tf), lambda i, f: (0, f)),
            pl.BlockSpec((1, tf), lambda i, f: (0, f)),
            pl.BlockSpec((tf, d), lambda i, f: (f, 0)),
            pl.BlockSpec((1, d), lambda i, f: (0, 0)),
        ],
        out_specs=pl.BlockSpec((tm, d), lambda i, f: (i, 0)),
        out_shape=jax.ShapeDtypeStruct((t, d), F32),
        scratch_shapes=[pltpu.VMEM((tm + 2 * HALO, d), BF16)],
        compiler_params=_cparams(("parallel", "arbitrary"), 56),
        name="conv_glu_mlp",
    )(x2, x2, x2, g, mod, mod, mod, w_up, w_up, conv_w, conv_b, w_down, g_final)


def _stack_heads(q, width=HEAD_DIM):
    n = q.shape[1] // width
    return jnp.concatenate([q[:, i * width:(i + 1) * width] for i in range(n)], axis=0)


def _unstack_heads(o, n):
    tq = o.shape[0] // n
    return jnp.concatenate([o[i * tq:(i + 1) * tq] for i in range(n)], axis=1)


def _rel_iota(tq, tk):
    return lax.broadcasted_iota(jnp.int32, (tq, tk), 1) - lax.broadcasted_iota(jnp.int32, (tq, tk), 0)


def _attn_specs(seq, tq, k_col0, v_col0, kv_width, kv_per_step):
    nq = seq // tq
    qw, kw = kv_per_step * 4 * HEAD_DIM, kv_per_step * kv_width
    q_row = lambda b, qi: b * nq + jnp.minimum(qi, nq - 1)
    o_row = lambda b, qi: b * nq + jnp.maximum(qi - 1, 0)
    return [
        pl.BlockSpec((tq, qw), lambda b, kh, qi: (q_row(b, qi), kh)),
        pl.BlockSpec((seq, kw), lambda b, kh, qi: (b, k_col0 // kw + kh)),
        pl.BlockSpec((seq, kw), lambda b, kh, qi: (b, v_col0 // kw + kh)),
    ], pl.BlockSpec((tq, qw), lambda b, kh, qi: (o_row(b, qi), kh))


_SMEM = pl.BlockSpec(memory_space=pltpu.SMEM)
_ATTN_SEM = ("parallel", "parallel", "arbitrary")


def _with_ones(v):
    return jnp.concatenate([v, jnp.ones(v.shape, v.dtype)], axis=1)


def _pipeline_init(qi, s_ref, macc_ref):
    @pl.when(qi == 0)
    def _():
        s_ref[...] = jnp.zeros(s_ref.shape, F32)
        macc_ref[...] = jnp.zeros(macc_ref.shape, F32)


def _pipeline_step(score_fn, value_fn, s_ref, macc_ref, finish, *, v_width, tk, mxu_row_sums):
    nck, m_rows, _ = s_ref.shape
    m_prev = jnp.max(macc_ref[...], axis=-1, keepdims=True)

    def body(c, carry):
        macc, lacc, acc = carry
        x = s_ref[c] - m_prev
        s = score_fn(c)
        s_ref[c] = s
        for j in range(tk // 128):
            macc = jnp.maximum(macc, s[:, j * 128:(j + 1) * 128])
        v = value_fn(c)
        if mxu_row_sums:
            acc = acc + _dot(jnp.exp2(x.astype(BF16)), _with_ones(v))
        else:
            p = jnp.exp2(x)
            for j in range(tk // 128):
                lacc = lacc + p[:, j * 128:(j + 1) * 128]
            acc = acc + _dot(p.astype(BF16), v)
        return macc, lacc, acc

    acc_width = 2 * v_width if mxu_row_sums else v_width
    init = (jnp.full((m_rows, 128), NEG, F32), jnp.zeros((m_rows, 128), F32),
            jnp.zeros((m_rows, acc_width), F32))
    macc, lacc, acc = lax.fori_loop(0, nck, body, init, unroll=True)
    macc_ref[...] = macc
    if mxu_row_sums:
        finish(acc[:, :v_width], acc[:, v_width:v_width + 1])
    else:
        finish(acc, jnp.sum(lacc, axis=-1, keepdims=True))


def _pipelined_attention(name, kernel_fn, operands, in_specs, out_spec, batch, seq, tq, tk, m_rows, kv_per_step,
                         extra_scratch=()):
    nck = seq // tk
    return pl.pallas_call(
        kernel_fn,
        grid=(batch, N_KV_HEADS // kv_per_step, seq // tq + 1),
        in_specs=in_specs,
        out_specs=out_spec,
        out_shape=jax.ShapeDtypeStruct((batch * seq, D_MODEL), BF16),
        scratch_shapes=[pltpu.VMEM((kv_per_step, nck, m_rows, tk), F32),
                        pltpu.VMEM((kv_per_step, m_rows, 128), F32), *extra_scratch],
        compiler_params=_cparams(_ATTN_SEM, 52),
        name=name,
    )(*operands)


def _prep_a_kernel(qkv_ref, cos_ref, sin_ref, w_ref, o_ref):
    tp = qkv_ref.shape[0]
    cos, sin = cos_ref[...], sin_ref[...]
    lane = lax.broadcasted_iota(jnp.int32, (tp, HEAD_DIM), 1)
    low_half = (lane % (HEAD_DIM // 2)) < (HEAD_DIM // 4)
    for h in range(N_HEADS + N_KV_HEADS):
        cols = slice(h * HEAD_DIM, (h + 1) * HEAD_DIM)
        x = qkv_ref[:, cols].astype(F32)
        is_q = h < N_HEADS
        y = x * lax.rsqrt(jnp.mean(x * x, axis=-1, keepdims=True) + EPS) * (w_ref[0:1, :] if is_q else w_ref[1:2, :])
        partner = jnp.where(low_half, pltpu.roll(y, 3 * HEAD_DIM // 4, 1), pltpu.roll(y, HEAD_DIM // 4, 1))
        r = y * cos + partner * sin
        if is_q:
            r = r * Q_SCALE_LOG2
        o_ref[:, cols] = r.astype(BF16)


def _prep_a(qkv, cos_t, sin_t, qk_norm, seq, tp=512):
    t = qkv.shape[0]
    tp = min(tp, seq)
    width = (N_HEADS + N_KV_HEADS) * HEAD_DIM
    return pl.pallas_call(
        _prep_a_kernel,
        grid=(t // tp,),
        in_specs=[
            pl.BlockSpec((tp, width), lambda i: (i, 0)),
            pl.BlockSpec((tp, HEAD_DIM), lambda i: (i % (seq // tp), 0)),
            pl.BlockSpec((tp, HEAD_DIM), lambda i: (i % (seq // tp), 0)),
            pl.BlockSpec((2, HEAD_DIM), lambda i: (0, 0)),
        ],
        out_specs=pl.BlockSpec((tp, width), lambda i: (i, 0)),
        out_shape=jax.ShapeDtypeStruct((t, width), BF16),
        compiler_params=_cparams(("parallel",), 32),
        name="rope_qk_norm",
    )(qkv, cos_t, sin_t, qk_norm)


def _attn_a_kernel(q_ref, k_ref, v_ref, o_ref, s_ref, macc_ref, *, tk):
    _pipeline_init(pl.program_id(2), s_ref, macc_ref)
    for kk in range(s_ref.shape[0]):
        kcols = slice(kk * HEAD_DIM, (kk + 1) * HEAD_DIM)
        qcols = slice(kk * GQA_GROUP * HEAD_DIM, (kk + 1) * GQA_GROUP * HEAD_DIM)
        qs = _stack_heads(q_ref[:, qcols])

        def score(c, qs=qs, kcols=kcols):
            return _dot_nt(qs, k_ref[pl.ds(pl.multiple_of(c * tk, tk), tk), kcols])

        def value(c, kcols=kcols):
            return v_ref[pl.ds(pl.multiple_of(c * tk, tk), tk), kcols]

        def finish(acc, l, qcols=qcols):
            o_ref[:, qcols] = _unstack_heads(acc / l, GQA_GROUP).astype(o_ref.dtype)

        _pipeline_step(score, value, s_ref.at[kk], macc_ref.at[kk], finish, v_width=HEAD_DIM, tk=tk,
                       mxu_row_sums=True)


def _attn_a(qk_rot, qkv, batch, seq, tq=128, tk=1024, kv_per_step=2):
    tk = min(tk, seq)
    in_specs, out_spec = _attn_specs(seq, tq, N_HEADS * HEAD_DIM, (N_HEADS + N_KV_HEADS) * HEAD_DIM, HEAD_DIM,
                                     kv_per_step)
    return _pipelined_attention("attn_axial", functools.partial(_attn_a_kernel, tk=tk), (qk_rot, qk_rot, qkv),
                                in_specs, out_spec, batch, seq, tq, tk, GQA_GROUP * tq, kv_per_step)


KV_PER_STEP = 4


def _window_kernel(slopes_ref, sink_ref, q_ref, k_ref, v_ref, o_ref, lse_ref, s_ref, macc_ref,
                   *, tq, band, window, nq, seq_c, dil, use_sink, n_steps):
    n = pl.program_id(0)
    pairs = N_KV_HEADS // KV_PER_STEP

    def pair_and_block(step):
        return (step // nq) % pairs, step % nq

    def band_start(qi):
        if band == seq_c:
            return 0
        return pl.multiple_of(jnp.clip(qi * tq - window, 0, seq_c - band), window)

    @pl.when(n == 0)
    def _():
        s_ref[...] = jnp.zeros(s_ref.shape, F32)
        macc_ref[...] = jnp.zeros(macc_ref.shape, F32)

    kp, qi = pair_and_block(jnp.minimum(n, n_steps - 1))
    kpp, qip = pair_and_block(jnp.maximum(n - 1, 0))
    ws, wsp = band_start(qi), band_start(qip)
    rel = jnp.abs(_rel_iota(tq, band) + (ws - qi * tq))
    in_window = rel <= window
    dist = rel.astype(F32)
    for kk in range(KV_PER_STEP):
        kcols = slice(kk * HEAD_DIM, (kk + 1) * HEAD_DIM)
        qcols = slice(kk * GQA_GROUP * HEAD_DIM, (kk + 1) * GQA_GROUP * HEAD_DIM)
        head0 = (kpp * KV_PER_STEP + kk) * GQA_GROUP
        m = jnp.max(macc_ref[kk], axis=-1, keepdims=True)
        if use_sink:
            sink = jnp.concatenate(
                [jnp.full((tq, 1), sink_ref[head0 + g] * LOG2E, F32) for g in range(GQA_GROUP)], axis=0)
            m = jnp.maximum(m, sink)
        x = s_ref[kk] - m
        head0 = (kp * KV_PER_STEP + kk) * GQA_GROUP
        bias = jnp.concatenate(
            [jnp.where(in_window, -(slopes_ref[head0 + g] * (dil * LOG2E)) * dist, NEG)
             for g in range(GQA_GROUP)], axis=0)
        s = _dot_nt(_stack_heads(q_ref[:, qcols]), k_ref[pl.ds(ws, band), kcols]) + bias
        s_ref[kk] = s
        macc = s[:, 0:128]
        for j in range(1, band // 128):
            macc = jnp.maximum(macc, s[:, j * 128:(j + 1) * 128])
        macc_ref[kk] = macc
        ol = _dot(jnp.exp2(x.astype(BF16)), _with_ones(v_ref[pl.ds(wsp, band), kcols]))
        l = ol[:, HEAD_DIM:HEAD_DIM + 1]
        if use_sink:
            l = l + jnp.exp2(sink - m)
        o_ref[:, qcols] = _unstack_heads(ol[:, :HEAD_DIM] / l, GQA_GROUP).astype(o_ref.dtype)
        lse = m + jnp.log2(l)
        lse_ref[:, kcols] = jnp.concatenate(
            [jnp.broadcast_to(lse[g * tq:(g + 1) * tq], (tq, 128 // GQA_GROUP)) for g in range(GQA_GROUP)], axis=1)


def _attn_window(qkv_c, slopes, sink, batch, seq, *, window, use_sink):
    dil, t_c, width = qkv_c.shape
    seq_c = seq // dil
    tq = min(256, seq_c)
    band = min(tq + 2 * window, seq_c)
    nq = seq_c // tq
    assert band % 128 == 0 and seq_c % tq == 0, "bands are reduced in 128-lane blocks"
    pairs = N_KV_HEADS // KV_PER_STEP
    n_steps = batch * dil * pairs * nq
    qw = KV_PER_STEP * GQA_GROUP * HEAD_DIM
    kw = KV_PER_STEP * HEAD_DIM

    def split(step):
        qi = step % nq
        rest = step // nq
        kp = rest % pairs
        rest = rest // pairs
        return rest // dil, rest % dil, kp, qi

    def q_map(n):
        b, r, kp, qi = split(jnp.minimum(n, n_steps - 1))
        return r, b * nq + qi, kp

    def k_map(n):
        b, r, kp, _ = split(jnp.minimum(n, n_steps - 1))
        return r, b, N_HEADS // KV_PER_STEP + kp

    def v_map(n):
        b, r, kp, _ = split(jnp.maximum(n - 1, 0))
        return r, b, (N_HEADS + N_KV_HEADS) // KV_PER_STEP + kp

    def o_map(n):
        b, r, kp, qi = split(jnp.maximum(n - 1, 0))
        return r, b * nq + qi, kp

    return pl.pallas_call(
        functools.partial(_window_kernel, tq=tq, band=band, window=window, nq=nq, seq_c=seq_c, dil=dil,
                          use_sink=use_sink, n_steps=n_steps),
        grid=(n_steps + 1,),
        in_specs=[_SMEM, _SMEM,
                  pl.BlockSpec((None, tq, qw), q_map),
                  pl.BlockSpec((None, seq_c, kw), k_map),
                  pl.BlockSpec((None, seq_c, kw), v_map)],
        out_specs=[pl.BlockSpec((None, tq, qw), o_map), pl.BlockSpec((None, tq, kw), o_map)],
        out_shape=[jax.ShapeDtypeStruct((dil, t_c, D_MODEL), BF16),
                   jax.ShapeDtypeStruct((dil, t_c, N_KV_HEADS * 128), F32)],
        scratch_shapes=[pltpu.VMEM((KV_PER_STEP, GQA_GROUP * tq, band), F32),
                        pltpu.VMEM((KV_PER_STEP, GQA_GROUP * tq, 128), F32)],
        compiler_params=_cparams(("arbitrary",), 40),
        name=f"attn_window_d{dil}",
    )(slopes, sink, qkv_c, qkv_c, qkv_c)


def _attn_c_kernel(slopes_ref, q_ref, k_ref, v_ref, lam_ref, subln_ref, o_ref, s_ref, macc_ref, dist_ref,
                   *, tq, tk, seq, lam_init):
    kp, qi = pl.program_id(1), pl.program_id(2)
    start = jnp.minimum(qi, seq // tq - 1) * tq
    h = HEAD_DIM
    n_blk = seq // 128
    kv_per_step = s_ref.shape[0]

    @pl.when(qi == 0)
    def _():
        lane_minus_row = _rel_iota(128, 128)
        for u in range(2 * n_blk - 1):
            dist_ref[u] = jnp.abs(lane_minus_row + (u - (n_blk - 1)) * 128).astype(F32)

    _pipeline_init(qi, s_ref, macc_ref)
    lp = lam_ref[...]
    lam = (jnp.exp(jnp.sum(lp[0:1] * lp[1:2], axis=-1, keepdims=True))
           - jnp.exp(jnp.sum(lp[2:3] * lp[3:4], axis=-1, keepdims=True)) + lam_init)

    for kk in range(kv_per_step):
        q = q_ref[:, kk * 4 * h:(kk + 1) * 4 * h]
        qs0 = jnp.concatenate([q[:, 0:h], q[:, 2 * h:3 * h]], axis=0)
        qs1 = jnp.concatenate([q[:, h:2 * h], q[:, 3 * h:4 * h]], axis=0)
        head0 = (kp * kv_per_step + kk) * 2
        slopes = [slopes_ref[head0 + g] for g in range(2)]
        kvcols = slice(kk * 2 * h, (kk + 1) * 2 * h)
        ocols = slice(kk * 4 * h, (kk + 1) * 4 * h)

        def score(c, qs0=qs0, qs1=qs1, slopes=slopes, kvcols=kvcols):
            off = pl.multiple_of(c * tk, tk)
            k = k_ref[pl.ds(off, tk), kvcols]
            u0 = (off - start) // 128 + (n_blk - 1)
            dist = jnp.concatenate(
                [jnp.concatenate([dist_ref[u0 + j - rb] for j in range(tk // 128)], axis=1)
                 for rb in range(tq // 128)], axis=0)
            bias = jnp.concatenate([slopes[0] * dist, slopes[1] * dist], axis=0)
            return jnp.concatenate([_dot_nt(qs0, k[:, :h]) - bias, _dot_nt(qs1, k[:, h:]) - bias], axis=0)

        def value(c, kvcols=kvcols):
            return v_ref[pl.ds(pl.multiple_of(c * tk, tk), tk), kvcols]

        def finish(acc, l, ocols=ocols):
            a = acc / l
            o = a[:2 * tq] - lam * a[2 * tq:]
            y = o * lax.rsqrt(jnp.mean(o * o, axis=-1, keepdims=True) + EPS) * subln_ref[...] * (1.0 - lam_init)
            o_ref[:, ocols] = _unstack_heads(y, 2).astype(o_ref.dtype)

        _pipeline_step(score, value, s_ref.at[kk], macc_ref.at[kk], finish, v_width=2 * h, tk=tk,
                       mxu_row_sums=False)


def _attn_c(qkv, slopes_log2, lam_params, subln, batch, seq, lam_init, tq=128, tk=1024, kv_per_step=2):
    assert tq % 128 == 0, "the distance table holds 128 x 128 tiles"
    tk = min(tk, seq)
    in_specs, out_spec = _attn_specs(seq, tq, D_MODEL, D_MODEL + N_KV_HEADS * 2 * HEAD_DIM, 2 * HEAD_DIM,
                                     kv_per_step)
    in_specs = [_SMEM] + in_specs + [
        pl.BlockSpec((4, HEAD_DIM), lambda b, kh, qi: (0, 0)),
        pl.BlockSpec((1, 2 * HEAD_DIM), lambda b, kh, qi: (0, 0)),
    ]
    return _pipelined_attention(
        "attn_diff", functools.partial(_attn_c_kernel, tq=tq, tk=tk, seq=seq, lam_init=lam_init),
        (slopes_log2, qkv, qkv, qkv, lam_params, subln), in_specs, out_spec, batch, seq, tq, tk, 4 * tq,
        kv_per_step, extra_scratch=[pltpu.VMEM((2 * (seq // 128) - 1, 128, 128), F32)])


def _combine_kernel(*refs):
    nb = len(DILATIONS)
    o_refs, l_refs, out_ref, scratch = refs[:nb], refs[nb:2 * nb], refs[2 * nb], refs[2 * nb + 1:]
    tc = out_ref.shape[0]
    o_nat, l_nat = [], []
    for b, (_, dil) in enumerate(DILATIONS):
        if dil == 1:
            o_nat.append(o_refs[b][0].astype(F32))
            l_nat.append(l_refs[b][0])
            continue
        so, sl = scratch[2 * b], scratch[2 * b + 1]
        rows = tc // dil
        for r in range(dil):
            o_r, l_r = o_refs[b][r].astype(F32), l_refs[b][r]
            for jb in range(so.shape[0]):
                so[jb, pl.ds(r, rows, stride=dil), :] = o_r[:, jb * 128:(jb + 1) * 128]
            for jb in range(sl.shape[0]):
                sl[jb, pl.ds(r, rows, stride=dil), :] = l_r[:, jb * 128:(jb + 1) * 128]
        o_nat.append(jnp.concatenate([so[jb] for jb in range(so.shape[0])], axis=1))
        l_nat.append(jnp.concatenate([sl[jb] for jb in range(sl.shape[0])], axis=1))
    lanes = 128 // GQA_GROUP
    for h in range(N_HEADS):
        e = [l[:, h * lanes:h * lanes + 1] for l in l_nat]
        m = functools.reduce(jnp.maximum, e)
        w = [jnp.exp2(x - m) for x in e]
        cols = slice(h * HEAD_DIM, (h + 1) * HEAD_DIM)
        mix = sum(wi * o[:, cols] for wi, o in zip(w, o_nat))
        out_ref[:, cols] = (mix / sum(w)).astype(out_ref.dtype)


def _attn_d(qkv_classes, slopes, batch, seq, tc=256):
    t = qkv_classes[0].shape[1]
    dummy_sink = jnp.zeros((N_HEADS,), F32)
    branches = [_attn_window(qkv_c, slopes, dummy_sink, batch, seq, window=(win // 2) // dil, use_sink=False)
                for qkv_c, (win, dil) in zip(qkv_classes, DILATIONS)]
    lw = N_KV_HEADS * 128
    o_specs = [pl.BlockSpec((dil, tc // dil, D_MODEL), lambda i: (0, i, 0)) for _, dil in DILATIONS]
    l_specs = [pl.BlockSpec((dil, tc // dil, lw), lambda i: (0, i, 0)) for _, dil in DILATIONS]
    scratch = []
    for _ in DILATIONS:
        scratch += [pltpu.VMEM((D_MODEL // 128, tc, 128), F32), pltpu.VMEM((lw // 128, tc, 128), F32)]
    return pl.pallas_call(
        _combine_kernel,
        grid=(t // tc,),
        in_specs=o_specs + l_specs,
        out_specs=pl.BlockSpec((tc, D_MODEL), lambda i: (i, 0)),
        out_shape=jax.ShapeDtypeStruct((t, D_MODEL), BF16),
        scratch_shapes=scratch,
        compiler_params=_cparams(("parallel",), 40),
        name="dilated_combine",
    )(*[b[0] for b in branches], *[b[1] for b in branches])


def _alibi_slopes(n):
    return 2.0 ** (-8.0 * jnp.arange(1, n + 1, dtype=F32) / n)


def _lambda_init(layer):
    return 0.8 - 0.6 * math.exp(-0.3 * layer)


def _rope_tables(seq):
    rows = seq // GRID_W
    row = jnp.repeat(jnp.arange(rows, dtype=F32), GRID_W)
    col = jnp.tile(jnp.arange(GRID_W, dtype=F32), rows)
    n_freq = HEAD_DIM // 4
    inv_freq = ROPE_THETA ** (-jnp.arange(n_freq, dtype=F32) / n_freq)
    ang = jnp.stack([row, col], axis=-1)[..., None] * inv_freq
    cos, sin = jnp.cos(ang), jnp.sin(ang)
    cos_t = jnp.concatenate([cos[:, 0], cos[:, 0], cos[:, 1], cos[:, 1]], axis=-1)
    sin_t = jnp.concatenate([-sin[:, 0], sin[:, 0], -sin[:, 1], sin[:, 1]], axis=-1)
    return cos_t, sin_t


def _q_colscale(scale, n_q_cols, n_cols):
    return jnp.concatenate([jnp.full((1, n_q_cols), scale, F32), jnp.ones((1, n_cols - n_q_cols), F32)], axis=1)


def _prepare_weights(p):
    pad = D_FF_PAD - D_FF
    w_up = p['ffn_w_up']
    w_up = jnp.concatenate([jnp.pad(w_up[..., :D_FF], ((0, 0), (0, 0), (0, pad))),
                            jnp.pad(w_up[..., D_FF:], ((0, 0), (0, 0), (0, pad)))], axis=-1).astype(BF16)
    return dict(
        a_wqkv=p['a_wqkv'].astype(BF16), a_wo=p['a_wo'].astype(BF16),
        b_wqkv=p['b_wqkv'].astype(BF16), b_wo=p['b_wo'].astype(BF16),
        c_wqkv=p['c_wqkv'].astype(BF16), c_wo=p['c_wo'].astype(BF16),
        d_wqkv=p['d_wqkv'].astype(BF16), d_wo=p['d_wo'].astype(BF16),
        ffn_w_up=w_up,
        ffn_w_down=jnp.pad(p['ffn_w_down'], ((0, 0), (0, pad), (0, 0))).astype(BF16),
        ffn_conv_w=jnp.pad(p['ffn_conv_w'], ((0, 0), (0, 0), (0, pad))),
        ffn_conv_b=jnp.pad(p['ffn_conv_b'], ((0, 0), (0, pad)))[:, None, :],
    )


def _trunk(x, mod_all, row0, p, w):
    batch, seq, d = x.shape
    x2 = x.reshape(batch * seq, d)
    slopes16, slopes8 = _alibi_slopes(N_HEADS), _alibi_slopes(DIFF_HEADS)
    n_qkv = (N_HEADS + 2 * N_KV_HEADS) * HEAD_DIM
    cs_plain = jnp.ones((1, n_qkv), F32)
    cs_q_log2 = _q_colscale(Q_SCALE_LOG2, N_HEADS * HEAD_DIM, n_qkv)
    cs_q_diff = _q_colscale(Q_SCALE_LOG2, DIFF_HEADS * 2 * HEAD_DIM, 2 * D_MODEL)
    for i in range(DEPTH):
        mod = mod_all[i][:, None, :]
        g_attn = p['norm_attn'][i][None, :]
        m, j = i % 4, i // 4
        if m == 0:
            qkv = _norm_matmul(x2, g_attn, mod, row0, seq, 1, 0, w['a_wqkv'][j], cs_plain)
            cos_t, sin_t = _rope_tables(seq)
            qk_norm = jnp.stack([p['a_q_norm'][j], p['a_k_norm'][j]])
            o = _attn_a(_prep_a(qkv, cos_t, sin_t, qk_norm, seq), qkv, batch, seq)
            wo = w['a_wo'][j]
        elif m == 1:
            qkv = _norm_matmul(x2, g_attn, mod, row0, seq, 1, 0, w['b_wqkv'][j], cs_q_log2)
            o, _ = _attn_window(qkv[None], slopes16, p['b_sink'][j].astype(F32), batch, seq, window=WINDOW,
                                use_sink=True)
            o = o[0]
            wo = w['b_wo'][j]
        elif m == 2:
            qkv = _norm_matmul(x2, g_attn, mod, row0, seq, 1, 0, w['c_wqkv'][j], cs_q_diff)
            o = _attn_c(qkv, slopes8 * LOG2E, p['c_lambda'][j].astype(F32), p['c_subln'][j][None, :],
                        batch, seq, _lambda_init(i))
            wo = w['c_wo'][j]
        else:
            dils = tuple(dil for _, dil in DILATIONS if dil > 1)
            qkv, *qkv_classes = _norm_matmul(x2, g_attn, mod, row0, seq, 1, 0, w['d_wqkv'][j], cs_q_log2, dils=dils)
            o = _attn_d([qkv[None], *qkv_classes], slopes16, batch, seq)
            wo = w['d_wo'][j]
        x2 = _out_residual(o, wo, x2, mod, row0, seq, 2)
        x2 = _ffn(x2, p['norm_ffn'][i][None, :], mod, row0, seq, w['ffn_w_up'][i], w['ffn_conv_w'][i],
                  w['ffn_conv_b'][i], w['ffn_w_down'][i], p['norm_final'][None, :], final_norm=(i == DEPTH - 1))
    return x2.reshape(batch, seq, d)


def kernel(x_prompt, x_sample, c_prompt, c_sample, norm_attn, norm_ffn, w_ada, b_ada, a_wqkv, a_q_norm, a_k_norm, a_wo, b_wqkv, b_sink, b_wo, c_wqkv, c_lambda, c_subln, c_wo, d_wqkv, d_wo, ffn_w_up, ffn_conv_w, ffn_conv_b, ffn_w_down, norm_final):
    p = dict(norm_attn=norm_attn, norm_ffn=norm_ffn, a_q_norm=a_q_norm, a_k_norm=a_k_norm, b_sink=b_sink,
             c_lambda=c_lambda, c_subln=c_subln, norm_final=norm_final,
             a_wqkv=a_wqkv, a_wo=a_wo, b_wqkv=b_wqkv, b_wo=b_wo, c_wqkv=c_wqkv, c_wo=c_wo,
             d_wqkv=d_wqkv, d_wo=d_wo, ffn_w_up=ffn_w_up, ffn_conv_w=ffn_conv_w, ffn_conv_b=ffn_conv_b,
             ffn_w_down=ffn_w_down)
    w = _prepare_weights(p)
    n_prompt, n_sample = c_prompt.shape[0], c_sample.shape[0]
    rows = -(-(n_prompt + n_sample) // 8) * 8
    c_all = jnp.concatenate([c_prompt, c_sample, jnp.zeros((rows - n_prompt - n_sample, D_MODEL), F32)], axis=0)
    mod_all = _modulation(c_all, w_ada, b_ada)
    y_prompt = _trunk(x_prompt, mod_all, 0, p, w)
    y_sample = _trunk(x_sample, mod_all, n_prompt, p, w)
    return (y_prompt, y_sample)
```

```python
import functools
import math

import jax
import jax.numpy as jnp
from jax import lax
from jax.experimental import pallas as pl
from jax.experimental.pallas import tpu as pltpu

F32 = jnp.float32
BF16 = jnp.bfloat16

D_MODEL = 2048
HEAD_DIM = 128
N_HEADS = 16
N_KV_HEADS = 4
GQA_GROUP = 4
DIFF_HEADS = 8
D_FF = 5504
FF_TILE = 512
D_FF_PAD = 5632
N_MOD = 6
DEPTH = 4
GRID_W = 64
ROPE_THETA = 10000.0
EPS = 1e-6
NEG = -1e30
WINDOW = 128
DILATIONS = ((128, 1), (512, 4), (2048, 16))
Q_SCALE = HEAD_DIM ** -0.5
LOG2E = math.log2(math.e)
Q_SCALE_LOG2 = Q_SCALE * LOG2E
HALO = 16
MIB = 1024 * 1024


def _cparams(sem, vmem_mib):
    return pltpu.CompilerParams(dimension_semantics=sem, vmem_limit_bytes=vmem_mib * MIB)


def _dot(a, b):
    return jnp.dot(a, b, preferred_element_type=F32)


def _dot_nt(a, b):
    return lax.dot_general(a, b, (((1,), (1,)), ((), ())), preferred_element_type=F32)


def _norm_mod(x, g, sc, sh):
    y = x * lax.rsqrt(jnp.mean(x * x, axis=-1, keepdims=True) + EPS) * g
    return y * (1.0 + sc) + sh


def _mod_kernel(c_ref, w_ref, b_ref, o_ref):
    c = c_ref[...]
    a = (c / (1.0 + jnp.exp(-c))).astype(BF16)
    o_ref[0] = _dot(a, w_ref[0].astype(BF16)) + b_ref[0]


def _modulation(c_all, w_ada, b_ada, tn=1024):
    nb, d = c_all.shape
    depth, _, n = w_ada.shape
    return pl.pallas_call(
        _mod_kernel,
        grid=(depth, n // tn),
        in_specs=[
            pl.BlockSpec((nb, d), lambda l, j: (0, 0)),
            pl.BlockSpec((1, d, tn), lambda l, j: (l, 0, j)),
            pl.BlockSpec((1, 1, tn), lambda l, j: (l, 0, j)),
        ],
        out_specs=pl.BlockSpec((1, nb, tn), lambda l, j: (l, 0, j)),
        out_shape=jax.ShapeDtypeStruct((depth, nb, n), F32),
        compiler_params=_cparams(("parallel", "parallel"), 40),
        name="adaln_modulation",
    )(c_all, w_ada, b_ada.reshape(depth, 1, n))


def _norm_matmul_kernel(x_ref, g_ref, sc_ref, sh_ref, w_ref, cs_ref, *refs, dils):
    o_ref, class_refs, h_ref = refs[0], refs[1:1 + len(dils)], refs[1 + len(dils)]

    def project(h):
        y = _dot(h, w_ref[...]) * cs_ref[...]
        o_ref[...] = y.astype(o_ref.dtype)
        if dils:
            y_ref = refs[-1]
            nblk, rows, _ = y_ref.shape
            for jb in range(nblk):
                y_ref[jb] = y[:, jb * 128:(jb + 1) * 128]
            for dil, c_ref in zip(dils, class_refs):
                for r in range(dil):
                    c_ref[r] = jnp.concatenate(
                        [y_ref[jb, pl.ds(r, rows // dil, stride=dil), :] for jb in range(nblk)],
                        axis=1).astype(c_ref.dtype)

    @pl.when(pl.program_id(1) == 0)
    def _():
        h = _norm_mod(x_ref[...], g_ref[...], sc_ref[0], sh_ref[0]).astype(BF16)
        h_ref[...] = h
        project(h)

    @pl.when(pl.program_id(1) > 0)
    def _():
        project(h_ref[...])


def _norm_matmul(x2, g, mod, row0, seq, k_sc, k_sh, w, colscale, tm=1024, tn=1024, dils=()):
    t, d = x2.shape
    n = w.shape[1]
    tm = min(tm, seq)
    if dils:
        tn //= 2
    brow = lambda i, j: row0 + (i * tm) // seq
    out_specs = [pl.BlockSpec((tm, tn), lambda i, j: (i, j))]
    out_shape = [jax.ShapeDtypeStruct((t, n), BF16)]
    scratch = [pltpu.VMEM((tm, d), BF16)]
    for dil in dils:
        out_specs.append(pl.BlockSpec((dil, tm // dil, tn), lambda i, j: (0, i, j)))
        out_shape.append(jax.ShapeDtypeStruct((dil, t // dil, n), BF16))
    if dils:
        scratch.append(pltpu.VMEM((tn // 128, tm, 128), F32))
    outs = pl.pallas_call(
        functools.partial(_norm_matmul_kernel, dils=tuple(dils)),
        grid=(t // tm, n // tn),
        in_specs=[
            pl.BlockSpec((tm, d), lambda i, j: (i, 0)),
            pl.BlockSpec((1, d), lambda i, j: (0, 0)),
            pl.BlockSpec((1, 1, d), lambda i, j: (brow(i, j), 0, k_sc)),
            pl.BlockSpec((1, 1, d), lambda i, j: (brow(i, j), 0, k_sh)),
            pl.BlockSpec((d, tn), lambda i, j: (0, j)),
            pl.BlockSpec((1, tn), lambda i, j: (0, j)),
        ],
        out_specs=out_specs,
        out_shape=out_shape,
        scratch_shapes=scratch,
        compiler_params=_cparams(("parallel", "arbitrary"), 52),
        name="norm_qkv_proj",
    )(x2, g, mod, mod, w, colscale)
    return outs if dils else outs[0]


def _out_res_kernel(o_ref, w_ref, x_ref, gate_ref, out_ref):
    out_ref[...] = x_ref[...] + gate_ref[0] * _dot(o_ref[...], w_ref[...])


def _out_residual(o2, w, x2, mod, row0, seq, k_gate, tm=1024, tn=1024):
    t, d = x2.shape
    kdim = o2.shape[1]
    tm = min(tm, seq)
    return pl.pallas_call(
        _out_res_kernel,
        grid=(t // tm, d // tn),
        in_specs=[
            pl.BlockSpec((tm, kdim), lambda i, j: (i, 0)),
            pl.BlockSpec((kdim, tn), lambda i, j: (0, j)),
            pl.BlockSpec((tm, tn), lambda i, j: (i, j)),
            pl.BlockSpec((1, 1, tn), lambda i, j: (row0 + (i * tm) // seq, 0, k_gate * (d // tn) + j)),
        ],
        out_specs=pl.BlockSpec((tm, tn), lambda i, j: (i, j)),
        out_shape=jax.ShapeDtypeStruct((t, d), F32),
        compiler_params=_cparams(("parallel", "parallel"), 48),
        name="out_proj_residual",
    )(o2, w, x2, mod)


def _ffn_kernel(xp_ref, x_ref, xn_ref, g_ref, sc_ref, sh_ref, gate_ref, wa_ref, wb_ref,
                cw_ref, cb_ref, wd_ref, gf_ref, out_ref, h_ref, *, tm, seq, final_norm):
    i = pl.program_id(0)
    f = pl.program_id(1)
    n_ext = tm + 2 * HALO

    def partial_down(h_ext):
        a_ext = _dot(h_ext, wa_ref[...])
        b = _dot(h_ext[HALO:HALO + tm], wb_ref[...])
        a_prev = pltpu.roll(a_ext, 1, 0)[HALO:HALO + tm]
        a_next = pltpu.roll(a_ext, n_ext - 1, 0)[HALO:HALO + tm]
        a_cur = a_ext[HALO:HALO + tm]
        a = cb_ref[...] + (a_prev * cw_ref[0:1, :] + a_cur * cw_ref[1:2, :] + a_next * cw_ref[2:3, :])
        gelu = 0.5 * a * (1.0 + lax.erf(a * (2.0 ** -0.5)))
        return _dot((gelu * b).astype(BF16), wd_ref[...])

    @pl.when(f == 0)
    def _():
        g, sc, sh = g_ref[...], sc_ref[0], sh_ref[0]
        at_start = (i * tm) % seq == 0
        at_end = ((i + 1) * tm) % seq == 0
        hp = jnp.where(at_start, 0.0, _norm_mod(xp_ref[...], g, sc, sh))
        hn = jnp.where(at_end, 0.0, _norm_mod(xn_ref[...], g, sc, sh))
        h_ext = jnp.concatenate([hp, _norm_mod(x_ref[...], g, sc, sh), hn], axis=0).astype(BF16)
        h_ref[...] = h_ext
        out_ref[...] = partial_down(h_ext)

    @pl.when(f > 0)
    def _():
        out_ref[...] += partial_down(h_ref[...])

    @pl.when(f == pl.num_programs(1) - 1)
    def _():
        y = x_ref[...] + gate_ref[0] * out_ref[...]
        if final_norm:
            y = y * lax.rsqrt(jnp.mean(y * y, axis=-1, keepdims=True) + EPS) * gf_ref[...]
        out_ref[...] = y


def _ffn(x2, g, mod, row0, seq, w_up, conv_w, conv_b, w_down, g_final, final_norm, tm=512, tf=FF_TILE):
    t, d = x2.shape
    dff = w_down.shape[0]
    nf = dff // tf
    tm = min(tm, seq)
    hb = tm // HALO
    brow = lambda i, f: row0 + (i * tm) // seq
    return pl.pallas_call(
        functools.partial(_ffn_kernel, tm=tm, seq=seq, final_norm=final_norm),
        grid=(t // tm, nf),
        in_specs=[
            pl.BlockSpec((HALO, d), lambda i, f: (jnp.maximum(i * hb - 1, 0), 0)),
            pl.BlockSpec((tm, d), lambda i, f: (i, 0)),
            pl.BlockSpec((HALO, d), lambda i, f: (jnp.minimum((i + 1) * hb, t // HALO - 1), 0)),
            pl.BlockSpec((1, d), lambda i, f: (0, 0)),
            pl.BlockSpec((1, 1, d), lambda i, f: (brow(i, f), 0, 4)),
            pl.BlockSpec((1, 1, d), lambda i, f: (brow(i, f), 0, 3)),
            pl.BlockSpec((1, 1, d), lambda i, f: (brow(i, f), 0, 5)),
            pl.BlockSpec((d, tf), lambda i, f: (0, f)),
            pl.BlockSpec((d, tf), lambda i, f: (0, nf + f)),
            pl.BlockSpec((3, tf), lambda i, f: (0, f)),
            pl.BlockSpec((1, tf), lambda i, f: (0, f)),
            pl.BlockSpec((tf, d), lambda i, f: (f, 0)),
            pl.BlockSpec((1, d), lambda i, f: (0, 0)),
        ],
        out_specs=pl.BlockSpec((tm, d), lambda i, f: (i, 0)),
        out_shape=jax.ShapeDtypeStruct((t, d), F32),
        scratch_shapes=[pltpu.VMEM((tm + 2 * HALO, d), BF16)],
        compiler_params=_cparams(("parallel", "arbitrary"), 56),
        name="conv_glu_mlp",
    )(x2, x2, x2, g, mod, mod, mod, w_up, w_up, conv_w, conv_b, w_down, g_final)


def _stack_heads(q, width=HEAD_DIM):
    n = q.shape[1] // width
    return jnp.concatenate([q[:, i * width:(i + 1) * width] for i in range(n)], axis=0)


def _unstack_heads(o, n):
    tq = o.shape[0] // n
    return jnp.concatenate([o[i * tq:(i + 1) * tq] for i in range(n)], axis=1)


def _rel_iota(tq, tk):
    return lax.broadcasted_iota(jnp.int32, (tq, tk), 1) - lax.broadcasted_iota(jnp.int32, (tq, tk), 0)


def _attn_specs(seq, tq, k_col0, v_col0, kv_width, kv_per_step):
    nq = seq // tq
    qw, kw = kv_per_step * 4 * HEAD_DIM, kv_per_step * kv_width
    q_row = lambda b, qi: b * nq + jnp.minimum(qi, nq - 1)
    o_row = lambda b, qi: b * nq + jnp.maximum(qi - 1, 0)
    return [
        pl.BlockSpec((tq, qw), lambda b, kh, qi: (q_row(b, qi), kh)),
        pl.BlockSpec((seq, kw), lambda b, kh, qi: (b, k_col0 // kw + kh)),
        pl.BlockSpec((seq, kw), lambda b, kh, qi: (b, v_col0 // kw + kh)),
    ], pl.BlockSpec((tq, qw), lambda b, kh, qi: (o_row(b, qi), kh))


_SMEM = pl.BlockSpec(memory_space=pltpu.SMEM)
_ATTN_SEM = ("parallel", "parallel", "arbitrary")


def _with_ones(v):
    return jnp.concatenate([v, jnp.ones(v.shape, v.dtype)], axis=1)


def _pipeline_init(qi, s_ref, macc_ref):
    @pl.when(qi == 0)
    def _():
        s_ref[...] = jnp.zeros(s_ref.shape, F32)
        macc_ref[...] = jnp.zeros(macc_ref.shape, F32)


def _pipeline_step(score_fn, value_fn, s_ref, macc_ref, finish, *, v_width, tk, mxu_row_sums):
    nck, m_rows, _ = s_ref.shape
    m_prev = jnp.max(macc_ref[...], axis=-1, keepdims=True)

    def body(c, carry):
        macc, lacc, acc = carry
        x = s_ref[c] - m_prev
        s = score_fn(c)
        s_ref[c] = s
        for j in range(tk // 128):
            macc = jnp.maximum(macc, s[:, j * 128:(j + 1) * 128])
        v = value_fn(c)
        if mxu_row_sums:
            acc = acc + _dot(jnp.exp2(x.astype(BF16)), _with_ones(v))
        else:
            p = jnp.exp2(x)
            for j in range(tk // 128):
                lacc = lacc + p[:, j * 128:(j + 1) * 128]
            acc = acc + _dot(p.astype(BF16), v)
        return macc, lacc, acc

    acc_width = 2 * v_width if mxu_row_sums else v_width
    init = (jnp.full((m_rows, 128), NEG, F32), jnp.zeros((m_rows, 128), F32),
            jnp.zeros((m_rows, acc_width), F32))
    macc, lacc, acc = lax.fori_loop(0, nck, body, init, unroll=True)
    macc_ref[...] = macc
    if mxu_row_sums:
        finish(acc[:, :v_width], acc[:, v_width:v_width + 1])
    else:
        finish(acc, jnp.sum(lacc, axis=-1, keepdims=True))


def _pipelined_attention(name, kernel_fn, operands, in_specs, out_spec, batch, seq, tq, tk, m_rows, kv_per_step,
                         extra_scratch=()):
    nck = seq // tk
    return pl.pallas_call(
        kernel_fn,
        grid=(batch, N_KV_HEADS // kv_per_step, seq // tq + 1),
        in_specs=in_specs,
        out_specs=out_spec,
        out_shape=jax.ShapeDtypeStruct((batch * seq, D_MODEL), BF16),
        scratch_shapes=[pltpu.VMEM((kv_per_step, nck, m_rows, tk), F32),
                        pltpu.VMEM((kv_per_step, m_rows, 128), F32), *extra_scratch],
        compiler_params=_cparams(_ATTN_SEM, 52),
        name=name,
    )(*operands)


def _prep_a_kernel(qkv_ref, cos_ref, sin_ref, w_ref, o_ref):
    tp = qkv_ref.shape[0]
    cos, sin = cos_ref[...], sin_ref[...]
    lane = lax.broadcasted_iota(jnp.int32, (tp, HEAD_DIM), 1)
    low_half = (lane % (HEAD_DIM // 2)) < (HEAD_DIM // 4)
    for h in range(N_HEADS + N_KV_HEADS):
        cols = slice(h * HEAD_DIM, (h + 1) * HEAD_DIM)
        x = qkv_ref[:, cols].astype(F32)
        is_q = h < N_HEADS
        y = x * lax.rsqrt(jnp.mean(x * x, axis=-1, keepdims=True) + EPS) * (w_ref[0:1, :] if is_q else w_ref[1:2, :])
        partner = jnp.where(low_half, pltpu.roll(y, 3 * HEAD_DIM // 4, 1), pltpu.roll(y, HEAD_DIM // 4, 1))
        r = y * cos + partner * sin
        if is_q:
            r = r * Q_SCALE_LOG2
        o_ref[:, cols] = r.astype(BF16)


def _prep_a(qkv, cos_t, sin_t, qk_norm, seq, tp=512):
    t = qkv.shape[0]
    tp = min(tp, seq)
    width = (N_HEADS + N_KV_HEADS) * HEAD_DIM
    return pl.pallas_call(
        _prep_a_kernel,
        grid=(t // tp,),
        in_specs=[
            pl.BlockSpec((tp, width), lambda i: (i, 0)),
            pl.BlockSpec((tp, HEAD_DIM), lambda i: (i % (seq // tp), 0)),
            pl.BlockSpec((tp, HEAD_DIM), lambda i: (i % (seq // tp), 0)),
            pl.BlockSpec((2, HEAD_DIM), lambda i: (0, 0)),
        ],
        out_specs=pl.BlockSpec((tp, width), lambda i: (i, 0)),
        out_shape=jax.ShapeDtypeStruct((t, width), BF16),
        compiler_params=_cparams(("parallel",), 32),
        name="rope_qk_norm",
    )(qkv, cos_t, sin_t, qk_norm)


def _attn_a_kernel(q_ref, k_ref, v_ref, o_ref, s_ref, macc_ref, *, tk):
    _pipeline_init(pl.program_id(2), s_ref, macc_ref)
    for kk in range(s_ref.shape[0]):
        kcols = slice(kk * HEAD_DIM, (kk + 1) * HEAD_DIM)
        qcols = slice(kk * GQA_GROUP * HEAD_DIM, (kk + 1) * GQA_GROUP * HEAD_DIM)
        qs = _stack_heads(q_ref[:, qcols])

        def score(c, qs=qs, kcols=kcols):
            return _dot_nt(qs, k_ref[pl.ds(pl.multiple_of(c * tk, tk), tk), kcols])

        def value(c, kcols=kcols):
            return v_ref[pl.ds(pl.multiple_of(c * tk, tk), tk), kcols]

        def finish(acc, l, qcols=qcols):
            o_ref[:, qcols] = _unstack_heads(acc / l, GQA_GROUP).astype(o_ref.dtype)

        _pipeline_step(score, value, s_ref.at[kk], macc_ref.at[kk], finish, v_width=HEAD_DIM, tk=tk,
                       mxu_row_sums=True)


def _attn_a(qk_rot, qkv, batch, seq, tq=128, tk=1024, kv_per_step=2):
    tk = min(tk, seq)
    in_specs, out_spec = _attn_specs(seq, tq, N_HEADS * HEAD_DIM, (N_HEADS + N_KV_HEADS) * HEAD_DIM, HEAD_DIM,
                                     kv_per_step)
    return _pipelined_attention("attn_axial", functools.partial(_attn_a_kernel, tk=tk), (qk_rot, qk_rot, qkv),
                                in_specs, out_spec, batch, seq, tq, tk, GQA_GROUP * tq, kv_per_step)


KV_PER_STEP = 4


def _window_kernel(slopes_ref, sink_ref, q_ref, k_ref, v_ref, o_ref, lse_ref, s_ref, macc_ref,
                   *, tq, band, window, nq, seq_c, dil, use_sink, n_steps):
    n = pl.program_id(0)
    groups = N_KV_HEADS // KV_PER_STEP

    def group_and_block(step):
        return (step // nq) % groups, step % nq

    def band_start(qi):
        if band == seq_c:
            return 0
        return pl.multiple_of(jnp.clip(qi * tq - window, 0, seq_c - band), window)

    @pl.when(n == 0)
    def _():
        s_ref[...] = jnp.zeros(s_ref.shape, F32)
        macc_ref[...] = jnp.zeros(macc_ref.shape, F32)

    kp, qi = group_and_block(jnp.minimum(n, n_steps - 1))
    kpp, qip = group_and_block(jnp.maximum(n - 1, 0))
    ws, wsp = band_start(qi), band_start(qip)
    rel = jnp.abs(_rel_iota(tq, band) + (ws - qi * tq))
    in_window = rel <= window
    dist = rel.astype(F32)
    for kk in range(KV_PER_STEP):
        kcols = slice(kk * HEAD_DIM, (kk + 1) * HEAD_DIM)
        qcols = slice(kk * GQA_GROUP * HEAD_DIM, (kk + 1) * GQA_GROUP * HEAD_DIM)
        head0 = (kpp * KV_PER_STEP + kk) * GQA_GROUP
        m = jnp.max(macc_ref[kk], axis=-1, keepdims=True)
        if use_sink:
            sink = jnp.concatenate(
                [jnp.full((tq, 1), sink_ref[head0 + g] * LOG2E, F32) for g in range(GQA_GROUP)], axis=0)
            m = jnp.maximum(m, sink)
        x = s_ref[kk] - m
        head0 = (kp * KV_PER_STEP + kk) * GQA_GROUP
        bias = jnp.concatenate(
            [jnp.where(in_window, -(slopes_ref[head0 + g] * (dil * LOG2E)) * dist, NEG)
             for g in range(GQA_GROUP)], axis=0)
        s = _dot_nt(_stack_heads(q_ref[:, qcols]), k_ref[pl.ds(ws, band), kcols]) + bias
        s_ref[kk] = s
        macc = s[:, 0:128]
        for j in range(1, band // 128):
            macc = jnp.maximum(macc, s[:, j * 128:(j + 1) * 128])
        macc_ref[kk] = macc
        ol = _dot(jnp.exp2(x.astype(BF16)), _with_ones(v_ref[pl.ds(wsp, band), kcols]))
        l = ol[:, HEAD_DIM:HEAD_DIM + 1]
        if use_sink:
            l = l + jnp.exp2(sink - m)
        o_ref[:, qcols] = _unstack_heads(ol[:, :HEAD_DIM] / l, GQA_GROUP).astype(o_ref.dtype)
        lse = m + jnp.log2(l)
        lse_ref[:, kcols] = jnp.concatenate(
            [jnp.broadcast_to(lse[g * tq:(g + 1) * tq], (tq, 128 // GQA_GROUP)) for g in range(GQA_GROUP)], axis=1)


def _attn_window(qkv_c, slopes, sink, batch, seq, *, window, use_sink):
    dil, t_c, width = qkv_c.shape
    seq_c = seq // dil
    tq = min(256, seq_c)
    band = min(tq + 2 * window, seq_c)
    nq = seq_c // tq
    assert band % 128 == 0 and seq_c % tq == 0, "bands are reduced in 128-lane blocks"
    groups = N_KV_HEADS // KV_PER_STEP
    n_steps = batch * dil * groups * nq
    qw = KV_PER_STEP * GQA_GROUP * HEAD_DIM
    kw = KV_PER_STEP * HEAD_DIM

    def split(step):
        qi = step % nq
        rest = step // nq
        kp = rest % groups
        rest = rest // groups
        return rest // dil, rest % dil, kp, qi

    def q_map(n):
        b, r, kp, qi = split(jnp.minimum(n, n_steps - 1))
        return r, b * nq + qi, kp

    def k_map(n):
        b, r, kp, _ = split(jnp.minimum(n, n_steps - 1))
        return r, b, N_HEADS // KV_PER_STEP + kp

    def v_map(n):
        b, r, kp, _ = split(jnp.maximum(n - 1, 0))
        return r, b, (N_HEADS + N_KV_HEADS) // KV_PER_STEP + kp

    def o_map(n):
        b, r, kp, qi = split(jnp.maximum(n - 1, 0))
        return r, b * nq + qi, kp

    return pl.pallas_call(
        functools.partial(_window_kernel, tq=tq, band=band, window=window, nq=nq, seq_c=seq_c, dil=dil,
                          use_sink=use_sink, n_steps=n_steps),
        grid=(n_steps + 1,),
        in_specs=[_SMEM, _SMEM,
                  pl.BlockSpec((None, tq, qw), q_map),
                  pl.BlockSpec((None, seq_c, kw), k_map),
                  pl.BlockSpec((None, seq_c, kw), v_map)],
        out_specs=[pl.BlockSpec((None, tq, qw), o_map), pl.BlockSpec((None, tq, kw), o_map)],
        out_shape=[jax.ShapeDtypeStruct((dil, t_c, D_MODEL), BF16),
                   jax.ShapeDtypeStruct((dil, t_c, N_KV_HEADS * 128), F32)],
        scratch_shapes=[pltpu.VMEM((KV_PER_STEP, GQA_GROUP * tq, band), F32),
                        pltpu.VMEM((KV_PER_STEP, GQA_GROUP * tq, 128), F32)],
        compiler_params=_cparams(("arbitrary",), 40),
        name=f"attn_window_d{dil}",
    )(slopes, sink, qkv_c, qkv_c, qkv_c)


def _attn_c_kernel(slopes_ref, q_ref, k_ref, v_ref, lam_ref, subln_ref, o_ref, s_ref, macc_ref, dist_ref,
                   *, tq, tk, seq, lam_init):
    kp, qi = pl.program_id(1), pl.program_id(2)
    start = jnp.minimum(qi, seq // tq - 1) * tq
    h = HEAD_DIM
    n_blk = seq // 128
    kv_per_step = s_ref.shape[0]

    @pl.when(qi == 0)
    def _():
        lane_minus_row = _rel_iota(128, 128)
        for u in range(2 * n_blk - 1):
            dist_ref[u] = jnp.abs(lane_minus_row + (u - (n_blk - 1)) * 128).astype(F32)

    _pipeline_init(qi, s_ref, macc_ref)
    lp = lam_ref[...]
    lam = (jnp.exp(jnp.sum(lp[0:1] * lp[1:2], axis=-1, keepdims=True))
           - jnp.exp(jnp.sum(lp[2:3] * lp[3:4], axis=-1, keepdims=True)) + lam_init)

    for kk in range(kv_per_step):
        q = q_ref[:, kk * 4 * h:(kk + 1) * 4 * h]
        qs0 = jnp.concatenate([q[:, 0:h], q[:, 2 * h:3 * h]], axis=0)
        qs1 = jnp.concatenate([q[:, h:2 * h], q[:, 3 * h:4 * h]], axis=0)
        head0 = (kp * kv_per_step + kk) * 2
        slopes = [slopes_ref[head0 + g] for g in range(2)]
        kvcols = slice(kk * 2 * h, (kk + 1) * 2 * h)
        ocols = slice(kk * 4 * h, (kk + 1) * 4 * h)

        def score(c, qs0=qs0, qs1=qs1, slopes=slopes, kvcols=kvcols):
            off = pl.multiple_of(c * tk, tk)
            k = k_ref[pl.ds(off, tk), kvcols]
            u0 = (off - start) // 128 + (n_blk - 1)
            dist = jnp.concatenate(
                [jnp.concatenate([dist_ref[u0 + j - rb] for j in range(tk // 128)], axis=1)
                 for rb in range(tq // 128)], axis=0)
            bias = jnp.concatenate([slopes[0] * dist, slopes[1] * dist], axis=0)
            return jnp.concatenate([_dot_nt(qs0, k[:, :h]) - bias, _dot_nt(qs1, k[:, h:]) - bias], axis=0)

        def value(c, kvcols=kvcols):
            return v_ref[pl.ds(pl.multiple_of(c * tk, tk), tk), kvcols]

        def finish(acc, l, ocols=ocols):
            a = acc / l
            o = a[:2 * tq] - lam * a[2 * tq:]
            y = o * lax.rsqrt(jnp.mean(o * o, axis=-1, keepdims=True) + EPS) * subln_ref[...] * (1.0 - lam_init)
            o_ref[:, ocols] = _unstack_heads(y, 2).astype(o_ref.dtype)

        _pipeline_step(score, value, s_ref.at[kk], macc_ref.at[kk], finish, v_width=2 * h, tk=tk,
                       mxu_row_sums=False)


def _attn_c(qkv, slopes_log2, lam_params, subln, batch, seq, lam_init, tq=128, tk=1024, kv_per_step=2):
    assert tq % 128 == 0, "the distance table holds 128 x 128 tiles"
    tk = min(tk, seq)
    in_specs, out_spec = _attn_specs(seq, tq, D_MODEL, D_MODEL + N_KV_HEADS * 2 * HEAD_DIM, 2 * HEAD_DIM,
                                     kv_per_step)
    in_specs = [_SMEM] + in_specs + [
        pl.BlockSpec((4, HEAD_DIM), lambda b, kh, qi: (0, 0)),
        pl.BlockSpec((1, 2 * HEAD_DIM), lambda b, kh, qi: (0, 0)),
    ]
    return _pipelined_attention(
        "attn_diff", functools.partial(_attn_c_kernel, tq=tq, tk=tk, seq=seq, lam_init=lam_init),
        (slopes_log2, qkv, qkv, qkv, lam_params, subln), in_specs, out_spec, batch, seq, tq, tk, 4 * tq,
        kv_per_step, extra_scratch=[pltpu.VMEM((2 * (seq // 128) - 1, 128, 128), F32)])


def _combine_kernel(*refs):
    nb = len(DILATIONS)
    o_refs, l_refs, out_ref, scratch = refs[:nb], refs[nb:2 * nb], refs[2 * nb], refs[2 * nb + 1:]
    tc = out_ref.shape[0]
    o_nat, l_nat = [], []
    for b, (_, dil) in enumerate(DILATIONS):
        if dil == 1:
            o_nat.append(o_refs[b][0].astype(F32))
            l_nat.append(l_refs[b][0])
            continue
        so, sl = scratch[2 * b], scratch[2 * b + 1]
        rows = tc // dil
        for r in range(dil):
            o_r, l_r = o_refs[b][r].astype(F32), l_refs[b][r]
            for jb in range(so.shape[0]):
                so[jb, pl.ds(r, rows, stride=dil), :] = o_r[:, jb * 128:(jb + 1) * 128]
            for jb in range(sl.shape[0]):
                sl[jb, pl.ds(r, rows, stride=dil), :] = l_r[:, jb * 128:(jb + 1) * 128]
        o_nat.append(jnp.concatenate([so[jb] for jb in range(so.shape[0])], axis=1))
        l_nat.append(jnp.concatenate([sl[jb] for jb in range(sl.shape[0])], axis=1))
    lanes = 128 // GQA_GROUP
    for h in range(N_HEADS):
        e = [l[:, h * lanes:h * lanes + 1] for l in l_nat]
        m = functools.reduce(jnp.maximum, e)
        w = [jnp.exp2(x - m) for x in e]
        cols = slice(h * HEAD_DIM, (h + 1) * HEAD_DIM)
        mix = sum(wi * o[:, cols] for wi, o in zip(w, o_nat))
        out_ref[:, cols] = (mix / sum(w)).astype(out_ref.dtype)


def _attn_d(qkv_classes, slopes, batch, seq, tc=256):
    t = qkv_classes[0].shape[1]
    dummy_sink = jnp.zeros((N_HEADS,), F32)
    branches = [_attn_window(qkv_c, slopes, dummy_sink, batch, seq, window=(win // 2) // dil, use_sink=False)
                for qkv_c, (win, dil) in zip(qkv_classes, DILATIONS)]
    lw = N_KV_HEADS * 128
    o_specs = [pl.BlockSpec((dil, tc // dil, D_MODEL), lambda i: (0, i, 0)) for _, dil in DILATIONS]
    l_specs = [pl.BlockSpec((dil, tc // dil, lw), lambda i: (0, i, 0)) for _, dil in DILATIONS]
    scratch = []
    for _ in DILATIONS:
        scratch += [pltpu.VMEM((D_MODEL // 128, tc, 128), F32), pltpu.VMEM((lw // 128, tc, 128), F32)]
    return pl.pallas_call(
        _combine_kernel,
        grid=(t // tc,),
        in_specs=o_specs + l_specs,
        out_specs=pl.BlockSpec((tc, D_MODEL), lambda i: (i, 0)),
        out_shape=jax.ShapeDtypeStruct((t, D_MODEL), BF16),
        scratch_shapes=scratch,
        compiler_params=_cparams(("parallel",), 40),
        name="dilated_combine",
    )(*[b[0] for b in branches], *[b[1] for b in branches])


def _alibi_slopes(n):
    return 2.0 ** (-8.0 * jnp.arange(1, n + 1, dtype=F32) / n)


def _lambda_init(layer):
    return 0.8 - 0.6 * math.exp(-0.3 * layer)


def _rope_tables(seq):
    rows = seq // GRID_W
    row = jnp.repeat(jnp.arange(rows, dtype=F32), GRID_W)
    col = jnp.tile(jnp.arange(GRID_W, dtype=F32), rows)
    n_freq = HEAD_DIM // 4
    inv_freq = ROPE_THETA ** (-jnp.arange(n_freq, dtype=F32) / n_freq)
    ang = jnp.stack([row, col], axis=-1)[..., None] * inv_freq
    cos, sin = jnp.cos(ang), jnp.sin(ang)
    cos_t = jnp.concatenate([cos[:, 0], cos[:, 0], cos[:, 1], cos[:, 1]], axis=-1)
    sin_t = jnp.concatenate([-sin[:, 0], sin[:, 0], -sin[:, 1], sin[:, 1]], axis=-1)
    return cos_t, sin_t


def _q_colscale(scale, n_q_cols, n_cols):
    return jnp.concatenate([jnp.full((1, n_q_cols), scale, F32), jnp.ones((1, n_cols - n_q_cols), F32)], axis=1)


def _prepare_weights(p):
    pad = D_FF_PAD - D_FF
    w_up = p['ffn_w_up']
    w_up = jnp.concatenate([jnp.pad(w_up[..., :D_FF], ((0, 0), (0, 0), (0, pad))),
                            jnp.pad(w_up[..., D_FF:], ((0, 0), (0, 0), (0, pad)))], axis=-1).astype(BF16)
    return dict(
        a_wqkv=p['a_wqkv'].astype(BF16), a_wo=p['a_wo'].astype(BF16),
        b_wqkv=p['b_wqkv'].astype(BF16), b_wo=p['b_wo'].astype(BF16),
        c_wqkv=p['c_wqkv'].astype(BF16), c_wo=p['c_wo'].astype(BF16),
        d_wqkv=p['d_wqkv'].astype(BF16), d_wo=p['d_wo'].astype(BF16),
        ffn_w_up=w_up,
        ffn_w_down=jnp.pad(p['ffn_w_down'], ((0, 0), (0, pad), (0, 0))).astype(BF16),
        ffn_conv_w=jnp.pad(p['ffn_conv_w'], ((0, 0), (0, 0), (0, pad))),
        ffn_conv_b=jnp.pad(p['ffn_conv_b'], ((0, 0), (0, pad)))[:, None, :],
    )


def _trunk(x, mod_all, row0, p, w):
    batch, seq, d = x.shape
    x2 = x.reshape(batch * seq, d)
    slopes16, slopes8 = _alibi_slopes(N_HEADS), _alibi_slopes(DIFF_HEADS)
    n_qkv = (N_HEADS + 2 * N_KV_HEADS) * HEAD_DIM
    cs_plain = jnp.ones((1, n_qkv), F32)
    cs_q_log2 = _q_colscale(Q_SCALE_LOG2, N_HEADS * HEAD_DIM, n_qkv)
    cs_q_diff = _q_colscale(Q_SCALE_LOG2, DIFF_HEADS * 2 * HEAD_DIM, 2 * D_MODEL)
    for i in range(DEPTH):
        mod = mod_all[i][:, None, :]
        g_attn = p['norm_attn'][i][None, :]
        m, j = i % 4, i // 4
        if m == 0:
            qkv = _norm_matmul(x2, g_attn, mod, row0, seq, 1, 0, w['a_wqkv'][j], cs_plain)
            cos_t, sin_t = _rope_tables(seq)
            qk_norm = jnp.stack([p['a_q_norm'][j], p['a_k_norm'][j]])
            o = _attn_a(_prep_a(qkv, cos_t, sin_t, qk_norm, seq), qkv, batch, seq)
            wo = w['a_wo'][j]
        elif m == 1:
            qkv = _norm_matmul(x2, g_attn, mod, row0, seq, 1, 0, w['b_wqkv'][j], cs_q_log2)
            o, _ = _attn_window(qkv[None], slopes16, p['b_sink'][j].astype(F32), batch, seq, window=WINDOW,
                                use_sink=True)
            o = o[0]
            wo = w['b_wo'][j]
        elif m == 2:
            qkv = _norm_matmul(x2, g_attn, mod, row0, seq, 1, 0, w['c_wqkv'][j], cs_q_diff)
            o = _attn_c(qkv, slopes8 * LOG2E, p['c_lambda'][j].astype(F32), p['c_subln'][j][None, :],
                        batch, seq, _lambda_init(i))
            wo = w['c_wo'][j]
        else:
            dils = tuple(dil for _, dil in DILATIONS if dil > 1)
            qkv, *qkv_classes = _norm_matmul(x2, g_attn, mod, row0, seq, 1, 0, w['d_wqkv'][j], cs_q_log2, dils=dils)
            o = _attn_d([qkv[None], *qkv_classes], slopes16, batch, seq)
            wo = w['d_wo'][j]
        x2 = _out_residual(o, wo, x2, mod, row0, seq, 2)
        x2 = _ffn(x2, p['norm_ffn'][i][None, :], mod, row0, seq, w['ffn_w_up'][i], w['ffn_conv_w'][i],
                  w['ffn_conv_b'][i], w['ffn_w_down'][i], p['norm_final'][None, :], final_norm=(i == DEPTH - 1))
    return x2.reshape(batch, seq, d)


def kernel(x_prompt, x_sample, c_prompt, c_sample, norm_attn, norm_ffn, w_ada, b_ada, a_wqkv, a_q_norm, a_k_norm, a_wo, b_wqkv, b_sink, b_wo, c_wqkv, c_lambda, c_subln, c_wo, d_wqkv, d_wo, ffn_w_up, ffn_conv_w, ffn_conv_b, ffn_w_down, norm_final):
    p = dict(norm_attn=norm_attn, norm_ffn=norm_ffn, a_q_norm=a_q_norm, a_k_norm=a_k_norm, b_sink=b_sink,
             c_lambda=c_lambda, c_subln=c_subln, norm_final=norm_final,
             a_wqkv=a_wqkv, a_wo=a_wo, b_wqkv=b_wqkv, b_wo=b_wo, c_wqkv=c_wqkv, c_wo=c_wo,
             d_wqkv=d_wqkv, d_wo=d_wo, ffn_w_up=ffn_w_up, ffn_conv_w=ffn_conv_w, ffn_conv_b=ffn_conv_b,
             ffn_w_down=ffn_w_down)
    w = _prepare_weights(p)
    n_prompt, n_sample = c_prompt.shape[0], c_sample.shape[0]
    rows = -(-(n_prompt + n_sample) // 8) * 8
    c_all = jnp.concatenate([c_prompt, c_sample, jnp.zeros((rows - n_prompt - n_sample, D_MODEL), F32)], axis=0)
    mod_all = _modulation(c_all, w_ada, b_ada)
    y_prompt = _trunk(x_prompt, mod_all, 0, p, w)
    y_sample = _trunk(x_sample, mod_all, n_prompt, p, w)
    return (y_prompt, y_sample)
```

```python
import functools
import math

import jax
import jax.numpy as jnp
from jax import lax
from jax.experimental import pallas as pl
from jax.experimental.pallas import tpu as pltpu

F32 = jnp.float32
BF16 = jnp.bfloat16

D_MODEL = 2048
HEAD_DIM = 128
N_HEADS = 16
N_KV_HEADS = 4
GQA_GROUP = 4
DIFF_HEADS = 8
D_FF = 5504
FF_TILE = 512
D_FF_PAD = 5632
N_MOD = 6
DEPTH = 4
GRID_W = 64
ROPE_THETA = 10000.0
EPS = 1e-6
NEG = -1e30
WINDOW = 128
DILATIONS = ((128, 1), (512, 4), (2048, 16))
Q_SCALE = HEAD_DIM ** -0.5
LOG2E = math.log2(math.e)
Q_SCALE_LOG2 = Q_SCALE * LOG2E
HALO = 16
MIB = 1024 * 1024


def _cparams(sem, vmem_mib):
    return pltpu.CompilerParams(dimension_semantics=sem, vmem_limit_bytes=vmem_mib * MIB)


def _dot(a, b):
    return jnp.dot(a, b, preferred_element_type=F32)


def _dot_nt(a, b):
    return lax.dot_general(a, b, (((1,), (1,)), ((), ())), preferred_element_type=F32)


def _norm_mod(x, g, sc, sh):
    y = x * lax.rsqrt(jnp.mean(x * x, axis=-1, keepdims=True) + EPS) * g
    return y * (1.0 + sc) + sh


def _mod_kernel(c_ref, w_ref, b_ref, o_ref):
    c = c_ref[...]
    a = (c / (1.0 + jnp.exp(-c))).astype(BF16)
    o_ref[0] = _dot(a, w_ref[0].astype(BF16)) + b_ref[0]


def _modulation(c_all, w_ada, b_ada, tn=1024):
    nb, d = c_all.shape
    depth, _, n = w_ada.shape
    return pl.pallas_call(
        _mod_kernel,
        grid=(depth, n // tn),
        in_specs=[
            pl.BlockSpec((nb, d), lambda l, j: (0, 0)),
            pl.BlockSpec((1, d, tn), lambda l, j: (l, 0, j)),
            pl.BlockSpec((1, 1, tn), lambda l, j: (l, 0, j)),
        ],
        out_specs=pl.BlockSpec((1, nb, tn), lambda l, j: (l, 0, j)),
        out_shape=jax.ShapeDtypeStruct((depth, nb, n), F32),
        compiler_params=_cparams(("parallel", "parallel"), 40),
        name="adaln_modulation",
    )(c_all, w_ada, b_ada.reshape(depth, 1, n))


def _norm_matmul_kernel(x_ref, g_ref, sc_ref, sh_ref, w_ref, cs_ref, *refs, dils):
    o_ref, class_refs, h_ref = refs[0], refs[1:1 + len(dils)], refs[1 + len(dils)]

    def project(h):
        y = _dot(h, w_ref[...]) * cs_ref[...]
        o_ref[...] = y.astype(o_ref.dtype)
        if dils:
            y_ref = refs[-1]
            nblk, rows, _ = y_ref.shape
            for jb in range(nblk):
                y_ref[jb] = y[:, jb * 128:(jb + 1) * 128]
            for dil, c_ref in zip(dils, class_refs):
                for r in range(dil):
                    c_ref[r] = jnp.concatenate(
                        [y_ref[jb, pl.ds(r, rows // dil, stride=dil), :] for jb in range(nblk)],
                        axis=1).astype(c_ref.dtype)

    @pl.when(pl.program_id(1) == 0)
    def _():
        h = _norm_mod(x_ref[...], g_ref[...], sc_ref[0], sh_ref[0]).astype(BF16)
        h_ref[...] = h
        project(h)

    @pl.when(pl.program_id(1) > 0)
    def _():
        project(h_ref[...])


def _norm_matmul(x2, g, mod, row0, seq, k_sc, k_sh, w, colscale, tm=1024, tn=1024, dils=()):
    t, d = x2.shape
    n = w.shape[1]
    tm = min(tm, seq)
    if dils:
        tn //= 2
    brow = lambda i, j: row0 + (i * tm) // seq
    out_specs = [pl.BlockSpec((tm, tn), lambda i, j: (i, j))]
    out_shape = [jax.ShapeDtypeStruct((t, n), BF16)]
    scratch = [pltpu.VMEM((tm, d), BF16)]
    for dil in dils:
        out_specs.append(pl.BlockSpec((dil, tm // dil, tn), lambda i, j: (0, i, j)))
        out_shape.append(jax.ShapeDtypeStruct((dil, t // dil, n), BF16))
    if dils:
        scratch.append(pltpu.VMEM((tn // 128, tm, 128), F32))
    outs = pl.pallas_call(
        functools.partial(_norm_matmul_kernel, dils=tuple(dils)),
        grid=(t // tm, n // tn),
        in_specs=[
            pl.BlockSpec((tm, d), lambda i, j: (i, 0)),
            pl.BlockSpec((1, d), lambda i, j: (0, 0)),
            pl.BlockSpec((1, 1, d), lambda i, j: (brow(i, j), 0, k_sc)),
            pl.BlockSpec((1, 1, d), lambda i, j: (brow(i, j), 0, k_sh)),
            pl.BlockSpec((d, tn), lambda i, j: (0, j)),
            pl.BlockSpec((1, tn), lambda i, j: (0, j)),
        ],
        out_specs=out_specs,
        out_shape=out_shape,
        scratch_shapes=scratch,
        compiler_params=_cparams(("parallel", "arbitrary"), 52),
        name="norm_qkv_proj",
    )(x2, g, mod, mod, w, colscale)
    return outs if dils else outs[0]


def _out_res_kernel(o_ref, w_ref, x_ref, gate_ref, out_ref):
    out_ref[...] = x_ref[...] + gate_ref[0] * _dot(o_ref[...], w_ref[...])


def _out_residual(o2, w, x2, mod, row0, seq, k_gate, tm=1024, tn=1024):
    t, d = x2.shape
    kdim = o2.shape[1]
    tm = min(tm, seq)
    return pl.pallas_call(
        _out_res_kernel,
        grid=(t // tm, d // tn),
        in_specs=[
            pl.BlockSpec((tm, kdim), lambda i, j: (i, 0)),
            pl.BlockSpec((kdim, tn), lambda i, j: (0, j)),
            pl.BlockSpec((tm, tn), lambda i, j: (i, j)),
            pl.BlockSpec((1, 1, tn), lambda i, j: (row0 + (i * tm) // seq, 0, k_gate * (d // tn) + j)),
        ],
        out_specs=pl.BlockSpec((tm, tn), lambda i, j: (i, j)),
        out_shape=jax.ShapeDtypeStruct((t, d), F32),
        compiler_params=_cparams(("parallel", "parallel"), 48),
        name="out_proj_residual",
    )(o2, w, x2, mod)


def _ffn_kernel(xp_ref, x_ref, xn_ref, g_ref, sc_ref, sh_ref, gate_ref, wa_ref, wb_ref,
                cw_ref, cb_ref, wd_ref, gf_ref, out_ref, h_ref, *, tm, seq, final_norm):
    i = pl.program_id(0)
    f = pl.program_id(1)
    n_ext = tm + 2 * HALO

    def partial_down(h_ext):
        a_ext = _dot(h_ext, wa_ref[...])
        b = _dot(h_ext[HALO:HALO + tm], wb_ref[...])
        a_prev = pltpu.roll(a_ext, 1, 0)[HALO:HALO + tm]
        a_next = pltpu.roll(a_ext, n_ext - 1, 0)[HALO:HALO + tm]
        a_cur = a_ext[HALO:HALO + tm]
        a = cb_ref[...] + (a_prev * cw_ref[0:1, :] + a_cur * cw_ref[1:2, :] + a_next * cw_ref[2:3, :])
        gelu = 0.5 * a * (1.0 + lax.erf(a * (2.0 ** -0.5)))
        return _dot((gelu * b).astype(BF16), wd_ref[...])

    @pl.when(f == 0)
    def _():
        g, sc, sh = g_ref[...], sc_ref[0], sh_ref[0]
        at_start = (i * tm) % seq == 0
        at_end = ((i + 1) * tm) % seq == 0
        hp = jnp.where(at_start, 0.0, _norm_mod(xp_ref[...], g, sc, sh))
        hn = jnp.where(at_end, 0.0, _norm_mod(xn_ref[...], g, sc, sh))
        h_ext = jnp.concatenate([hp, _norm_mod(x_ref[...], g, sc, sh), hn], axis=0).astype(BF16)
        h_ref[...] = h_ext
        out_ref[...] = partial_down(h_ext)

    @pl.when(f > 0)
    def _():
        out_ref[...] += partial_down(h_ref[...])

    @pl.when(f == pl.num_programs(1) - 1)
    def _():
        y = x_ref[...] + gate_ref[0] * out_ref[...]
        if final_norm:
            y = y * lax.rsqrt(jnp.mean(y * y, axis=-1, keepdims=True) + EPS) * gf_ref[...]
        out_ref[...] = y


def _ffn(x2, g, mod, row0, seq, w_up, conv_w, conv_b, w_down, g_final, final_norm, tm=512, tf=FF_TILE):
    t, d = x2.shape
    dff = w_down.shape[0]
    nf = dff // tf
    tm = min(tm, seq)
    hb = tm // HALO
    brow = lambda i, f: row0 + (i * tm) // seq
    return pl.pallas_call(
        functools.partial(_ffn_kernel, tm=tm, seq=seq, final_norm=final_norm),
        grid=(t // tm, nf),
        in_specs=[
            pl.BlockSpec((HALO, d), lambda i, f: (jnp.maximum(i * hb - 1, 0), 0)),
            pl.BlockSpec((tm, d), lambda i, f: (i, 0)),
            pl.BlockSpec((HALO, d), lambda i, f: (jnp.minimum((i + 1) * hb, t // HALO - 1), 0)),
            pl.BlockSpec((1, d), lambda i, f: (0, 0)),
            pl.BlockSpec((1, 1, d), lambda i, f: (brow(i, f), 0, 4)),
            pl.BlockSpec((1, 1, d), lambda i, f: (brow(i, f), 0, 3)),
            pl.BlockSpec((1, 1, d), lambda i, f: (brow(i, f), 0, 5)),
            pl.BlockSpec((d, tf), lambda i, f: (0, f)),
            pl.BlockSpec((d, tf), lambda i, f: (0, nf + f)),
            pl.BlockSpec((3, tf), lambda i, f: (0, f)),
            pl.BlockSpec((1, tf), lambda i, f: (0, f)),
            pl.BlockSpec((tf, d), lambda i, f: (f, 0)),
            pl.BlockSpec((1, d), lambda i, f: (0, 0)),
        ],
        out_specs=pl.BlockSpec((tm, d), lambda i, f: (i, 0)),
        out_shape=jax.ShapeDtypeStruct((t, d), F32),
        scratch_shapes=[pltpu.VMEM((tm + 2 * HALO, d), BF16)],
        compiler_params=_cparams(("parallel", "arbitrary"), 56),
        name="conv_glu_mlp",
    )(x2, x2, x2, g, mod, mod, mod, w_up, w_up, conv_w, conv_b, w_down, g_final)


def _stack_heads(q, width=HEAD_DIM):
    n = q.shape[1] // width
    return jnp.concatenate([q[:, i * width:(i + 1) * width] for i in range(n)], axis=0)


def _unstack_heads(o, n):
    tq = o.shape[0] // n
    return jnp.concatenate([o[i * tq:(i + 1) * tq] for i in range(n)], axis=1)


def _rel_iota(tq, tk):
    return lax.broadcasted_iota(jnp.int32, (tq, tk), 1) - lax.broadcasted_iota(jnp.int32, (tq, tk), 0)


def _attn_specs(seq, tq, k_col0, v_col0, kv_width, kv_per_step):
    nq = seq // tq
    qw, kw = kv_per_step * 4 * HEAD_DIM, kv_per_step * kv_width
    q_row = lambda b, qi: b * nq + jnp.minimum(qi, nq - 1)
    o_row = lambda b, qi: b * nq + jnp.maximum(qi - 1, 0)
    return [
        pl.BlockSpec((tq, qw), lambda b, kh, qi: (q_row(b, qi), kh)),
        pl.BlockSpec((seq, kw), lambda b, kh, qi: (b, k_col0 // kw + kh)),
        pl.BlockSpec((seq, kw), lambda b, kh, qi: (b, v_col0 // kw + kh)),
    ], pl.BlockSpec((tq, qw), lambda b, kh, qi: (o_row(b, qi), kh))


_SMEM = pl.BlockSpec(memory_space=pltpu.SMEM)
_ATTN_SEM = ("parallel", "parallel", "arbitrary")


def _with_ones(v):
    return jnp.concatenate([v, jnp.ones(v.shape, v.dtype)], axis=1)


def _pipeline_init(qi, s_ref, macc_ref):
    @pl.when(qi == 0)
    def _():
        s_ref[...] = jnp.zeros(s_ref.shape, F32)
        macc_ref[...] = jnp.zeros(macc_ref.shape, F32)


def _pipeline_step(score_fn, value_fn, s_ref, macc_ref, finish, *, v_width, tk, mxu_row_sums):
    nck, m_rows, _ = s_ref.shape
    m_prev = jnp.max(macc_ref[...], axis=-1, keepdims=True)

    def body(c, carry):
        macc, lacc, acc = carry
        x = s_ref[c] - m_prev
        s = score_fn(c)
        s_ref[c] = s
        for j in range(tk // 128):
            macc = jnp.maximum(macc, s[:, j * 128:(j + 1) * 128])
        v = value_fn(c)
        if mxu_row_sums:
            acc = acc + _dot(jnp.exp2(x.astype(BF16)), _with_ones(v))
        else:
            p = jnp.exp2(x)
            for j in range(tk // 128):
                lacc = lacc + p[:, j * 128:(j + 1) * 128]
            acc = acc + _dot(p.astype(BF16), v)
        return macc, lacc, acc

    acc_width = 2 * v_width if mxu_row_sums else v_width
    init = (jnp.full((m_rows, 128), NEG, F32), jnp.zeros((m_rows, 128), F32),
            jnp.zeros((m_rows, acc_width), F32))
    macc, lacc, acc = lax.fori_loop(0, nck, body, init, unroll=True)
    macc_ref[...] = macc
    if mxu_row_sums:
        finish(acc[:, :v_width], acc[:, v_width:v_width + 1])
    else:
        finish(acc, jnp.sum(lacc, axis=-1, keepdims=True))


def _pipelined_attention(name, kernel_fn, operands, in_specs, out_spec, batch, seq, tq, tk, m_rows, kv_per_step,
                         extra_scratch=()):
    nck = seq // tk
    return pl.pallas_call(
        kernel_fn,
        grid=(batch, N_KV_HEADS // kv_per_step, seq // tq + 1),
        in_specs=in_specs,
        out_specs=out_spec,
        out_shape=jax.ShapeDtypeStruct((batch * seq, D_MODEL), BF16),
        scratch_shapes=[pltpu.VMEM((kv_per_step, nck, m_rows, tk), F32),
                        pltpu.VMEM((kv_per_step, m_rows, 128), F32), *extra_scratch],
        compiler_params=_cparams(_ATTN_SEM, 52),
        name=name,
    )(*operands)


def _prep_a_kernel(qkv_ref, cos_ref, sin_ref, w_ref, o_ref):
    tp = qkv_ref.shape[0]
    cos, sin = cos_ref[...], sin_ref[...]
    lane = lax.broadcasted_iota(jnp.int32, (tp, HEAD_DIM), 1)
    low_half = (lane % (HEAD_DIM // 2)) < (HEAD_DIM // 4)
    for h in range(N_HEADS + N_KV_HEADS):
        cols = slice(h * HEAD_DIM, (h + 1) * HEAD_DIM)
        x = qkv_ref[:, cols].astype(F32)
        is_q = h < N_HEADS
        y = x * lax.rsqrt(jnp.mean(x * x, axis=-1, keepdims=True) + EPS) * (w_ref[0:1, :] if is_q else w_ref[1:2, :])
        partner = jnp.where(low_half, pltpu.roll(y, 3 * HEAD_DIM // 4, 1), pltpu.roll(y, HEAD_DIM // 4, 1))
        r = y * cos + partner * sin
        if is_q:
            r = r * Q_SCALE_LOG2
        o_ref[:, cols] = r.astype(BF16)


def _prep_a(qkv, cos_t, sin_t, qk_norm, seq, tp=512):
    t = qkv.shape[0]
    tp = min(tp, seq)
    width = (N_HEADS + N_KV_HEADS) * HEAD_DIM
    return pl.pallas_call(
        _prep_a_kernel,
        grid=(t // tp,),
        in_specs=[
            pl.BlockSpec((tp, width), lambda i: (i, 0)),
            pl.BlockSpec((tp, HEAD_DIM), lambda i: (i % (seq // tp), 0)),
            pl.BlockSpec((tp, HEAD_DIM), lambda i: (i % (seq // tp), 0)),
            pl.BlockSpec((2, HEAD_DIM), lambda i: (0, 0)),
        ],
        out_specs=pl.BlockSpec((tp, width), lambda i: (i, 0)),
        out_shape=jax.ShapeDtypeStruct((t, width), BF16),
        compiler_params=_cparams(("parallel",), 32),
        name="rope_qk_norm",
    )(qkv, cos_t, sin_t, qk_norm)


def _attn_a_kernel(q_ref, k_ref, v_ref, o_ref, s_ref, macc_ref, *, tk):
    _pipeline_init(pl.program_id(2), s_ref, macc_ref)
    for kk in range(s_ref.shape[0]):
        kcols = slice(kk * HEAD_DIM, (kk + 1) * HEAD_DIM)
        qcols = slice(kk * GQA_GROUP * HEAD_DIM, (kk + 1) * GQA_GROUP * HEAD_DIM)
        qs = _stack_heads(q_ref[:, qcols])

        def score(c, qs=qs, kcols=kcols):
            return _dot_nt(qs, k_ref[pl.ds(pl.multiple_of(c * tk, tk), tk), kcols])

        def value(c, kcols=kcols):
            return v_ref[pl.ds(pl.multiple_of(c * tk, tk), tk), kcols]

        def finish(acc, l, qcols=qcols):
            o_ref[:, qcols] = _unstack_heads(acc / l, GQA_GROUP).astype(o_ref.dtype)

        _pipeline_step(score, value, s_ref.at[kk], macc_ref.at[kk], finish, v_width=HEAD_DIM, tk=tk,
                       mxu_row_sums=True)


def _attn_a(qk_rot, qkv, batch, seq, tq=128, tk=1024, kv_per_step=2):
    tk = min(tk, seq)
    in_specs, out_spec = _attn_specs(seq, tq, N_HEADS * HEAD_DIM, (N_HEADS + N_KV_HEADS) * HEAD_DIM, HEAD_DIM,
                                     kv_per_step)
    return _pipelined_attention("attn_axial", functools.partial(_attn_a_kernel, tk=tk), (qk_rot, qk_rot, qkv),
                                in_specs, out_spec, batch, seq, tq, tk, GQA_GROUP * tq, kv_per_step)


KV_PER_STEP = 4


def _window_kernel(slopes_ref, sink_ref, q_ref, k_ref, v_ref, o_ref, lse_ref, s_ref, macc_ref,
                   *, tq, band, window, nq, seq_c, dil, use_sink, n_steps):
    n = pl.program_id(0)
    groups = N_KV_HEADS // KV_PER_STEP

    def group_and_block(step):
        return (step // nq) % groups, step % nq

    def band_start(qi):
        if band == seq_c:
            return 0
        return pl.multiple_of(jnp.clip(qi * tq - window, 0, seq_c - band), window)

    @pl.when(n == 0)
    def _():
        s_ref[...] = jnp.zeros(s_ref.shape, F32)
        macc_ref[...] = jnp.zeros(macc_ref.shape, F32)

    kp, qi = group_and_block(jnp.minimum(n, n_steps - 1))
    kpp, qip = group_and_block(jnp.maximum(n - 1, 0))
    ws, wsp = band_start(qi), band_start(qip)
    rel = jnp.abs(_rel_iota(tq, band) + (ws - qi * tq))
    in_window = rel <= window
    dist = rel.astype(F32)
    for kk in range(KV_PER_STEP):
        kcols = slice(kk * HEAD_DIM, (kk + 1) * HEAD_DIM)
        qcols = slice(kk * GQA_GROUP * HEAD_DIM, (kk + 1) * GQA_GROUP * HEAD_DIM)
        head0 = (kpp * KV_PER_STEP + kk) * GQA_GROUP
        m = jnp.max(macc_ref[kk], axis=-1, keepdims=True)
        if use_sink:
            sink = jnp.concatenate(
                [jnp.full((tq, 1), sink_ref[head0 + g] * LOG2E, F32) for g in range(GQA_GROUP)], axis=0)
            m = jnp.maximum(m, sink)
        x = s_ref[kk] - m
        head0 = (kp * KV_PER_STEP + kk) * GQA_GROUP
        bias = jnp.concatenate(
            [jnp.where(in_window, -(slopes_ref[head0 + g] * (dil * LOG2E)) * dist, NEG)
             for g in range(GQA_GROUP)], axis=0)
        s = _dot_nt(_stack_heads(q_ref[:, qcols]), k_ref[pl.ds(ws, band), kcols]) + bias
        s_ref[kk] = s
        macc = s[:, 0:128]
        for j in range(1, band // 128):
            macc = jnp.maximum(macc, s[:, j * 128:(j + 1) * 128])
        macc_ref[kk] = macc
        ol = _dot(jnp.exp2(x.astype(BF16)), _with_ones(v_ref[pl.ds(wsp, band), kcols]))
        l = ol[:, HEAD_DIM:HEAD_DIM + 1]
        if use_sink:
            l = l + jnp.exp2(sink - m)
        o_ref[:, qcols] = _unstack_heads(ol[:, :HEAD_DIM] / l, GQA_GROUP).astype(o_ref.dtype)
        lse = m + jnp.log2(l)
        lse_ref[:, kcols] = jnp.concatenate(
            [jnp.broadcast_to(lse[g * tq:(g + 1) * tq], (tq, 128 // GQA_GROUP)) for g in range(GQA_GROUP)], axis=1)


def _attn_window(qkv_c, slopes, sink, batch, seq, *, window, use_sink):
    dil, t_c, width = qkv_c.shape
    seq_c = seq // dil
    tq = min(256, seq_c)
    band = min(tq + 2 * window, seq_c)
    nq = seq_c // tq
    assert band % 128 == 0 and seq_c % tq == 0, "bands are reduced in 128-lane blocks"
    groups = N_KV_HEADS // KV_PER_STEP
    n_steps = batch * dil * groups * nq
    qw = KV_PER_STEP * GQA_GROUP * HEAD_DIM
    kw = KV_PER_STEP * HEAD_DIM

    def split(step):
        qi = step % nq
        rest = step // nq
        kp = rest % groups
        rest = rest // groups
        return rest // dil, rest % dil, kp, qi

    def q_map(n):
        b, r, kp, qi = split(jnp.minimum(n, n_steps - 1))
        return r, b * nq + qi, kp

    def k_map(n):
        b, r, kp, _ = split(jnp.minimum(n, n_steps - 1))
        return r, b, N_HEADS // KV_PER_STEP + kp

    def v_map(n):
        b, r, kp, _ = split(jnp.maximum(n - 1, 0))
        return r, b, (N_HEADS + N_KV_HEADS) // KV_PER_STEP + kp

    def o_map(n):
        b, r, kp, qi = split(jnp.maximum(n - 1, 0))
        return r, b * nq + qi, kp

    return pl.pallas_call(
        functools.partial(_window_kernel, tq=tq, band=band, window=window, nq=nq, seq_c=seq_c, dil=dil,
                          use_sink=use_sink, n_steps=n_steps),
        grid=(n_steps + 1,),
        in_specs=[_SMEM, _SMEM,
                  pl.BlockSpec((None, tq, qw), q_map),
                  pl.BlockSpec((None, seq_c, kw), k_map),
                  pl.BlockSpec((None, seq_c, kw), v_map)],
        out_specs=[pl.BlockSpec((None, tq, qw), o_map), pl.BlockSpec((None, tq, kw), o_map)],
        out_shape=[jax.ShapeDtypeStruct((dil, t_c, D_MODEL), BF16),
                   jax.ShapeDtypeStruct((dil, t_c, N_KV_HEADS * 128), F32)],
        scratch_shapes=[pltpu.VMEM((KV_PER_STEP, GQA_GROUP * tq, band), F32),
                        pltpu.VMEM((KV_PER_STEP, GQA_GROUP * tq, 128), F32)],
        compiler_params=_cparams(("arbitrary",), 40),
        name=f"attn_window_d{dil}",
    )(slopes, sink, qkv_c, qkv_c, qkv_c)


def _attn_c_kernel(slopes_ref, q_ref, k_ref, v_ref, lam_ref, subln_ref, o_ref, s_ref, macc_ref, dist_ref,
                   *, tq, tk, seq, lam_init):
    kp, qi = pl.program_id(1), pl.program_id(2)
    start = jnp.minimum(qi, seq // tq - 1) * tq
    h = HEAD_DIM
    n_blk = seq // 128
    kv_per_step = s_ref.shape[0]

    @pl.when(qi == 0)
    def _():
        lane_minus_row = _rel_iota(128, 128)
        for u in range(2 * n_blk - 1):
            dist_ref[u] = jnp.abs(lane_minus_row + (u - (n_blk - 1)) * 128).astype(F32)

    _pipeline_init(qi, s_ref, macc_ref)
    lp = lam_ref[...]
    lam = (jnp.exp(jnp.sum(lp[0:1] * lp[1:2], axis=-1, keepdims=True))
           - jnp.exp(jnp.sum(lp[2:3] * lp[3:4], axis=-1, keepdims=True)) + lam_init)

    for kk in range(kv_per_step):
        q = q_ref[:, kk * 4 * h:(kk + 1) * 4 * h]
        qs0 = jnp.concatenate([q[:, 0:h], q[:, 2 * h:3 * h]], axis=0)
        qs1 = jnp.concatenate([q[:, h:2 * h], q[:, 3 * h:4 * h]], axis=0)
        head0 = (kp * kv_per_step + kk) * 2
        slopes = [slopes_ref[head0 + g] for g in range(2)]
        kvcols = slice(kk * 2 * h, (kk + 1) * 2 * h)
        ocols = slice(kk * 4 * h, (kk + 1) * 4 * h)

        def score(c, qs0=qs0, qs1=qs1, slopes=slopes, kvcols=kvcols):
            off = pl.multiple_of(c * tk, tk)
            k = k_ref[pl.ds(off, tk), kvcols]
            u0 = (off - start) // 128 + (n_blk - 1)
            dist = jnp.concatenate(
                [jnp.concatenate([dist_ref[u0 + j - rb] for j in range(tk // 128)], axis=1)
                 for rb in range(tq // 128)], axis=0)
            bias = jnp.concatenate([slopes[0] * dist, slopes[1] * dist], axis=0)
            return jnp.concatenate([_dot_nt(qs0, k[:, :h]) - bias, _dot_nt(qs1, k[:, h:]) - bias], axis=0)

        def value(c, kvcols=kvcols):
            return v_ref[pl.ds(pl.multiple_of(c * tk, tk), tk), kvcols]

        def finish(acc, l, ocols=ocols):
            a = acc / l
            o = a[:2 * tq] - lam * a[2 * tq:]
            y = o * lax.rsqrt(jnp.mean(o * o, axis=-1, keepdims=True) + EPS) * subln_ref[...] * (1.0 - lam_init)
            o_ref[:, ocols] = _unstack_heads(y, 2).astype(o_ref.dtype)

        _pipeline_step(score, value, s_ref.at[kk], macc_ref.at[kk], finish, v_width=2 * h, tk=tk,
                       mxu_row_sums=False)


def _attn_c(qkv, slopes_log2, lam_params, subln, batch, seq, lam_init, tq=128, tk=1024, kv_per_step=2):
    assert tq % 128 == 0, "the distance table holds 128 x 128 tiles"
    tk = min(tk, seq)
    in_specs, out_spec = _attn_specs(seq, tq, D_MODEL, D_MODEL + N_KV_HEADS * 2 * HEAD_DIM, 2 * HEAD_DIM,
                                     kv_per_step)
    in_specs = [_SMEM] + in_specs + [
        pl.BlockSpec((4, HEAD_DIM), lambda b, kh, qi: (0, 0)),
        pl.BlockSpec((1, 2 * HEAD_DIM), lambda b, kh, qi: (0, 0)),
    ]
    return _pipelined_attention(
        "attn_diff", functools.partial(_attn_c_kernel, tq=tq, tk=tk, seq=seq, lam_init=lam_init),
        (slopes_log2, qkv, qkv, qkv, lam_params, subln), in_specs, out_spec, batch, seq, tq, tk, 4 * tq,
        kv_per_step, extra_scratch=[pltpu.VMEM((2 * (seq // 128) - 1, 128, 128), F32)])


def _combine_out_res_kernel(*refs):
    nb = len(DILATIONS)
    o_refs, l_refs = refs[:nb], refs[nb:2 * nb]
    w_ref, x_ref, gate_ref, out_ref = refs[2 * nb:2 * nb + 4]
    scratch = refs[2 * nb + 4:]
    tc = out_ref.shape[0]
    o_nat, l_nat = [], []
    for b, (_, dil) in enumerate(DILATIONS):
        if dil == 1:
            o_nat.append(o_refs[b][0].astype(F32))
            l_nat.append(l_refs[b][0])
            continue
        so, sl = scratch[2 * b], scratch[2 * b + 1]
        rows = tc // dil
        for r in range(dil):
            o_r, l_r = o_refs[b][r].astype(F32), l_refs[b][r]
            for jb in range(so.shape[0]):
                so[jb, pl.ds(r, rows, stride=dil), :] = o_r[:, jb * 128:(jb + 1) * 128]
            for jb in range(sl.shape[0]):
                sl[jb, pl.ds(r, rows, stride=dil), :] = l_r[:, jb * 128:(jb + 1) * 128]
        o_nat.append(jnp.concatenate([so[jb] for jb in range(so.shape[0])], axis=1))
        l_nat.append(jnp.concatenate([sl[jb] for jb in range(sl.shape[0])], axis=1))
    lanes = 128 // GQA_GROUP
    heads = []
    for h in range(N_HEADS):
        e = [l[:, h * lanes:h * lanes + 1] for l in l_nat]
        m = functools.reduce(jnp.maximum, e)
        w = [jnp.exp2(x - m) for x in e]
        cols = slice(h * HEAD_DIM, (h + 1) * HEAD_DIM)
        mix = sum(wi * o[:, cols] for wi, o in zip(w, o_nat))
        heads.append((mix / sum(w)).astype(BF16))
    out_ref[...] = x_ref[...] + gate_ref[0] * _dot(jnp.concatenate(heads, axis=1), w_ref[...])


def _attn_d_out_residual(qkv_classes, slopes, batch, seq, wo, x2, mod, row0, k_gate, tc=256):
    t = qkv_classes[0].shape[1]
    d = x2.shape[1]
    dummy_sink = jnp.zeros((N_HEADS,), F32)
    branches = [_attn_window(qkv_c, slopes, dummy_sink, batch, seq, window=(win // 2) // dil, use_sink=False)
                for qkv_c, (win, dil) in zip(qkv_classes, DILATIONS)]
    lw = N_KV_HEADS * 128
    o_specs = [pl.BlockSpec((dil, tc // dil, D_MODEL), lambda i: (0, i, 0)) for _, dil in DILATIONS]
    l_specs = [pl.BlockSpec((dil, tc // dil, lw), lambda i: (0, i, 0)) for _, dil in DILATIONS]
    scratch = []
    for _ in DILATIONS:
        scratch += [pltpu.VMEM((D_MODEL // 128, tc, 128), F32), pltpu.VMEM((lw // 128, tc, 128), F32)]
    return pl.pallas_call(
        _combine_out_res_kernel,
        grid=(t // tc,),
        in_specs=o_specs + l_specs + [
            pl.BlockSpec((D_MODEL, d), lambda i: (0, 0)),
            pl.BlockSpec((tc, d), lambda i: (i, 0)),
            pl.BlockSpec((1, 1, d), lambda i: (row0 + (i * tc) // seq, 0, k_gate)),
        ],
        out_specs=pl.BlockSpec((tc, d), lambda i: (i, 0)),
        out_shape=jax.ShapeDtypeStruct((t, d), F32),
        scratch_shapes=scratch,
        compiler_params=_cparams(("parallel",), 52),
        name="dilated_combine_out_proj",
    )(*[b[0] for b in branches], *[b[1] for b in branches], wo, x2, mod)


def _alibi_slopes(n):
    return 2.0 ** (-8.0 * jnp.arange(1, n + 1, dtype=F32) / n)


def _lambda_init(layer):
    return 0.8 - 0.6 * math.exp(-0.3 * layer)


def _rope_tables(seq):
    rows = seq // GRID_W
    row = jnp.repeat(jnp.arange(rows, dtype=F32), GRID_W)
    col = jnp.tile(jnp.arange(GRID_W, dtype=F32), rows)
    n_freq = HEAD_DIM // 4
    inv_freq = ROPE_THETA ** (-jnp.arange(n_freq, dtype=F32) / n_freq)
    ang = jnp.stack([row, col], axis=-1)[..., None] * inv_freq
    cos, sin = jnp.cos(ang), jnp.sin(ang)
    cos_t = jnp.concatenate([cos[:, 0], cos[:, 0], cos[:, 1], cos[:, 1]], axis=-1)
    sin_t = jnp.concatenate([-sin[:, 0], sin[:, 0], -sin[:, 1], sin[:, 1]], axis=-1)
    return cos_t, sin_t


def _q_colscale(scale, n_q_cols, n_cols):
    return jnp.concatenate([jnp.full((1, n_q_cols), scale, F32), jnp.ones((1, n_cols - n_q_cols), F32)], axis=1)


def _prepare_weights(p):
    pad = D_FF_PAD - D_FF
    w_up = p['ffn_w_up']
    w_up = jnp.concatenate([jnp.pad(w_up[..., :D_FF], ((0, 0), (0, 0), (0, pad))),
                            jnp.pad(w_up[..., D_FF:], ((0, 0), (0, 0), (0, pad)))], axis=-1).astype(BF16)
    return dict(
        a_wqkv=p['a_wqkv'].astype(BF16), a_wo=p['a_wo'].astype(BF16),
        b_wqkv=p['b_wqkv'].astype(BF16), b_wo=p['b_wo'].astype(BF16),
        c_wqkv=p['c_wqkv'].astype(BF16), c_wo=p['c_wo'].astype(BF16),
        d_wqkv=p['d_wqkv'].astype(BF16), d_wo=p['d_wo'].astype(BF16),
        ffn_w_up=w_up,
        ffn_w_down=jnp.pad(p['ffn_w_down'], ((0, 0), (0, pad), (0, 0))).astype(BF16),
        ffn_conv_w=jnp.pad(p['ffn_conv_w'], ((0, 0), (0, 0), (0, pad))),
        ffn_conv_b=jnp.pad(p['ffn_conv_b'], ((0, 0), (0, pad)))[:, None, :],
    )


def _trunk(x, mod_all, row0, p, w):
    batch, seq, d = x.shape
    x2 = x.reshape(batch * seq, d)
    slopes16, slopes8 = _alibi_slopes(N_HEADS), _alibi_slopes(DIFF_HEADS)
    n_qkv = (N_HEADS + 2 * N_KV_HEADS) * HEAD_DIM
    cs_plain = jnp.ones((1, n_qkv), F32)
    cs_q_log2 = _q_colscale(Q_SCALE_LOG2, N_HEADS * HEAD_DIM, n_qkv)
    cs_q_diff = _q_colscale(Q_SCALE_LOG2, DIFF_HEADS * 2 * HEAD_DIM, 2 * D_MODEL)
    for i in range(DEPTH):
        mod = mod_all[i][:, None, :]
        g_attn = p['norm_attn'][i][None, :]
        m, j = i % 4, i // 4
        if m == 0:
            qkv = _norm_matmul(x2, g_attn, mod, row0, seq, 1, 0, w['a_wqkv'][j], cs_plain)
            cos_t, sin_t = _rope_tables(seq)
            qk_norm = jnp.stack([p['a_q_norm'][j], p['a_k_norm'][j]])
            o = _attn_a(_prep_a(qkv, cos_t, sin_t, qk_norm, seq), qkv, batch, seq)
            wo = w['a_wo'][j]
        elif m == 1:
            qkv = _norm_matmul(x2, g_attn, mod, row0, seq, 1, 0, w['b_wqkv'][j], cs_q_log2)
            o, _ = _attn_window(qkv[None], slopes16, p['b_sink'][j].astype(F32), batch, seq, window=WINDOW,
                                use_sink=True)
            o = o[0]
            wo = w['b_wo'][j]
        elif m == 2:
            qkv = _norm_matmul(x2, g_attn, mod, row0, seq, 1, 0, w['c_wqkv'][j], cs_q_diff)
            o = _attn_c(qkv, slopes8 * LOG2E, p['c_lambda'][j].astype(F32), p['c_subln'][j][None, :],
                        batch, seq, _lambda_init(i))
            wo = w['c_wo'][j]
        else:
            dils = tuple(dil for _, dil in DILATIONS if dil > 1)
            qkv, *qkv_classes = _norm_matmul(x2, g_attn, mod, row0, seq, 1, 0, w['d_wqkv'][j], cs_q_log2, dils=dils)
            x2 = _attn_d_out_residual([qkv[None], *qkv_classes], slopes16, batch, seq, w['d_wo'][j], x2, mod, row0, 2)
        if m != 3:
            x2 = _out_residual(o, wo, x2, mod, row0, seq, 2)
        x2 = _ffn(x2, p['norm_ffn'][i][None, :], mod, row0, seq, w['ffn_w_up'][i], w['ffn_conv_w'][i],
                  w['ffn_conv_b'][i], w['ffn_w_down'][i], p['norm_final'][None, :], final_norm=(i == DEPTH - 1))
    return x2.reshape(batch, seq, d)


def kernel(x_prompt, x_sample, c_prompt, c_sample, norm_attn, norm_ffn, w_ada, b_ada, a_wqkv, a_q_norm, a_k_norm, a_wo, b_wqkv, b_sink, b_wo, c_wqkv, c_lambda, c_subln, c_wo, d_wqkv, d_wo, ffn_w_up, ffn_conv_w, ffn_conv_b, ffn_w_down, norm_final):
    p = dict(norm_attn=norm_attn, norm_ffn=norm_ffn, a_q_norm=a_q_norm, a_k_norm=a_k_norm, b_sink=b_sink,
             c_lambda=c_lambda, c_subln=c_subln, norm_final=norm_final,
             a_wqkv=a_wqkv, a_wo=a_wo, b_wqkv=b_wqkv, b_wo=b_wo, c_wqkv=c_wqkv, c_wo=c_wo,
             d_wqkv=d_wqkv, d_wo=d_wo, ffn_w_up=ffn_w_up, ffn_conv_w=ffn_conv_w, ffn_conv_b=ffn_conv_b,
             ffn_w_down=ffn_w_down)
    w = _prepare_weights(p)
    n_prompt, n_sample = c_prompt.shape[0], c_sample.shape[0]
    rows = -(-(n_prompt + n_sample) // 8) * 8
    c_all = jnp.concatenate([c_prompt, c_sample, jnp.zeros((rows - n_prompt - n_sample, D_MODEL), F32)], axis=0)
    mod_all = _modulation(c_all, w_ada, b_ada)
    y_prompt = _trunk(x_prompt, mod_all, 0, p, w)
    y_sample = _trunk(x_sample, mod_all, n_prompt, p, w)
    return (y_prompt, y_sample)
```
